```python
import math
import jax, jax.numpy as jnp
from jax import lax
import numpy as np

D_MODEL = 1024
BATCH = 8
SEQ = 2048
DEPTH = 2

GRID_W = 64
CTX_LEN = 256
HEAD_DIM = 64
N_GROUPS = 4
GROUP_W = D_MODEL // N_GROUPS
A_HEADS = GROUP_W // HEAD_DIM
A_KV_HEADS = A_HEADS // 2
A_WINDOW = 128
BLK = 128
CONV_CH = GROUP_W
CONV_K = 31
C_HEADS = GROUP_W // HEAD_DIM
C_QK_DIM = HEAD_DIM // 2
C_V_DIM = HEAD_DIM
D_HEADS = GROUP_W // HEAD_DIM
NA_KH = 8
NA_KW = 16
FFN_HIDDEN = -(-8 * D_MODEL // (3 * 256)) * 256

PROJ_SIZES = (A_HEADS * HEAD_DIM, A_KV_HEADS * HEAD_DIM, A_KV_HEADS * HEAD_DIM,
              2 * CONV_CH,
              C_HEADS * 2 * C_QK_DIM, C_HEADS * 2 * C_QK_DIM, C_HEADS * C_V_DIM,
              D_HEADS * HEAD_DIM, D_HEADS * HEAD_DIM, D_HEADS * HEAD_DIM)
IN_WIDTH = sum(PROJ_SIZES)
SPLIT_IDX = tuple(int(v) for v in np.cumsum(PROJ_SIZES)[:-1])
MIX_WIDTH = A_HEADS * HEAD_DIM + CONV_CH + C_HEADS * C_V_DIM + D_HEADS * HEAD_DIM
ROPE_BASE = 10000.0
EPS = 1e-6
NEG_INF = -1e30

kernel_name = "hybrid_parallel_groups_dit_block"


def rmsnorm(x, g):
    xf = x.astype(jnp.float32)
    r = lax.rsqrt(jnp.mean(xf * xf, axis=-1, keepdims=True) + EPS)
    return (xf * r).astype(x.dtype) * g


def layernorm(x, g, b):
    xf = x.astype(jnp.float32)
    mu = jnp.mean(xf, axis=-1, keepdims=True)
    var = jnp.mean(jnp.square(xf - mu), axis=-1, keepdims=True)
    return ((xf - mu) * lax.rsqrt(var + EPS)).astype(x.dtype) * g + b


def modulate(h, shift, scale):
    return h * (1 + scale) + shift


def axial_rope(n_tok, dim):
    t = jnp.arange(n_tok)
    row = (t // GRID_W).astype(jnp.float32)
    col = (t % GRID_W).astype(jnp.float32)
    nf = dim // 4
    inv = ROPE_BASE ** (-jnp.arange(nf, dtype=jnp.float32) / nf)
    ang = jnp.concatenate([row[:, None] * inv, col[:, None] * inv], axis=-1)
    return jnp.cos(ang), jnp.sin(ang)


def apply_rope(x, cos, sin):
    half = x.shape[-1] // 2
    shape = (1, x.shape[1]) + (1,) * (x.ndim - 3) + (half,)
    cs = cos.reshape(shape).astype(x.dtype)
    sn = sin.reshape(shape).astype(x.dtype)
    x1, x2 = x[..., :half], x[..., half:]
    return jnp.concatenate([x1 * cs - x2 * sn, x1 * sn + x2 * cs], axis=-1)


def project(h, w_in):
    b, n = h.shape[:2]
    qa, ka, va, ub, qc, kc, vc, qd, kd, vd = jnp.split(h @ w_in, SPLIT_IDX, axis=-1)
    return (qa.reshape(b, n, A_HEADS, HEAD_DIM), ka.reshape(b, n, A_KV_HEADS, HEAD_DIM),
            va.reshape(b, n, A_KV_HEADS, HEAD_DIM), ub,
            qc.reshape(b, n, C_HEADS, 2, C_QK_DIM), kc.reshape(b, n, C_HEADS, 2, C_QK_DIM),
            vc.reshape(b, n, C_HEADS, C_V_DIM),
            qd.reshape(b, n, D_HEADS, HEAD_DIM), kd.reshape(b, n, D_HEADS, HEAD_DIM),
            vd.reshape(b, n, D_HEADS, HEAD_DIM))


def ctx_attn(q, k, v, sink=None):
    b, l, h, d = q.shape
    g = k.shape[2]
    r = h // g
    qg = q.reshape(b, l, g, r, d)
    s = jnp.einsum('blgrd,bmgd->bgrlm', qg, k).astype(jnp.float32) * (d ** -0.5)
    if sink is not None:
        s_sink = jnp.broadcast_to(sink.astype(jnp.float32).reshape(1, g, r, 1, 1), s.shape[:-1] + (1,))
        s = jnp.concatenate([s, s_sink], axis=-1)
    p = jax.nn.softmax(s, axis=-1)[..., :l].astype(v.dtype)
    return jnp.einsum('bgrlm,bmgd->blgrd', p, v).reshape(b, l, h * d)


def window_gqa(q, k, v, kc, vc, sink):
    b, s_len, h, d = q.shape
    g = k.shape[2]
    r = h // g
    nb = s_len // BLK
    qb = q.reshape(b, nb, BLK, g, r, d)

    def band(t):
        tp = jnp.pad(t, ((0, 0), (BLK, BLK), (0, 0), (0, 0))).reshape(b, nb + 2, BLK, g, d)
        return jnp.concatenate([tp[:, :-2], tp[:, 1:-1], tp[:, 2:]], axis=2)

    kb, vb = band(k), band(v)
    scale = d ** -0.5
    s_loc = jnp.einsum('bnqgrd,bnkgd->bgrnqk', qb, kb).astype(jnp.float32) * scale
    s_ctx = jnp.einsum('bnqgrd,blgd->bgrnql', qb, kc).astype(jnp.float32) * scale
    blocks = jnp.arange(nb)[:, None, None] * BLK
    qpos = blocks + jnp.arange(BLK)[None, :, None]
    kpos = blocks + jnp.arange(3 * BLK)[None, None, :] - BLK
    valid = (jnp.abs(qpos - kpos) <= A_WINDOW) & (kpos >= 0) & (kpos < s_len)
    s_loc = jnp.where(valid, s_loc, NEG_INF)
    s_sink = jnp.broadcast_to(sink.astype(jnp.float32).reshape(1, g, r, 1, 1, 1), s_ctx.shape[:-1] + (1,))
    p = jax.nn.softmax(jnp.concatenate([s_loc, s_ctx, s_sink], axis=-1), axis=-1).astype(v.dtype)
    nk = 3 * BLK
    l = kc.shape[1]
    out = (jnp.einsum('bgrnqk,bnkgd->bnqgrd', p[..., :nk], vb)
           + jnp.einsum('bgrnql,blgd->bnqgrd', p[..., nk:nk + l], vc))
    return out.reshape(b, s_len, h * d)


def conformer_conv(u, w_dw, b_dw, ln_g, ln_b):
    a, gate = jnp.split(u, 2, axis=-1)
    h = a * jax.nn.sigmoid(gate)
    h = lax.conv_general_dilated(h, w_dw, window_strides=(1,),
                                 padding=((CONV_K // 2, CONV_K // 2),),
                                 dimension_numbers=('NWC', 'WIO', 'NWC'),
                                 feature_group_count=CONV_CH) + b_dw
    return jax.nn.silu(layernorm(h, ln_g, ln_b))


def diff_weights(q, k, v, lam):
    s = jnp.einsum('bqhmd,bkhmd->bhmqk', q, k).astype(jnp.float32) * (C_QK_DIM ** -0.5)
    p = jax.nn.softmax(s, axis=-1)
    w = (p[:, :, 0] - lam * p[:, :, 1]).astype(v.dtype)
    return jnp.einsum('bhqk,bkhd->bqhd', w, v)


def diff_out(o, subln_g, lambda_init):
    b, n, h, dv = o.shape
    return (rmsnorm(o, subln_g) * (1.0 - lambda_init)).reshape(b, n, h * dv)


def diff_attention_latent(q, k, v, kc, vc, lam):
    b, s_len = q.shape[:2]
    nb = s_len // BLK
    keys = jnp.concatenate([k, kc], axis=1)
    vals = jnp.concatenate([v, vc], axis=1)
    qb = jnp.moveaxis(q.reshape(b, nb, BLK, C_HEADS, 2, C_QK_DIM), 1, 0)
    o = lax.map(lambda qi: diff_weights(qi, keys, vals, lam), qb)
    return jnp.moveaxis(o, 0, 1).reshape(b, s_len, C_HEADS, C_V_DIM)


def neighbourhood_attn(q, k, v, kc, vc, rpb):
    b, s_len, h, d = q.shape
    rows = s_len // GRID_W
    kh = min(NA_KH, rows)
    qg = q.reshape(b, rows, GRID_W, h, d)
    kg = k.reshape(b, rows, GRID_W, h, d)
    vg = v.reshape(b, rows, GRID_W, h, d)
    r = jnp.arange(rows)
    rs = jnp.clip(r - kh // 2, 0, rows - kh)
    rows_idx = rs[:, None] + jnp.arange(kh)[None, :]
    kw_ = kg[:, rows_idx]
    vw_ = vg[:, rows_idx]
    cq = jnp.arange(GRID_W)
    cs = jnp.clip(cq - NA_KW // 2, 0, GRID_W - NA_KW)
    col_valid = (cq[None, :] >= cs[:, None]) & (cq[None, :] < cs[:, None] + NA_KW)
    dr = rows_idx - r[:, None] + (NA_KH - 1)
    dc = jnp.clip(cq[None, :] - cq[:, None], -(NA_KW - 1), NA_KW - 1) + (NA_KW - 1)
    bias = rpb.astype(jnp.float32)[:, dr[:, None, :, None], dc[None, :, None, :]]
    scale = d ** -0.5
    s_loc = jnp.einsum('brqhd,brkwhd->bhrqkw', qg, kw_).astype(jnp.float32) * scale + bias[None]
    s_loc = jnp.where(col_valid[:, None, :], s_loc, NEG_INF).reshape(b, h, rows, GRID_W, kh * GRID_W)
    s_ctx = jnp.einsum('brqhd,blhd->bhrql', qg, kc).astype(jnp.float32) * scale
    p = jax.nn.softmax(jnp.concatenate([s_loc, s_ctx], axis=-1), axis=-1).astype(v.dtype)
    nk = kh * GRID_W
    p_loc = p[..., :nk].reshape(b, h, rows, GRID_W, kh, GRID_W)
    out = (jnp.einsum('bhrqkw,brkwhd->brqhd', p_loc, vw_)
           + jnp.einsum('bhrql,blhd->brqhd', p[..., nk:], vc))
    return out.reshape(b, s_len, h * d)


def hybrid_mixer(hx, hc, w_in, w_out, sink, conv_w, conv_b, conv_ln_g, conv_ln_b,
                 lam, subln_g, lambda_init, rpb, rope_a, rope_c, ctx_out):
    qa, ka, va, ub, qc, kc, vc, qd, kd, vd = project(hx, w_in)
    qa_c, ka_c, va_c, ub_c, qc_c, kc_c, vc_c, qd_c, kd_c, vd_c = project(hc, w_in)
    qa, ka = apply_rope(qa, *rope_a), apply_rope(ka, *rope_a)
    qc, kc = apply_rope(qc, *rope_c), apply_rope(kc, *rope_c)
    y_a = window_gqa(qa, ka, va, ka_c, va_c, sink)
    y_b = conformer_conv(ub, conv_w, conv_b, conv_ln_g, conv_ln_b)
    y_c = diff_out(diff_attention_latent(qc, kc, vc, kc_c, vc_c, lam), subln_g, lambda_init)
    y_d = neighbourhood_attn(qd, kd, vd, kd_c, vd_c, rpb)
    yx = jnp.concatenate([y_a, y_b, y_c, y_d], axis=-1) @ w_out
    if not ctx_out:
        return yx, None
    yc_a = ctx_attn(qa_c, ka_c, va_c, sink)
    yc_b = conformer_conv(ub_c, conv_w, conv_b, conv_ln_g, conv_ln_b)
    yc_c = diff_out(diff_weights(qc_c, kc_c, vc_c, lam), subln_g, lambda_init)
    yc_d = ctx_attn(qd_c, kd_c, vd_c)
    yc = jnp.concatenate([yc_a, yc_b, yc_c, yc_d], axis=-1) @ w_out
    return yx, yc


def swiglu(h, w_gate, w_up, w_down):
    return (jax.nn.silu(h @ w_gate) * (h @ w_up)) @ w_down


def setup_inputs(seed: int = 0) -> dict:
    key = jax.random.key(seed)
    ks = jax.random.split(key, 25)
    nrm = lambda k, shape, s: jax.random.normal(k, shape, jnp.float32) * s
    D, F = D_MODEL, FFN_HIDDEN
    return {
        'x': nrm(ks[0], (BATCH, SEQ, D), 1.0),
        'c': nrm(ks[1], (BATCH, D), 1.0),
        'ctx': nrm(ks[2], (BATCH, CTX_LEN, D), 1.0),
        'c_ctx': nrm(ks[3], (D,), 1.0),
        'norm1_g': 1.0 + nrm(ks[4], (DEPTH, D), 0.1),
        'norm2_g': 1.0 + nrm(ks[5], (DEPTH, D), 0.1),
        'w_ada': nrm(ks[6], (DEPTH, D, 6 * D), 0.5 * D ** -0.5),
        'b_ada': nrm(ks[7], (DEPTH, 6 * D), 0.02),
        'w_in': nrm(ks[8], (DEPTH, D, IN_WIDTH), D ** -0.5),
        'w_out': nrm(ks[9], (DEPTH, MIX_WIDTH, D), MIX_WIDTH ** -0.5),
        'attn_sink': nrm(ks[10], (DEPTH, A_HEADS), 1.0),
        'conv_w': nrm(ks[11], (DEPTH, CONV_K, 1, CONV_CH), CONV_K ** -0.5),
        'conv_b': nrm(ks[12], (DEPTH, CONV_CH), 0.02),
        'conv_ln_g': 1.0 + nrm(ks[13], (DEPTH, CONV_CH), 0.1),
        'conv_ln_b': nrm(ks[14], (DEPTH, CONV_CH), 0.02),
        'diff_lq1': nrm(ks[15], (DEPTH, C_QK_DIM), 0.1),
        'diff_lk1': nrm(ks[16], (DEPTH, C_QK_DIM), 0.1),
        'diff_lq2': nrm(ks[17], (DEPTH, C_QK_DIM), 0.1),
        'diff_lk2': nrm(ks[18], (DEPTH, C_QK_DIM), 0.1),
        'diff_subln_g': 1.0 + nrm(ks[19], (DEPTH, C_V_DIM), 0.1),
        'na_rpb': nrm(ks[20], (DEPTH, D_HEADS, 2 * NA_KH - 1, 2 * NA_KW - 1), 0.1),
        'w_gate': nrm(ks[21], (DEPTH, D, F), D ** -0.5),
        'w_up': nrm(ks[22], (DEPTH, D, F), D ** -0.5),
        'w_down': nrm(ks[23], (DEPTH, F, D), F ** -0.5),
        'final_g': 1.0 + nrm(ks[24], (D,), 0.1),
    }


def reference(x, c, ctx, c_ctx, norm1_g, norm2_g, w_ada, b_ada, w_in, w_out, attn_sink,
              conv_w, conv_b, conv_ln_g, conv_ln_b, diff_lq1, diff_lk1, diff_lq2, diff_lk2,
              diff_subln_g, na_rpb, w_gate, w_up, w_down, final_g):
    s_len = x.shape[1]
    rope_a = axial_rope(s_len, HEAD_DIM)
    rope_c = axial_rope(s_len, C_QK_DIM)
    sc = jax.nn.silu(c)
    scc = jax.nn.silu(c_ctx)
    for l in range(DEPTH):
        ctx_needed = l < DEPTH - 1
        mx = sc @ w_ada[l] + b_ada[l]
        mc = scc @ w_ada[l] + b_ada[l]
        sh1, sc1, g1, sh2, sc2, g2 = jnp.split(mx[:, None, :], 6, axis=-1)
        csh1, csc1, cg1, csh2, csc2, cg2 = jnp.split(mc, 6, axis=-1)
        lambda_init = 0.8 - 0.6 * math.exp(-0.3 * l)
        lam = (jnp.exp(jnp.sum(diff_lq1[l].astype(jnp.float32) * diff_lk1[l].astype(jnp.float32)))
               - jnp.exp(jnp.sum(diff_lq2[l].astype(jnp.float32) * diff_lk2[l].astype(jnp.float32)))
               + lambda_init)
        hx = modulate(rmsnorm(x, norm1_g[l]), sh1, sc1)
        hc = modulate(rmsnorm(ctx, norm1_g[l]), csh1, csc1)
        yx, yc = hybrid_mixer(hx, hc, w_in[l], w_out[l], attn_sink[l], conv_w[l], conv_b[l],
                              conv_ln_g[l], conv_ln_b[l], lam, diff_subln_g[l], lambda_init,
                              na_rpb[l], rope_a, rope_c, ctx_needed)
        x = x + g1 * yx
        x = x + g2 * swiglu(modulate(rmsnorm(x, norm2_g[l]), sh2, sc2), w_gate[l], w_up[l], w_down[l])
        if ctx_needed:
            ctx = ctx + cg1 * yc
            ctx = ctx + cg2 * swiglu(modulate(rmsnorm(ctx, norm2_g[l]), csh2, csc2),
                                     w_gate[l], w_up[l], w_down[l])
    return rmsnorm(x, final_g)
```

```python
import functools
import math

import numpy as np
import jax
import jax.numpy as jnp
from jax import lax
from jax.experimental import pallas as pl
from jax.experimental.pallas import tpu as pltpu

F32 = jnp.float32
BF16 = jnp.bfloat16

D_MODEL = 1024
DEPTH = 2
GRID_W = 64
HEAD_DIM = 64
GROUP_W = 256
N_HEADS = 4
A_KV_HEADS = 2
A_WINDOW = 128
BLK = 128
CONV_K = 31
C_QK_DIM = 32
NA_KH = 8
NA_KW = 16
FFN_HIDDEN = 2816
ROPE_BASE = 10000.0
EPS = 1e-6
NEG_INF = -1e30

VMEM_LIMIT = 56 * 1024 * 1024
ADA_ROWS = 16
ADA_TN = 1536
ROW_TILE = 512
FFN_CHUNK = 256
CONV_TC = 64
CONV_HALO = 16
DIFF_TQ = 128


def _compiler_params():
    return pltpu.CompilerParams(vmem_limit_bytes=VMEM_LIMIT)


def _resident(shape):
    return pl.BlockSpec(shape, lambda *_: (0,) * len(shape), pipeline_mode=pl.Buffered(1))


def _sigmoid(v):
    return 1.0 / (1.0 + jnp.exp(-v))


def _dot(a, b):
    return jnp.dot(a, b, preferred_element_type=F32)


def _dot_nt(a, b):
    return lax.dot_general(a, b, (((1,), (1,)), ((), ())), preferred_element_type=F32)


def _ada_kernel(c_ref, w_ref, b_ref, o_ref):
    cv = c_ref[...]
    s = cv * _sigmoid(cv)
    o_ref[0] = _dot(s.astype(BF16), w_ref[0].astype(BF16)) + b_ref[0]


def _ada_table(cvec, w_ada, b_ada):
    depth, d, n = w_ada.shape
    return pl.pallas_call(
        _ada_kernel,
        grid=(depth, n // ADA_TN),
        in_specs=[
            pl.BlockSpec((ADA_ROWS, d), lambda l, j: (0, 0)),
            pl.BlockSpec((1, d, ADA_TN), lambda l, j: (l, 0, j)),
            pl.BlockSpec((1, 1, ADA_TN), lambda l, j: (l, 0, j)),
        ],
        out_specs=pl.BlockSpec((1, ADA_ROWS, ADA_TN), lambda l, j: (l, 0, j)),
        out_shape=jax.ShapeDtypeStruct((depth, ADA_ROWS, n), F32),
        compiler_params=_compiler_params(),
        name="ada_table",
    )(cvec, w_ada, b_ada.reshape(depth, 1, n))


def _projection_columns():
    lane = np.arange(GROUP_W)
    part, h, j = lane // 128, (lane % 128) // 32, lane % 32
    qa = h * HEAD_DIM + part * 32 + j
    ka = 256 + (h // 2) * HEAD_DIM + part * 32 + j
    va = 384 + ((lane // HEAD_DIM) // 2) * HEAD_DIM + lane % HEAD_DIM
    ub = 512 + np.arange(512)
    grp, jc = (lane % 128) // 16, lane % 16
    qc = 1024 + grp * C_QK_DIM + part * 16 + jc
    kc = qc + 256
    rest = 1536 + np.arange(4 * GROUP_W)
    return np.concatenate([qa, ka, va, ub, qc, kc, rest]).astype(np.int32)


PROJ_COLS = _projection_columns()
PROJ_WIDTH = PROJ_COLS.shape[0]
OFF_QA, OFF_KA, OFF_VA, OFF_UB, OFF_QC, OFF_KC, OFF_VC, OFF_QD, OFF_KD, OFF_VD = (
    0, 256, 512, 768, 1280, 1536, 1792, 2048, 2304, 2560)


def _inproj_kernel(rope, x_ref, g_ref, sh_ref, sc_ref, w_ref, *rest):
    if rope:
        ca_ref, sa_ref, cc_ref, sc2_ref = rest[:4]
        rest = rest[4:]
    qa_o, ka_o, va_o, ub_o, qc_o, kc_o, vc_o, qd_o, kd_o, vd_o = rest
    x = x_ref[...]
    r = lax.rsqrt(jnp.mean(x * x, axis=-1, keepdims=True) + EPS)
    h = (x * r) * g_ref[...]
    h = h * (1.0 + sc_ref[0]) + sh_ref[0]
    hb = h.astype(BF16)

    def proj(off, width=GROUP_W):
        return _dot(hb, w_ref[:, off:off + width])

    def store_rot(o_ref, y, c_ref, s_ref, scale):
        if rope:
            x1, x2 = y[:, :128], y[:, 128:]
            cs, sn = c_ref[...], s_ref[...]
            y1, y2 = x1 * cs - x2 * sn, x1 * sn + x2 * cs
        else:
            y1, y2 = y[:, :128], y[:, 128:]
        o_ref[:, :128] = (y1 * scale).astype(BF16)
        o_ref[:, 128:] = (y2 * scale).astype(BF16)

    ca = sa = cc = sc2 = None
    if rope:
        ca, sa, cc, sc2 = ca_ref, sa_ref, cc_ref, sc2_ref
    store_rot(qa_o, proj(OFF_QA), ca, sa, HEAD_DIM ** -0.5)
    store_rot(ka_o, proj(OFF_KA), ca, sa, 1.0)
    va_o[...] = proj(OFF_VA).astype(BF16)
    ub_o[...] = proj(OFF_UB, 512)
    store_rot(qc_o, proj(OFF_QC), cc, sc2, 1.0)
    store_rot(kc_o, proj(OFF_KC), cc, sc2, 1.0)
    vc_o[...] = proj(OFF_VC).astype(BF16)
    qd_o[...] = (proj(OFF_QD) * (HEAD_DIM ** -0.5)).astype(BF16)
    kd_o[...] = proj(OFF_KD).astype(BF16)
    vd_o[...] = proj(OFF_VD).astype(BF16)


def _in_projection(x, g, shift, scale, w, rope_tables, rows_per_mod):
    rows, d = x.shape
    tm = ROW_TILE
    tiles_per_mod = rows_per_mod // tm
    rope = rope_tables is not None
    mod_spec = pl.BlockSpec((1, 1, d), lambda i: (i // tiles_per_mod, 0, 0))
    in_specs = [
        pl.BlockSpec((tm, d), lambda i: (i, 0)),
        pl.BlockSpec((1, d), lambda i: (0, 0)),
        mod_spec, mod_spec,
        _resident((d, PROJ_WIDTH)),
    ]
    args = [x, g.reshape(1, d), shift, scale, w]
    if rope:
        tiles_per_seq = rope_tables[0].shape[0] // tm
        tab_spec = pl.BlockSpec((tm, 128), lambda i: (i % tiles_per_seq, 0))
        in_specs += [tab_spec] * 4
        args += list(rope_tables)
    narrow = pl.BlockSpec((tm, GROUP_W), lambda i: (i, 0))
    wide = pl.BlockSpec((tm, 512), lambda i: (i, 0))
    out_specs = [narrow, narrow, narrow, wide] + [narrow] * 6
    bf = jax.ShapeDtypeStruct((rows, GROUP_W), BF16)
    out_shape = [bf, bf, bf, jax.ShapeDtypeStruct((rows, 512), F32)] + [bf] * 6
    return pl.pallas_call(
        functools.partial(_inproj_kernel, rope),
        grid=(rows // tm,),
        in_specs=in_specs,
        out_specs=out_specs,
        out_shape=out_shape,
        compiler_params=_compiler_params(),
        name="in_projection",
    )(*args)


def _lane_iota():
    return lax.broadcasted_iota(jnp.int32, (1, GROUP_W), 1)


def _split_half_head(lane):
    return jnp.right_shift(jnp.bitwise_and(lane, 127), 5)


def _natural_head(lane):
    return jnp.right_shift(lane, 6)


def _stack_heads(q, lane_group, groups):
    zero = jnp.zeros_like(q)
    return jnp.concatenate([jnp.where(lane_group == g, q, zero) for g in groups], axis=0)


def _softmax_pv(scores, values, sink_col=None):
    m = functools.reduce(jnp.maximum, [jnp.max(s, axis=-1, keepdims=True) for s in scores])
    if sink_col is not None:
        m = jnp.maximum(m, sink_col)
    es = [jnp.exp(s - m) for s in scores]
    l = functools.reduce(jnp.add, [jnp.sum(e, axis=-1, keepdims=True) for e in es])
    if sink_col is not None:
        l = l + jnp.exp(sink_col - m)
    o = functools.reduce(jnp.add, [_dot(e.astype(BF16), v) for e, v in zip(es, values)])
    return o / l


def _select_heads(o, tq):
    head = _natural_head(_lane_iota())
    out = o[0:tq]
    for h in range(1, N_HEADS):
        out = jnp.where(head == h, o[h * tq:(h + 1) * tq], out)
    return out


def _sink_column(sink_ref, tq):
    row = lax.broadcasted_iota(jnp.int32, (N_HEADS * tq, 1), 0)
    col = jnp.full((N_HEADS * tq, 1), sink_ref[N_HEADS - 1], F32)
    for h in range(N_HEADS - 2, -1, -1):
        col = jnp.where(row < (h + 1) * tq, sink_ref[h], col)
    return col


def _win_attn_kernel(seq, sink_ref, q_ref, kl_ref, vl_ref, kc_ref, vc_ref, o_ref):
    n = pl.program_id(1)
    start = pl.multiple_of(jnp.clip((n - 1) * BLK, 0, seq - 3 * BLK), BLK)
    qs = _stack_heads(q_ref[0], _split_half_head(_lane_iota()), range(N_HEADS))
    kl = kl_ref[0, pl.ds(start, 3 * BLK), :]
    vl = vl_ref[0, pl.ds(start, 3 * BLK), :]
    s_loc = _dot_nt(qs, kl)
    s_ctx = _dot_nt(qs, kc_ref[0])
    row = lax.broadcasted_iota(jnp.int32, s_loc.shape, 0)
    col = lax.broadcasted_iota(jnp.int32, s_loc.shape, 1)
    qpos = n * BLK + jnp.bitwise_and(row, BLK - 1)
    kpos = start + col
    s_loc = jnp.where(jnp.abs(qpos - kpos) <= A_WINDOW, s_loc, NEG_INF)
    o = _softmax_pv([s_loc, s_ctx], [vl, vc_ref[0]], _sink_column(sink_ref, BLK))
    o_ref[0] = _select_heads(o, BLK).astype(BF16)


def _window_attention(sink, q, k, v, kc, vc):
    b, seq, w = q.shape
    ctx_len = kc.shape[1]
    whole = pl.BlockSpec((1, seq, w), lambda i, n: (i, 0, 0))
    ctx_spec = pl.BlockSpec((1, ctx_len, w), lambda i, n: (i, 0, 0))
    blk = pl.BlockSpec((1, BLK, w), lambda i, n: (i, n, 0))
    return pl.pallas_call(
        functools.partial(_win_attn_kernel, seq),
        grid=(b, seq // BLK),
        in_specs=[pl.BlockSpec(memory_space=pltpu.SMEM), blk, whole, whole, ctx_spec, ctx_spec],
        out_specs=blk,
        out_shape=jax.ShapeDtypeStruct((b, seq, w), BF16),
        compiler_params=_compiler_params(),
        name="window_attention",
    )(sink, q, k, v, kc, vc)


def _na_bias_table(rpb):
    cq = np.arange(GRID_W)
    cs = np.clip(cq - NA_KW // 2, 0, GRID_W - NA_KW)
    col_valid = (cq[None, :] >= cs[:, None]) & (cq[None, :] < cs[:, None] + NA_KW)
    dc = np.clip(cq[None, :] - cq[:, None], -(NA_KW - 1), NA_KW - 1) + (NA_KW - 1)
    dr = np.arange(NA_KH)[:, None] + np.arange(NA_KH)[None, :]
    t = rpb.astype(F32)[:, dr[:, None, :, None], dc[None, :, None, :]]
    t = jnp.where(col_valid[None, None, :, None, :], t, NEG_INF)
    return jnp.transpose(t, (1, 0, 2, 3, 4)).reshape(NA_KH, N_HEADS * GRID_W, NA_KH * GRID_W)


def _na_row_start(r, rows):
    return jnp.clip(r - NA_KH // 2, 0, rows - NA_KH)


def _na_attn_kernel(rows, q_ref, kl_ref, vl_ref, kc_ref, vc_ref, bias_ref, o_ref):
    r = pl.program_id(1)
    start = pl.multiple_of(_na_row_start(r, rows) * GRID_W, GRID_W)
    qs = _stack_heads(q_ref[0], _natural_head(_lane_iota()), range(N_HEADS))
    kl = kl_ref[0, pl.ds(start, NA_KH * GRID_W), :]
    vl = vl_ref[0, pl.ds(start, NA_KH * GRID_W), :]
    s_loc = _dot_nt(qs, kl) + bias_ref[0]
    s_ctx = _dot_nt(qs, kc_ref[0])
    o = _softmax_pv([s_loc, s_ctx], [vl, vc_ref[0]])
    o_ref[0] = _select_heads(o, GRID_W).astype(BF16)


def _neighbourhood_attention(bias, q, k, v, kc, vc):
    b, seq, w = q.shape
    rows = seq // GRID_W
    ctx_len = kc.shape[1]
    whole = pl.BlockSpec((1, seq, w), lambda i, r: (i, 0, 0))
    ctx_spec = pl.BlockSpec((1, ctx_len, w), lambda i, r: (i, 0, 0))
    blk = pl.BlockSpec((1, GRID_W, w), lambda i, r: (i, r, 0))
    bias_spec = pl.BlockSpec((1,) + bias.shape[1:],
                             lambda i, r: (_na_row_start(r, rows) - r + NA_KH - 1, 0, 0))
    return pl.pallas_call(
        functools.partial(_na_attn_kernel, rows),
        grid=(b, rows),
        in_specs=[blk, whole, whole, ctx_spec, ctx_spec, bias_spec],
        out_specs=blk,
        out_shape=jax.ShapeDtypeStruct((b, seq, w), BF16),
        compiler_params=_compiler_params(),
        name="neighbourhood_attention",
    )(q, k, v, kc, vc, bias)


def _ctx_attn_kernel(split_layout, has_sink, *refs):
    if has_sink:
        sink_ref, q_ref, k_ref, v_ref, o_ref = refs
    else:
        q_ref, k_ref, v_ref, o_ref = refs
    tq = q_ref.shape[1]
    lane = _lane_iota()
    group = _split_half_head(lane) if split_layout else _natural_head(lane)
    qs = _stack_heads(q_ref[0], group, range(N_HEADS))
    s = _dot_nt(qs, k_ref[0])
    sink_col = _sink_column(sink_ref, tq) if has_sink else None
    o = _softmax_pv([s], [v_ref[0]], sink_col)
    o_ref[0] = _select_heads(o, tq).astype(BF16)


def _context_attention(q, k, v, sink=None, split_layout=False):
    b, n, w = q.shape
    spec = pl.BlockSpec((1, n, w), lambda i: (i, 0, 0))
    in_specs = [spec, spec, spec]
    args = [q, k, v]
    if sink is not None:
        in_specs = [pl.BlockSpec(memory_space=pltpu.SMEM)] + in_specs
        args = [sink] + args
    return pl.pallas_call(
        functools.partial(_ctx_attn_kernel, split_layout, sink is not None),
        grid=(b,),
        in_specs=in_specs,
        out_specs=spec,
        out_shape=jax.ShapeDtypeStruct((b, n, w), BF16),
        compiler_params=_compiler_params(),
        name="context_attention",
    )(*args)


def _diff_attn_kernel(has_local, lambda_init, lam_ref, subg_ref, q_ref, *refs):
    if has_local:
        kl_ref, vl_ref, kc_ref, vc_ref, o_ref = refs
    else:
        kc_ref, vc_ref, o_ref = refs
    tq = q_ref.shape[1]
    lv = lam_ref[...]
    lam = (jnp.exp(jnp.sum(lv[0:1] * lv[1:2], axis=-1, keepdims=True))
           - jnp.exp(jnp.sum(lv[2:3] * lv[3:4], axis=-1, keepdims=True)) + lambda_init)
    q = q_ref[0]
    lane = _lane_iota()
    group = jnp.right_shift(jnp.bitwise_and(lane, 127), 4)
    head = _natural_head(lane)
    scale = C_QK_DIM ** -0.5
    keys = ([kl_ref] if has_local else []) + [kc_ref]
    vals = ([vl_ref] if has_local else []) + [vc_ref]

    def one_head(h, out):
        qs = _stack_heads(q, group, (2 * h, 2 * h + 1))
        scores = [_dot_nt(qs, k_ref[0]) * scale for k_ref in keys]
        m = functools.reduce(jnp.maximum, [jnp.max(s, axis=-1, keepdims=True) for s in scores])
        es = [jnp.exp(s - m) for s in scores]
        l = functools.reduce(jnp.add, [jnp.sum(e, axis=-1, keepdims=True) for e in es])
        inv = 1.0 / l
        c0, c1 = inv[:tq], lam * inv[tq:]
        o = functools.reduce(jnp.add, [
            _dot((e[:tq] * c0 - e[tq:] * c1).astype(BF16), v_ref[0]) for e, v_ref in zip(es, vals)])
        sel = head == h
        ms = jnp.sum(jnp.where(sel, o * o, 0.0), axis=-1, keepdims=True) * (1.0 / HEAD_DIM)
        return jnp.where(sel, o * lax.rsqrt(ms + EPS), out)

    out = lax.fori_loop(0, N_HEADS, one_head, jnp.zeros((tq, GROUP_W), F32))
    o_ref[0] = (out * subg_ref[...] * (1.0 - lambda_init)).astype(BF16)


def _diff_attention(lam_vecs, subg, lambda_init, q, kc, vc, k=None, v=None):
    b, n, w = q.shape
    ctx_len = kc.shape[1]
    has_local = k is not None
    tq = DIFF_TQ if has_local else n
    blk = pl.BlockSpec((1, tq, w), lambda i, j: (i, j, 0))
    ctx_spec = pl.BlockSpec((1, ctx_len, w), lambda i, j: (i, 0, 0))
    in_specs = [pl.BlockSpec(lam_vecs.shape, lambda i, j: (0, 0)),
                pl.BlockSpec((1, w), lambda i, j: (0, 0)), blk]
    args = [lam_vecs, subg, q]
    if has_local:
        whole = pl.BlockSpec((1, n, w), lambda i, j: (i, 0, 0))
        in_specs += [whole, whole]
        args += [k, v]
    in_specs += [ctx_spec, ctx_spec]
    args += [kc, vc]
    return pl.pallas_call(
        functools.partial(_diff_attn_kernel, has_local, lambda_init),
        grid=(b, n // tq),
        in_specs=in_specs,
        out_specs=blk,
        out_shape=jax.ShapeDtypeStruct((b, n, w), BF16),
        compiler_params=_compiler_params(),
        name="diff_attention",
    )(*args)


def _conv_kernel(seq, u_ref, w_ref, b_ref, g_ref, beta_ref, o_ref, pad_ref):
    halo, tc, ch = CONV_HALO, CONV_TC, GROUP_W
    zeros = jnp.zeros((halo, ch), F32)
    pad_ref[0:halo, :] = zeros
    pad_ref[halo + seq:2 * halo + seq, :] = zeros

    def glu(i, carry):
        r0 = pl.multiple_of(i * tc, tc)
        u = u_ref[0, pl.ds(r0, tc), :]
        pad_ref[pl.ds(halo + r0, tc), :] = u[:, :ch] * _sigmoid(u[:, ch:])
        return carry

    lax.fori_loop(0, seq // tc, glu, 0)

    win_rows = tc + 2 * halo

    def chunk(i, carry):
        c0 = pl.multiple_of(i * tc, tc)
        win = pad_ref[pl.ds(c0, win_rows), :]
        acc = jnp.zeros((tc, ch), F32)
        for sub in range(8):
            shifted = win if sub == 0 else pltpu.roll(win, win_rows - sub, axis=0)
            for blk8 in range(win_rows // 8):
                tap = 8 * blk8 + sub - (halo - CONV_K // 2)
                if 0 <= tap < CONV_K:
                    acc = acc + shifted[8 * blk8:8 * blk8 + tc] * w_ref[tap:tap + 1, :]
        hcv = acc + b_ref[...]
        mu = jnp.mean(hcv, axis=-1, keepdims=True)
        cen = hcv - mu
        var = jnp.mean(cen * cen, axis=-1, keepdims=True)
        y = cen * lax.rsqrt(var + EPS) * g_ref[...] + beta_ref[...]
        o_ref[0, pl.ds(c0, tc), :] = (y * _sigmoid(y)).astype(BF16)
        return carry

    lax.fori_loop(0, seq // tc, chunk, 0)


def _conformer_conv(u, w, bias, ln_g, ln_b):
    b, seq, two_ch = u.shape
    ch = two_ch // 2
    vec = pl.BlockSpec((1, ch), lambda i: (0, 0))
    return pl.pallas_call(
        functools.partial(_conv_kernel, seq),
        grid=(b,),
        in_specs=[pl.BlockSpec((1, seq, two_ch), lambda i: (i, 0, 0)),
                  pl.BlockSpec((CONV_K, ch), lambda i: (0, 0)), vec, vec, vec],
        out_specs=pl.BlockSpec((1, seq, ch), lambda i: (i, 0, 0)),
        out_shape=jax.ShapeDtypeStruct((b, seq, ch), BF16),
        scratch_shapes=[pltpu.VMEM((seq + 2 * CONV_HALO, ch), F32)],
        compiler_params=_compiler_params(),
        name="conformer_conv",
    )(u, w, bias.reshape(1, ch), ln_g.reshape(1, ch), ln_b.reshape(1, ch))


def _ffn_kernel(final, x_ref, ya_ref, yb_ref, yc_ref, yd_ref, g1_ref, sh_ref, sc_ref, g2_ref, ng_ref,
                wo_ref, wg_ref, wu_ref, wd_ref, *rest):
    if final:
        fg_ref, o_ref, ycat_ref, act_ref = rest
    else:
        o_ref, ycat_ref, act_ref = rest
    for j, y_ref in enumerate((ya_ref, yb_ref, yc_ref, yd_ref)):
        ycat_ref[:, j * GROUP_W:(j + 1) * GROUP_W] = y_ref[...]
    x = x_ref[...] + g1_ref[0] * _dot(ycat_ref[...], wo_ref[...])
    r = lax.rsqrt(jnp.mean(x * x, axis=-1, keepdims=True) + EPS)
    h = (x * r) * ng_ref[...]
    hb = (h * (1.0 + sc_ref[0]) + sh_ref[0]).astype(BF16)
    for c in range(0, FFN_HIDDEN, FFN_CHUNK):
        gate = _dot(hb, wg_ref[:, c:c + FFN_CHUNK])
        up = _dot(hb, wu_ref[:, c:c + FFN_CHUNK])
        act_ref[:, c:c + FFN_CHUNK] = (gate * _sigmoid(gate) * up).astype(BF16)
    x = x + g2_ref[0] * _dot(act_ref[...], wd_ref[...])
    if final:
        r = lax.rsqrt(jnp.mean(x * x, axis=-1, keepdims=True) + EPS)
        x = (x * r) * fg_ref[...]
    o_ref[...] = x


def _out_projection_ffn(x, ys, g1, shift, scale, g2, norm_g, wo, wg, wu, wd, rows_per_mod, final_g=None):
    rows, d = x.shape
    tm = ROW_TILE
    tiles_per_mod = rows_per_mod // tm
    final = final_g is not None
    row_spec = pl.BlockSpec((tm, d), lambda i: (i, 0))
    y_spec = pl.BlockSpec((tm, GROUP_W), lambda i: (i, 0))
    mod_spec = pl.BlockSpec((1, 1, d), lambda i: (i // tiles_per_mod, 0, 0))
    vec_spec = pl.BlockSpec((1, d), lambda i: (0, 0))
    in_specs = [row_spec, y_spec, y_spec, y_spec, y_spec, mod_spec, mod_spec, mod_spec, mod_spec, vec_spec,
                _resident(wo.shape), _resident(wg.shape), _resident(wu.shape), _resident(wd.shape)]
    args = [x, *ys, g1, shift, scale, g2, norm_g.reshape(1, d), wo, wg, wu, wd]
    if final:
        in_specs.append(vec_spec)
        args.append(final_g.reshape(1, d))
    return pl.pallas_call(
        functools.partial(_ffn_kernel, final),
        grid=(rows // tm,),
        in_specs=in_specs,
        out_specs=row_spec,
        out_shape=jax.ShapeDtypeStruct((rows, d), F32),
        scratch_shapes=[pltpu.VMEM((tm, d), BF16), pltpu.VMEM((tm, FFN_HIDDEN), BF16)],
        compiler_params=_compiler_params(),
        name="out_projection_ffn",
    )(*args)


def _rope_tables(n_tok, dim):
    t = jnp.arange(n_tok)
    row = (t // GRID_W).astype(F32)
    col = (t % GRID_W).astype(F32)
    nf = dim // 4
    inv = ROPE_BASE ** (-jnp.arange(nf, dtype=F32) / nf)
    ang = jnp.concatenate([row[:, None] * inv, col[:, None] * inv], axis=-1)
    reps = 128 // (dim // 2)
    return jnp.tile(jnp.cos(ang), (1, reps)), jnp.tile(jnp.sin(ang), (1, reps))


def kernel(x, c, ctx, c_ctx, norm1_g, norm2_g, w_ada, b_ada, w_in, w_out, attn_sink, conv_w, conv_b,
           conv_ln_g, conv_ln_b, diff_lq1, diff_lk1, diff_lq2, diff_lk2, diff_subln_g, na_rpb,
           w_gate, w_up, w_down, final_g):
    batch, seq, d = x.shape
    ctx_len = ctx.shape[1]
    depth = w_ada.shape[0]

    cvec = jnp.zeros((ADA_ROWS, d), F32).at[:batch].set(c).at[batch].set(c_ctx)
    mods = _ada_table(cvec, w_ada, b_ada).reshape(depth, ADA_ROWS, 6, d)
    rope = _rope_tables(seq, HEAD_DIM) + _rope_tables(seq, C_QK_DIM)

    xl = x.reshape(batch * seq, d)
    xc = ctx.reshape(batch * ctx_len, d)
    for l in range(depth):
        ctx_needed = l < depth - 1
        lat = [mods[l, :batch, k][:, None, :] for k in range(6)]
        cmod = [mods[l, batch:batch + 1, k][:, None, :] for k in range(6)]
        lambda_init = 0.8 - 0.6 * math.exp(-0.3 * l)
        w_in_l = w_in[l][:, PROJ_COLS].astype(BF16)
        wo, wg, wu, wd = (w_out[l].astype(BF16), w_gate[l].astype(BF16),
                          w_up[l].astype(BF16), w_down[l].astype(BF16))
        lam_vecs = jnp.stack([diff_lq1[l], diff_lk1[l], diff_lq2[l], diff_lk2[l]]).astype(F32)
        subg = jnp.tile(diff_subln_g[l], N_HEADS).reshape(1, GROUP_W)
        cw = conv_w[l].reshape(CONV_K, GROUP_W)

        pl_lat = _in_projection(xl, norm1_g[l], lat[0], lat[1], w_in_l, rope, seq)
        pl_ctx = _in_projection(xc, norm1_g[l], cmod[0], cmod[1], w_in_l, None, batch * ctx_len)
        qa, ka, va, ub, qc, kc, vc, qd, kd, vd = [t.reshape(batch, seq, -1) for t in pl_lat]
        qa_c, ka_c, va_c, ub_c, qc_c, kc_c, vc_c, qd_c, kd_c, vd_c = [
            t.reshape(batch, ctx_len, -1) for t in pl_ctx]

        y_a = _window_attention(attn_sink[l], qa, ka, va, ka_c, va_c)
        y_b = _conformer_conv(ub, cw, conv_b[l], conv_ln_g[l], conv_ln_b[l])
        y_c = _diff_attention(lam_vecs, subg, lambda_init, qc, kc_c, vc_c, kc, vc)
        y_d = _neighbourhood_attention(_na_bias_table(na_rpb[l]), qd, kd, vd, kd_c, vd_c)
        ys = [t.reshape(batch * seq, GROUP_W) for t in (y_a, y_b, y_c, y_d)]
        xl = _out_projection_ffn(xl, ys, lat[2], lat[3], lat[4], lat[5], norm2_g[l], wo, wg, wu, wd, seq,
                                 final_g=None if ctx_needed else final_g)
        if ctx_needed:
            yc_a = _context_attention(qa_c, ka_c, va_c, sink=attn_sink[l], split_layout=True)
            yc_b = _conformer_conv(ub_c, cw, conv_b[l], conv_ln_g[l], conv_ln_b[l])
            yc_c = _diff_attention(lam_vecs, subg, lambda_init, qc_c, kc_c, vc_c)
            yc_d = _context_attention(qd_c, kd_c, vd_c)
            ycs = [t.reshape(batch * ctx_len, GROUP_W) for t in (yc_a, yc_b, yc_c, yc_d)]
            xc = _out_projection_ffn(xc, ycs, cmod[2], cmod[3], cmod[4], cmod[5], norm2_g[l],
                                     wo, wg, wu, wd, batch * ctx_len)
    return xl.reshape(batch, seq, d)
```

```python
import functools
import math

import numpy as np
import jax
import jax.numpy as jnp
from jax import lax
from jax.experimental import pallas as pl
from jax.experimental.pallas import tpu as pltpu

F32 = jnp.float32
BF16 = jnp.bfloat16

D_MODEL = 1024
DEPTH = 2
GRID_W = 64
HEAD_DIM = 64
GROUP_W = 256
N_HEADS = 4
A_KV_HEADS = 2
A_WINDOW = 128
BLK = 128
CONV_K = 31
C_QK_DIM = 32
NA_KH = 8
NA_KW = 16
FFN_HIDDEN = 2816
ROPE_BASE = 10000.0
EPS = 1e-6
NEG_INF = -1e30

VMEM_LIMIT = 56 * 1024 * 1024
ADA_ROWS = 16
ADA_TN = 1536
ROW_TILE = 512
FFN_CHUNK = 256
CONV_TC = 64
CONV_HALO = 16
DIFF_TQ = 128
WIN_BLOCKS_PER_STEP = 2
NA_ROWS_PER_STEP = 4


def _compiler_params():
    return pltpu.CompilerParams(vmem_limit_bytes=VMEM_LIMIT)


def _resident(shape):
    return pl.BlockSpec(shape, lambda *_: (0,) * len(shape), pipeline_mode=pl.Buffered(1))


def _sigmoid(v):
    return 1.0 / (1.0 + jnp.exp(-v))


def _dot(a, b):
    return jnp.dot(a, b, preferred_element_type=F32)


def _dot_nt(a, b):
    return lax.dot_general(a, b, (((1,), (1,)), ((), ())), preferred_element_type=F32)


def _ada_kernel(c_ref, w_ref, b_ref, o_ref):
    cv = c_ref[...]
    s = cv * _sigmoid(cv)
    o_ref[0] = _dot(s.astype(BF16), w_ref[0].astype(BF16)) + b_ref[0]


def _ada_table(cvec, w_ada, b_ada):
    depth, d, n = w_ada.shape
    return pl.pallas_call(
        _ada_kernel,
        grid=(depth, n // ADA_TN),
        in_specs=[
            pl.BlockSpec((ADA_ROWS, d), lambda l, j: (0, 0)),
            pl.BlockSpec((1, d, ADA_TN), lambda l, j: (l, 0, j)),
            pl.BlockSpec((1, 1, ADA_TN), lambda l, j: (l, 0, j)),
        ],
        out_specs=pl.BlockSpec((1, ADA_ROWS, ADA_TN), lambda l, j: (l, 0, j)),
        out_shape=jax.ShapeDtypeStruct((depth, ADA_ROWS, n), F32),
        compiler_params=_compiler_params(),
        name="ada_table",
    )(cvec, w_ada, b_ada.reshape(depth, 1, n))


def _relayout_w_in(w):
    d = w.shape[0]
    w = w.astype(BF16)

    def split_halves(cols, groups, half):
        return jnp.transpose(cols.reshape(d, groups, 2, half), (0, 2, 1, 3)).reshape(d, GROUP_W)

    qa = split_halves(w[:, 0:256], N_HEADS, 32)
    ka = split_halves(jnp.repeat(w[:, 256:384].reshape(d, A_KV_HEADS, HEAD_DIM), 2, axis=1), N_HEADS, 32)
    va = jnp.repeat(w[:, 384:512].reshape(d, A_KV_HEADS, HEAD_DIM), 2, axis=1).reshape(d, GROUP_W)
    qc = split_halves(w[:, 1024:1280], 2 * N_HEADS, 16)
    kc = split_halves(w[:, 1280:1536], 2 * N_HEADS, 16)
    return jnp.concatenate([qa, ka, va, w[:, 512:1024], qc, kc, w[:, 1536:]], axis=1)


PROJ_WIDTH = 9 * GROUP_W + 512
OFF_QA, OFF_KA, OFF_VA, OFF_UB, OFF_QC, OFF_KC, OFF_VC, OFF_QD, OFF_KD, OFF_VD = (
    0, 256, 512, 768, 1280, 1536, 1792, 2048, 2304, 2560)


def _inproj_kernel(rope, x_ref, g_ref, sh_ref, sc_ref, w_ref, *rest):
    if rope:
        ca_ref, sa_ref, cc_ref, sc2_ref = rest[:4]
        rest = rest[4:]
    qa_o, ka_o, va_o, ub_o, qc_o, kc_o, vc_o, qd_o, kd_o, vd_o = rest
    x = x_ref[...]
    r = lax.rsqrt(jnp.mean(x * x, axis=-1, keepdims=True) + EPS)
    h = (x * r) * g_ref[...]
    h = h * (1.0 + sc_ref[0]) + sh_ref[0]
    hb = h.astype(BF16)

    def proj(off, width=GROUP_W):
        return _dot(hb, w_ref[:, off:off + width])

    def store_rot(o_ref, y, c_ref, s_ref, scale):
        if rope:
            x1, x2 = y[:, :128], y[:, 128:]
            cs, sn = c_ref[...], s_ref[...]
            y1, y2 = x1 * cs - x2 * sn, x1 * sn + x2 * cs
        else:
            y1, y2 = y[:, :128], y[:, 128:]
        o_ref[:, :128] = (y1 * scale).astype(BF16)
        o_ref[:, 128:] = (y2 * scale).astype(BF16)

    ca = sa = cc = sc2 = None
    if rope:
        ca, sa, cc, sc2 = ca_ref, sa_ref, cc_ref, sc2_ref
    store_rot(qa_o, proj(OFF_QA), ca, sa, HEAD_DIM ** -0.5)
    store_rot(ka_o, proj(OFF_KA), ca, sa, 1.0)
    va_o[...] = proj(OFF_VA).astype(BF16)
    ub_o[...] = proj(OFF_UB, 512)
    store_rot(qc_o, proj(OFF_QC), cc, sc2, 1.0)
    store_rot(kc_o, proj(OFF_KC), cc, sc2, 1.0)
    vc_o[...] = proj(OFF_VC).astype(BF16)
    qd_o[...] = (proj(OFF_QD) * (HEAD_DIM ** -0.5)).astype(BF16)
    kd_o[...] = proj(OFF_KD).astype(BF16)
    vd_o[...] = proj(OFF_VD).astype(BF16)


def _in_projection(x, g, shift, scale, w, rope_tables, rows_per_mod):
    rows, d = x.shape
    tm = ROW_TILE
    tiles_per_mod = rows_per_mod // tm
    rope = rope_tables is not None
    mod_spec = pl.BlockSpec((1, 1, d), lambda i: (i // tiles_per_mod, 0, 0))
    in_specs = [
        pl.BlockSpec((tm, d), lambda i: (i, 0)),
        pl.BlockSpec((1, d), lambda i: (0, 0)),
        mod_spec, mod_spec,
        _resident((d, PROJ_WIDTH)),
    ]
    args = [x, g.reshape(1, d), shift, scale, w]
    if rope:
        tiles_per_seq = rope_tables[0].shape[0] // tm
        tab_spec = pl.BlockSpec((tm, 128), lambda i: (i % tiles_per_seq, 0))
        in_specs += [tab_spec] * 4
        args += list(rope_tables)
    narrow = pl.BlockSpec((tm, GROUP_W), lambda i: (i, 0))
    wide = pl.BlockSpec((tm, 512), lambda i: (i, 0))
    out_specs = [narrow, narrow, narrow, wide] + [narrow] * 6
    bf = jax.ShapeDtypeStruct((rows, GROUP_W), BF16)
    out_shape = [bf, bf, bf, jax.ShapeDtypeStruct((rows, 512), F32)] + [bf] * 6
    return pl.pallas_call(
        functools.partial(_inproj_kernel, rope),
        grid=(rows // tm,),
        in_specs=in_specs,
        out_specs=out_specs,
        out_shape=out_shape,
        compiler_params=_compiler_params(),
        name="in_projection",
    )(*args)


def _lane_iota():
    return lax.broadcasted_iota(jnp.int32, (1, GROUP_W), 1)


def _split_half_head(lane):
    return jnp.right_shift(jnp.bitwise_and(lane, 127), 5)


def _natural_head(lane):
    return jnp.right_shift(lane, 6)


def _stack_heads(q, lane_group, groups):
    zero = jnp.zeros_like(q)
    return jnp.concatenate([jnp.where(lane_group == g, q, zero) for g in groups], axis=0)


def _softmax_pv(scores, values, sink_col=None):
    m = functools.reduce(jnp.maximum, [jnp.max(s, axis=-1, keepdims=True) for s in scores])
    if sink_col is not None:
        m = jnp.maximum(m, sink_col)
    es = [jnp.exp(s - m) for s in scores]
    l = functools.reduce(jnp.add, [jnp.sum(e, axis=-1, keepdims=True) for e in es])
    if sink_col is not None:
        l = l + jnp.exp(sink_col - m)
    o = functools.reduce(jnp.add, [_dot(e.astype(BF16), v) for e, v in zip(es, values)])
    return o / l


def _select_heads(o, tq):
    head = _natural_head(_lane_iota())
    out = o[0:tq]
    for h in range(1, N_HEADS):
        out = jnp.where(head == h, o[h * tq:(h + 1) * tq], out)
    return out


def _sink_column(sink_ref, tq):
    row = lax.broadcasted_iota(jnp.int32, (N_HEADS * tq, 1), 0)
    col = jnp.full((N_HEADS * tq, 1), sink_ref[N_HEADS - 1], F32)
    for h in range(N_HEADS - 2, -1, -1):
        col = jnp.where(row < (h + 1) * tq, sink_ref[h], col)
    return col


def _win_attn_kernel(seq, sink_ref, q_ref, kl_ref, vl_ref, kc_ref, vc_ref, o_ref):
    lane_head = _split_half_head(_lane_iota())
    sink_col = _sink_column(sink_ref, BLK)
    shape = (N_HEADS * BLK, 3 * BLK)
    rel = (lax.broadcasted_iota(jnp.int32, shape, 1)
           - jnp.bitwise_and(lax.broadcasted_iota(jnp.int32, shape, 0), BLK - 1))
    for t in range(WIN_BLOCKS_PER_STEP):
        n = pl.program_id(1) * WIN_BLOCKS_PER_STEP + t
        start = pl.multiple_of(jnp.clip((n - 1) * BLK, 0, seq - 3 * BLK), BLK)
        qs = _stack_heads(q_ref[0, t * BLK:(t + 1) * BLK, :], lane_head, range(N_HEADS))
        kl = kl_ref[0, pl.ds(start, 3 * BLK), :]
        vl = vl_ref[0, pl.ds(start, 3 * BLK), :]
        s_loc = _dot_nt(qs, kl)
        s_ctx = _dot_nt(qs, kc_ref[0])
        dist = rel + (start - n * BLK)
        s_loc = jnp.where(jnp.abs(dist) <= A_WINDOW, s_loc, NEG_INF)
        o = _softmax_pv([s_loc, s_ctx], [vl, vc_ref[0]], sink_col)
        o_ref[0, t * BLK:(t + 1) * BLK, :] = _select_heads(o, BLK).astype(BF16)


def _window_attention(sink, q, k, v, kc, vc):
    b, seq, w = q.shape
    ctx_len = kc.shape[1]
    tq = WIN_BLOCKS_PER_STEP * BLK
    whole = pl.BlockSpec((1, seq, w), lambda i, n: (i, 0, 0))
    ctx_spec = pl.BlockSpec((1, ctx_len, w), lambda i, n: (i, 0, 0))
    blk = pl.BlockSpec((1, tq, w), lambda i, n: (i, n, 0))
    return pl.pallas_call(
        functools.partial(_win_attn_kernel, seq),
        grid=(b, seq // tq),
        in_specs=[pl.BlockSpec(memory_space=pltpu.SMEM), blk, whole, whole, ctx_spec, ctx_spec],
        out_specs=blk,
        out_shape=jax.ShapeDtypeStruct((b, seq, w), BF16),
        compiler_params=_compiler_params(),
        name="window_attention",
    )(sink, q, k, v, kc, vc)


def _na_bias_table(rpb):
    cq = np.arange(GRID_W)
    cs = np.clip(cq - NA_KW // 2, 0, GRID_W - NA_KW)
    col_valid = (cq[None, :] >= cs[:, None]) & (cq[None, :] < cs[:, None] + NA_KW)
    dc = np.clip(cq[None, :] - cq[:, None], -(NA_KW - 1), NA_KW - 1) + (NA_KW - 1)
    dr = np.arange(NA_KH)[:, None] + np.arange(NA_KH)[None, :]
    t = rpb.astype(F32)[:, dr[:, None, :, None], dc[None, :, None, :]]
    t = jnp.where(col_valid[None, None, :, None, :], t, NEG_INF)
    return jnp.transpose(t, (1, 0, 2, 3, 4)).reshape(NA_KH, N_HEADS * GRID_W, NA_KH * GRID_W)


def _na_row_start(r, rows):
    return jnp.clip(r - NA_KH // 2, 0, rows - NA_KH)


def _na_attn_kernel(rows, q_ref, kl_ref, vl_ref, kc_ref, vc_ref, bias_ref, o_ref):
    lane_head = _natural_head(_lane_iota())
    for t in range(NA_ROWS_PER_STEP):
        r = pl.program_id(1) * NA_ROWS_PER_STEP + t
        first = _na_row_start(r, rows)
        start = pl.multiple_of(first * GRID_W, GRID_W)
        qs = _stack_heads(q_ref[0, t * GRID_W:(t + 1) * GRID_W, :], lane_head, range(N_HEADS))
        kl = kl_ref[0, pl.ds(start, NA_KH * GRID_W), :]
        vl = vl_ref[0, pl.ds(start, NA_KH * GRID_W), :]
        s_loc = _dot_nt(qs, kl) + bias_ref[first - r + NA_KH - 1]
        s_ctx = _dot_nt(qs, kc_ref[0])
        o = _softmax_pv([s_loc, s_ctx], [vl, vc_ref[0]])
        o_ref[0, t * GRID_W:(t + 1) * GRID_W, :] = _select_heads(o, GRID_W).astype(BF16)


def _neighbourhood_attention(bias, q, k, v, kc, vc):
    b, seq, w = q.shape
    rows = seq // GRID_W
    ctx_len = kc.shape[1]
    tq = NA_ROWS_PER_STEP * GRID_W
    whole = pl.BlockSpec((1, seq, w), lambda i, r: (i, 0, 0))
    ctx_spec = pl.BlockSpec((1, ctx_len, w), lambda i, r: (i, 0, 0))
    blk = pl.BlockSpec((1, tq, w), lambda i, r: (i, r, 0))
    return pl.pallas_call(
        functools.partial(_na_attn_kernel, rows),
        grid=(b, seq // tq),
        in_specs=[blk, whole, whole, ctx_spec, ctx_spec, _resident(bias.shape)],
        out_specs=blk,
        out_shape=jax.ShapeDtypeStruct((b, seq, w), BF16),
        compiler_params=_compiler_params(),
        name="neighbourhood_attention",
    )(q, k, v, kc, vc, bias)


def _ctx_attn_kernel(split_layout, has_sink, *refs):
    if has_sink:
        sink_ref, q_ref, k_ref, v_ref, o_ref = refs
    else:
        q_ref, k_ref, v_ref, o_ref = refs
    tq = q_ref.shape[1]
    lane = _lane_iota()
    group = _split_half_head(lane) if split_layout else _natural_head(lane)
    qs = _stack_heads(q_ref[0], group, range(N_HEADS))
    s = _dot_nt(qs, k_ref[0])
    sink_col = _sink_column(sink_ref, tq) if has_sink else None
    o = _softmax_pv([s], [v_ref[0]], sink_col)
    o_ref[0] = _select_heads(o, tq).astype(BF16)


def _context_attention(q, k, v, sink=None, split_layout=False):
    b, n, w = q.shape
    spec = pl.BlockSpec((1, n, w), lambda i: (i, 0, 0))
    in_specs = [spec, spec, spec]
    args = [q, k, v]
    if sink is not None:
        in_specs = [pl.BlockSpec(memory_space=pltpu.SMEM)] + in_specs
        args = [sink] + args
    return pl.pallas_call(
        functools.partial(_ctx_attn_kernel, split_layout, sink is not None),
        grid=(b,),
        in_specs=in_specs,
        out_specs=spec,
        out_shape=jax.ShapeDtypeStruct((b, n, w), BF16),
        compiler_params=_compiler_params(),
        name="context_attention",
    )(*args)


def _diff_attn_kernel(has_local, lambda_init, lam_ref, subg_ref, q_ref, *refs):
    if has_local:
        kl_ref, vl_ref, kc_ref, vc_ref, o_ref = refs
    else:
        kc_ref, vc_ref, o_ref = refs
    tq = q_ref.shape[1]
    lv = lam_ref[...]
    lam = (jnp.exp(jnp.sum(lv[0:1] * lv[1:2], axis=-1, keepdims=True))
           - jnp.exp(jnp.sum(lv[2:3] * lv[3:4], axis=-1, keepdims=True)) + lambda_init)
    q = q_ref[0]
    lane = _lane_iota()
    group = jnp.right_shift(jnp.bitwise_and(lane, 127), 4)
    head = _natural_head(lane)
    exp2_scale = (C_QK_DIM ** -0.5) * math.log2(math.e)
    keys = ([kl_ref] if has_local else []) + [kc_ref]
    vals = ([vl_ref] if has_local else []) + [vc_ref]

    out = None
    for h in range(N_HEADS):
        qs = _stack_heads(q, group, (2 * h, 2 * h + 1))
        scores = [_dot_nt(qs, k_ref[0]) for k_ref in keys]
        m = functools.reduce(jnp.maximum, [jnp.max(s, axis=-1, keepdims=True) for s in scores])
        es = [jnp.exp2((s - m) * exp2_scale) for s in scores]
        l = functools.reduce(jnp.add, [jnp.sum(e, axis=-1, keepdims=True) for e in es])
        pv = functools.reduce(jnp.add, [_dot(e.astype(BF16), v_ref[0]) for e, v_ref in zip(es, vals)])
        inv = 1.0 / l
        o = pv[:tq] * inv[:tq] - pv[tq:] * (lam * inv[tq:])
        sel = head == h
        ms = jnp.sum(jnp.where(sel, o * o, 0.0), axis=-1, keepdims=True) * (1.0 / HEAD_DIM)
        y = o * lax.rsqrt(ms + EPS)
        out = y if out is None else jnp.where(sel, y, out)
    o_ref[0] = (out * subg_ref[...] * (1.0 - lambda_init)).astype(BF16)


def _diff_attention(lam_vecs, subg, lambda_init, q, kc, vc, k=None, v=None):
    b, n, w = q.shape
    ctx_len = kc.shape[1]
    has_local = k is not None
    tq = DIFF_TQ if has_local else n
    blk = pl.BlockSpec((1, tq, w), lambda i, j: (i, j, 0))
    ctx_spec = pl.BlockSpec((1, ctx_len, w), lambda i, j: (i, 0, 0))
    in_specs = [pl.BlockSpec(lam_vecs.shape, lambda i, j: (0, 0)),
                pl.BlockSpec((1, w), lambda i, j: (0, 0)), blk]
    args = [lam_vecs, subg, q]
    if has_local:
        whole = pl.BlockSpec((1, n, w), lambda i, j: (i, 0, 0))
        in_specs += [whole, whole]
        args += [k, v]
    in_specs += [ctx_spec, ctx_spec]
    args += [kc, vc]
    return pl.pallas_call(
        functools.partial(_diff_attn_kernel, has_local, lambda_init),
        grid=(b, n // tq),
        in_specs=in_specs,
        out_specs=blk,
        out_shape=jax.ShapeDtypeStruct((b, n, w), BF16),
        compiler_params=_compiler_params(),
        name="diff_attention",
    )(*args)


def _conv_kernel(seq, u_ref, w_ref, b_ref, g_ref, beta_ref, o_ref, pad_ref):
    halo, tc, ch = CONV_HALO, CONV_TC, GROUP_W
    zeros = jnp.zeros((halo, ch), F32)
    pad_ref[0:halo, :] = zeros
    pad_ref[halo + seq:2 * halo + seq, :] = zeros

    def glu(i, carry):
        r0 = pl.multiple_of(i * tc, tc)
        u = u_ref[0, pl.ds(r0, tc), :]
        pad_ref[pl.ds(halo + r0, tc), :] = u[:, :ch] * _sigmoid(u[:, ch:])
        return carry

    lax.fori_loop(0, seq // tc, glu, 0)

    win_rows = tc + 2 * halo

    def chunk(i, carry):
        c0 = pl.multiple_of(i * tc, tc)
        win = pad_ref[pl.ds(c0, win_rows), :]
        acc = jnp.zeros((tc, ch), F32)
        for sub in range(8):
            shifted = win if sub == 0 else pltpu.roll(win, win_rows - sub, axis=0)
            for blk8 in range(win_rows // 8):
                tap = 8 * blk8 + sub - (halo - CONV_K // 2)
                if 0 <= tap < CONV_K:
                    acc = acc + shifted[8 * blk8:8 * blk8 + tc] * w_ref[tap:tap + 1, :]
        hcv = acc + b_ref[...]
        mu = jnp.mean(hcv, axis=-1, keepdims=True)
        cen = hcv - mu
        var = jnp.mean(cen * cen, axis=-1, keepdims=True)
        y = cen * lax.rsqrt(var + EPS) * g_ref[...] + beta_ref[...]
        o_ref[0, pl.ds(c0, tc), :] = (y * _sigmoid(y)).astype(BF16)
        return carry

    lax.fori_loop(0, seq // tc, chunk, 0)


def _conformer_conv(u, w, bias, ln_g, ln_b):
    b, seq, two_ch = u.shape
    ch = two_ch // 2
    vec = pl.BlockSpec((1, ch), lambda i: (0, 0))
    return pl.pallas_call(
        functools.partial(_conv_kernel, seq),
        grid=(b,),
        in_specs=[pl.BlockSpec((1, seq, two_ch), lambda i: (i, 0, 0)),
                  pl.BlockSpec((CONV_K, ch), lambda i: (0, 0)), vec, vec, vec],
        out_specs=pl.BlockSpec((1, seq, ch), lambda i: (i, 0, 0)),
        out_shape=jax.ShapeDtypeStruct((b, seq, ch), BF16),
        scratch_shapes=[pltpu.VMEM((seq + 2 * CONV_HALO, ch), F32)],
        compiler_params=_compiler_params(),
        name="conformer_conv",
    )(u, w, bias.reshape(1, ch), ln_g.reshape(1, ch), ln_b.reshape(1, ch))


def _ffn_kernel(final, x_ref, ya_ref, yb_ref, yc_ref, yd_ref, g1_ref, sh_ref, sc_ref, g2_ref, ng_ref,
                wo_ref, wg_ref, wu_ref, wd_ref, *rest):
    if final:
        fg_ref, o_ref, ycat_ref, act_ref = rest
    else:
        o_ref, ycat_ref, act_ref = rest
    for j, y_ref in enumerate((ya_ref, yb_ref, yc_ref, yd_ref)):
        ycat_ref[:, j * GROUP_W:(j + 1) * GROUP_W] = y_ref[...]
    x = x_ref[...] + g1_ref[0] * _dot(ycat_ref[...], wo_ref[...])
    r = lax.rsqrt(jnp.mean(x * x, axis=-1, keepdims=True) + EPS)
    h = (x * r) * ng_ref[...]
    hb = (h * (1.0 + sc_ref[0]) + sh_ref[0]).astype(BF16)
    for c in range(0, FFN_HIDDEN, FFN_CHUNK):
        gate = _dot(hb, wg_ref[:, c:c + FFN_CHUNK])
        up = _dot(hb, wu_ref[:, c:c + FFN_CHUNK])
        act_ref[:, c:c + FFN_CHUNK] = (gate * _sigmoid(gate) * up).astype(BF16)
    x = x + g2_ref[0] * _dot(act_ref[...], wd_ref[...])
    if final:
        r = lax.rsqrt(jnp.mean(x * x, axis=-1, keepdims=True) + EPS)
        x = (x * r) * fg_ref[...]
    o_ref[...] = x


def _out_projection_ffn(x, ys, g1, shift, scale, g2, norm_g, wo, wg, wu, wd, rows_per_mod, final_g=None):
    rows, d = x.shape
    tm = ROW_TILE
    tiles_per_mod = rows_per_mod // tm
    final = final_g is not None
    row_spec = pl.BlockSpec((tm, d), lambda i: (i, 0))
    y_spec = pl.BlockSpec((tm, GROUP_W), lambda i: (i, 0))
    mod_spec = pl.BlockSpec((1, 1, d), lambda i: (i // tiles_per_mod, 0, 0))
    vec_spec = pl.BlockSpec((1, d), lambda i: (0, 0))
    in_specs = [row_spec, y_spec, y_spec, y_spec, y_spec, mod_spec, mod_spec, mod_spec, mod_spec, vec_spec,
                _resident(wo.shape), _resident(wg.shape), _resident(wu.shape), _resident(wd.shape)]
    args = [x, *ys, g1, shift, scale, g2, norm_g.reshape(1, d), wo, wg, wu, wd]
    if final:
        in_specs.append(vec_spec)
        args.append(final_g.reshape(1, d))
    return pl.pallas_call(
        functools.partial(_ffn_kernel, final),
        grid=(rows // tm,),
        in_specs=in_specs,
        out_specs=row_spec,
        out_shape=jax.ShapeDtypeStruct((rows, d), F32),
        scratch_shapes=[pltpu.VMEM((tm, d), BF16), pltpu.VMEM((tm, FFN_HIDDEN), BF16)],
        compiler_params=_compiler_params(),
        name="out_projection_ffn",
    )(*args)


def _rope_tables(n_tok, dim):
    t = jnp.arange(n_tok)
    row = (t // GRID_W).astype(F32)
    col = (t % GRID_W).astype(F32)
    nf = dim // 4
    inv = ROPE_BASE ** (-jnp.arange(nf, dtype=F32) / nf)
    ang = jnp.concatenate([row[:, None] * inv, col[:, None] * inv], axis=-1)
    reps = 128 // (dim // 2)
    return jnp.tile(jnp.cos(ang), (1, reps)), jnp.tile(jnp.sin(ang), (1, reps))


def kernel(x, c, ctx, c_ctx, norm1_g, norm2_g, w_ada, b_ada, w_in, w_out, attn_sink, conv_w, conv_b,
           conv_ln_g, conv_ln_b, diff_lq1, diff_lk1, diff_lq2, diff_lk2, diff_subln_g, na_rpb,
           w_gate, w_up, w_down, final_g):
    batch, seq, d = x.shape
    ctx_len = ctx.shape[1]
    depth = w_ada.shape[0]

    cvec = jnp.zeros((ADA_ROWS, d), F32).at[:batch].set(c).at[batch].set(c_ctx)
    mods = _ada_table(cvec, w_ada, b_ada).reshape(depth, ADA_ROWS, 6, d)
    rope = _rope_tables(seq, HEAD_DIM) + _rope_tables(seq, C_QK_DIM)

    xl = x.reshape(batch * seq, d)
    xc = ctx.reshape(batch * ctx_len, d)
    for l in range(depth):
        ctx_needed = l < depth - 1
        lat = [mods[l, :batch, k][:, None, :] for k in range(6)]
        cmod = [mods[l, batch:batch + 1, k][:, None, :] for k in range(6)]
        lambda_init = 0.8 - 0.6 * math.exp(-0.3 * l)
        w_in_l = _relayout_w_in(w_in[l])
        wo, wg, wu, wd = (w_out[l].astype(BF16), w_gate[l].astype(BF16),
                          w_up[l].astype(BF16), w_down[l].astype(BF16))
        lam_vecs = jnp.stack([diff_lq1[l], diff_lk1[l], diff_lq2[l], diff_lk2[l]]).astype(F32)
        subg = jnp.tile(diff_subln_g[l], N_HEADS).reshape(1, GROUP_W)
        cw = conv_w[l].reshape(CONV_K, GROUP_W)

        pl_lat = _in_projection(xl, norm1_g[l], lat[0], lat[1], w_in_l, rope, seq)
        pl_ctx = _in_projection(xc, norm1_g[l], cmod[0], cmod[1], w_in_l, None, batch * ctx_len)
        qa, ka, va, ub, qc, kc, vc, qd, kd, vd = [t.reshape(batch, seq, -1) for t in pl_lat]
        qa_c, ka_c, va_c, ub_c, qc_c, kc_c, vc_c, qd_c, kd_c, vd_c = [
            t.reshape(batch, ctx_len, -1) for t in pl_ctx]

        y_a = _window_attention(attn_sink[l], qa, ka, va, ka_c, va_c)
        y_b = _conformer_conv(ub, cw, conv_b[l], conv_ln_g[l], conv_ln_b[l])
        y_c = _diff_attention(lam_vecs, subg, lambda_init, qc, kc_c, vc_c, kc, vc)
        y_d = _neighbourhood_attention(_na_bias_table(na_rpb[l]), qd, kd, vd, kd_c, vd_c)
        ys = [t.reshape(batch * seq, GROUP_W) for t in (y_a, y_b, y_c, y_d)]
        xl = _out_projection_ffn(xl, ys, lat[2], lat[3], lat[4], lat[5], norm2_g[l], wo, wg, wu, wd, seq,
                                 final_g=None if ctx_needed else final_g)
        if ctx_needed:
            yc_a = _context_attention(qa_c, ka_c, va_c, sink=attn_sink[l], split_layout=True)
            yc_b = _conformer_conv(ub_c, cw, conv_b[l], conv_ln_g[l], conv_ln_b[l])
            yc_c = _diff_attention(lam_vecs, subg, lambda_init, qc_c, kc_c, vc_c)
            yc_d = _context_attention(qd_c, kd_c, vd_c)
            ycs = [t.reshape(batch * ctx_len, GROUP_W) for t in (yc_a, yc_b, yc_c, yc_d)]
            xc = _out_projection_ffn(xc, ycs, cmod[2], cmod[3], cmod[4], cmod[5], norm2_g[l],
                                     wo, wg, wu, wd, batch * ctx_len)
    return xl.reshape(batch, seq, d)
```

```python
import functools
import math

import numpy as np
import jax
import jax.numpy as jnp
from jax import lax
from jax.experimental import pallas as pl
from jax.experimental.pallas import tpu as pltpu

F32 = jnp.float32
BF16 = jnp.bfloat16

D_MODEL = 1024
DEPTH = 2
GRID_W = 64
HEAD_DIM = 64
GROUP_W = 256
N_HEADS = 4
A_KV_HEADS = 2
A_WINDOW = 128
BLK = 128
CONV_K = 31
C_QK_DIM = 32
NA_KH = 8
NA_KW = 16
FFN_HIDDEN = 2816
ROPE_BASE = 10000.0
EPS = 1e-6
NEG_INF = -1e30

VMEM_LIMIT = 56 * 1024 * 1024
ADA_ROWS = 16
ADA_TN = 1536
ROW_TILE = 512
FFN_CHUNK = 256
CONV_TC = 64
CONV_HALO = 16
DIFF_TQ = 128
WIN_BLOCKS_PER_STEP = 2
NA_ROWS_PER_STEP = 4


def _compiler_params():
    return pltpu.CompilerParams(vmem_limit_bytes=VMEM_LIMIT)


def _resident(shape):
    return pl.BlockSpec(shape, lambda *_: (0,) * len(shape), pipeline_mode=pl.Buffered(1))


def _sigmoid(v):
    return 1.0 / (1.0 + jnp.exp(-v))


def _dot(a, b):
    return jnp.dot(a, b, preferred_element_type=F32)


def _dot_nt(a, b):
    return lax.dot_general(a, b, (((1,), (1,)), ((), ())), preferred_element_type=F32)


def _ada_kernel(c_ref, w_ref, b_ref, o_ref):
    cv = c_ref[...]
    s = cv * _sigmoid(cv)
    o_ref[0] = _dot(s.astype(BF16), w_ref[0].astype(BF16)) + b_ref[0]


def _ada_table(cvec, w_ada, b_ada):
    depth, d, n = w_ada.shape
    return pl.pallas_call(
        _ada_kernel,
        grid=(depth, n // ADA_TN),
        in_specs=[
            pl.BlockSpec((ADA_ROWS, d), lambda l, j: (0, 0)),
            pl.BlockSpec((1, d, ADA_TN), lambda l, j: (l, 0, j)),
            pl.BlockSpec((1, 1, ADA_TN), lambda l, j: (l, 0, j)),
        ],
        out_specs=pl.BlockSpec((1, ADA_ROWS, ADA_TN), lambda l, j: (l, 0, j)),
        out_shape=jax.ShapeDtypeStruct((depth, ADA_ROWS, n), F32),
        compiler_params=_compiler_params(),
        name="ada_table",
    )(cvec, w_ada, b_ada.reshape(depth, 1, n))


def _relayout_w_in(w):
    d = w.shape[0]
    w = w.astype(BF16)

    def split_halves(cols, groups, half):
        return jnp.transpose(cols.reshape(d, groups, 2, half), (0, 2, 1, 3)).reshape(d, GROUP_W)

    qa = split_halves(w[:, 0:256], N_HEADS, 32)
    ka = split_halves(jnp.repeat(w[:, 256:384].reshape(d, A_KV_HEADS, HEAD_DIM), 2, axis=1), N_HEADS, 32)
    va = jnp.repeat(w[:, 384:512].reshape(d, A_KV_HEADS, HEAD_DIM), 2, axis=1).reshape(d, GROUP_W)
    qc = split_halves(w[:, 1024:1280], 2 * N_HEADS, 16)
    kc = split_halves(w[:, 1280:1536], 2 * N_HEADS, 16)
    return jnp.concatenate([qa, ka, va, w[:, 512:1024], qc, kc, w[:, 1536:]], axis=1)


PROJ_WIDTH = 9 * GROUP_W + 512
OFF_QA, OFF_KA, OFF_VA, OFF_UB, OFF_QC, OFF_KC, OFF_VC, OFF_QD, OFF_KD, OFF_VD = (
    0, 256, 512, 768, 1280, 1536, 1792, 2048, 2304, 2560)


def _inproj_kernel(rope, x_ref, g_ref, sh_ref, sc_ref, w_ref, *rest):
    if rope:
        ca_ref, sa_ref, cc_ref, sc2_ref = rest[:4]
        rest = rest[4:]
    qa_o, ka_o, va_o, ub_o, qc_o, kc_o, vc_o, qd_o, kd_o, vd_o = rest
    x = x_ref[...]
    r = lax.rsqrt(jnp.mean(x * x, axis=-1, keepdims=True) + EPS)
    h = (x * r) * g_ref[...]
    h = h * (1.0 + sc_ref[0]) + sh_ref[0]
    hb = h.astype(BF16)

    def proj(off, width=GROUP_W):
        return _dot(hb, w_ref[:, off:off + width])

    def store_rot(o_ref, y, c_ref, s_ref, scale):
        if rope:
            x1, x2 = y[:, :128], y[:, 128:]
            cs, sn = c_ref[...], s_ref[...]
            y1, y2 = x1 * cs - x2 * sn, x1 * sn + x2 * cs
        else:
            y1, y2 = y[:, :128], y[:, 128:]
        o_ref[:, :128] = (y1 * scale).astype(BF16)
        o_ref[:, 128:] = (y2 * scale).astype(BF16)

    ca = sa = cc = sc2 = None
    if rope:
        ca, sa, cc, sc2 = ca_ref, sa_ref, cc_ref, sc2_ref
    store_rot(qa_o, proj(OFF_QA), ca, sa, HEAD_DIM ** -0.5)
    store_rot(ka_o, proj(OFF_KA), ca, sa, 1.0)
    va_o[...] = proj(OFF_VA).astype(BF16)
    ub_o[...] = proj(OFF_UB, 512)
    store_rot(qc_o, proj(OFF_QC), cc, sc2, 1.0)
    store_rot(kc_o, proj(OFF_KC), cc, sc2, 1.0)
    vc_o[...] = proj(OFF_VC).astype(BF16)
    qd_o[...] = (proj(OFF_QD) * (HEAD_DIM ** -0.5)).astype(BF16)
    kd_o[...] = proj(OFF_KD).astype(BF16)
    vd_o[...] = proj(OFF_VD).astype(BF16)


def _in_projection(x, g, shift, scale, w, rope_tables, rows_per_mod):
    rows, d = x.shape
    tm = ROW_TILE
    tiles_per_mod = rows_per_mod // tm
    rope = rope_tables is not None
    mod_spec = pl.BlockSpec((1, 1, d), lambda i: (i // tiles_per_mod, 0, 0))
    in_specs = [
        pl.BlockSpec((tm, d), lambda i: (i, 0)),
        pl.BlockSpec((1, d), lambda i: (0, 0)),
        mod_spec, mod_spec,
        _resident((d, PROJ_WIDTH)),
    ]
    args = [x, g.reshape(1, d), shift, scale, w]
    if rope:
        tiles_per_seq = rope_tables[0].shape[0] // tm
        tab_spec = pl.BlockSpec((tm, 128), lambda i: (i % tiles_per_seq, 0))
        in_specs += [tab_spec] * 4
        args += list(rope_tables)
    narrow = pl.BlockSpec((tm, GROUP_W), lambda i: (i, 0))
    wide = pl.BlockSpec((tm, 512), lambda i: (i, 0))
    out_specs = [narrow, narrow, narrow, wide] + [narrow] * 6
    bf = jax.ShapeDtypeStruct((rows, GROUP_W), BF16)
    out_shape = [bf, bf, bf, jax.ShapeDtypeStruct((rows, 512), F32)] + [bf] * 6
    return pl.pallas_call(
        functools.partial(_inproj_kernel, rope),
        grid=(rows // tm,),
        in_specs=in_specs,
        out_specs=out_specs,
        out_shape=out_shape,
        compiler_params=_compiler_params(),
        name="in_projection",
    )(*args)


def _lane_iota():
    return lax.broadcasted_iota(jnp.int32, (1, GROUP_W), 1)


def _split_half_head(lane):
    return jnp.right_shift(jnp.bitwise_and(lane, 127), 5)


def _natural_head(lane):
    return jnp.right_shift(lane, 6)


def _stack_heads(q, lane_group, groups):
    zero = jnp.zeros_like(q)
    return jnp.concatenate([jnp.where(lane_group == g, q, zero) for g in groups], axis=0)


def _softmax_pv(scores, values, sink_col=None):
    m = functools.reduce(jnp.maximum, [jnp.max(s, axis=-1, keepdims=True) for s in scores])
    if sink_col is not None:
        m = jnp.maximum(m, sink_col)
    es = [jnp.exp(s - m) for s in scores]
    l = functools.reduce(jnp.add, [jnp.sum(e, axis=-1, keepdims=True) for e in es])
    if sink_col is not None:
        l = l + jnp.exp(sink_col - m)
    o = functools.reduce(jnp.add, [_dot(e.astype(BF16), v) for e, v in zip(es, values)])
    return o / l


def _select_heads(o, tq):
    head = _natural_head(_lane_iota())
    out = o[0:tq]
    for h in range(1, N_HEADS):
        out = jnp.where(head == h, o[h * tq:(h + 1) * tq], out)
    return out


def _sink_column(sink_ref, tq):
    row = lax.broadcasted_iota(jnp.int32, (N_HEADS * tq, 1), 0)
    col = jnp.full((N_HEADS * tq, 1), sink_ref[N_HEADS - 1], F32)
    for h in range(N_HEADS - 2, -1, -1):
        col = jnp.where(row < (h + 1) * tq, sink_ref[h], col)
    return col


def _win_attn_kernel(seq, sink_ref, q_ref, kl_ref, vl_ref, kc_ref, vc_ref, o_ref):
    lane_head = _split_half_head(_lane_iota())
    sink_col = _sink_column(sink_ref, BLK)
    shape = (N_HEADS * BLK, 3 * BLK)
    rel = (lax.broadcasted_iota(jnp.int32, shape, 1)
           - jnp.bitwise_and(lax.broadcasted_iota(jnp.int32, shape, 0), BLK - 1))
    for t in range(WIN_BLOCKS_PER_STEP):
        n = pl.program_id(1) * WIN_BLOCKS_PER_STEP + t
        start = pl.multiple_of(jnp.clip((n - 1) * BLK, 0, seq - 3 * BLK), BLK)
        qs = _stack_heads(q_ref[0, t * BLK:(t + 1) * BLK, :], lane_head, range(N_HEADS))
        kl = kl_ref[0, pl.ds(start, 3 * BLK), :]
        vl = vl_ref[0, pl.ds(start, 3 * BLK), :]
        s_loc = _dot_nt(qs, kl)
        s_ctx = _dot_nt(qs, kc_ref[0])
        dist = rel + (start - n * BLK)
        s_loc = jnp.where(jnp.abs(dist) <= A_WINDOW, s_loc, NEG_INF)
        o = _softmax_pv([s_loc, s_ctx], [vl, vc_ref[0]], sink_col)
        o_ref[0, t * BLK:(t + 1) * BLK, :] = _select_heads(o, BLK).astype(BF16)


def _window_attention(sink, q, k, v, kc, vc):
    b, seq, w = q.shape
    ctx_len = kc.shape[1]
    tq = WIN_BLOCKS_PER_STEP * BLK
    whole = pl.BlockSpec((1, seq, w), lambda i, n: (i, 0, 0))
    ctx_spec = pl.BlockSpec((1, ctx_len, w), lambda i, n: (i, 0, 0))
    blk = pl.BlockSpec((1, tq, w), lambda i, n: (i, n, 0))
    return pl.pallas_call(
        functools.partial(_win_attn_kernel, seq),
        grid=(b, seq // tq),
        in_specs=[pl.BlockSpec(memory_space=pltpu.SMEM), blk, whole, whole, ctx_spec, ctx_spec],
        out_specs=blk,
        out_shape=jax.ShapeDtypeStruct((b, seq, w), BF16),
        compiler_params=_compiler_params(),
        name="window_attention",
    )(sink, q, k, v, kc, vc)


def _na_bias_table(rpb):
    cq = np.arange(GRID_W)
    cs = np.clip(cq - NA_KW // 2, 0, GRID_W - NA_KW)
    col_valid = (cq[None, :] >= cs[:, None]) & (cq[None, :] < cs[:, None] + NA_KW)
    dc = np.clip(cq[None, :] - cq[:, None], -(NA_KW - 1), NA_KW - 1) + (NA_KW - 1)
    n_dr, n_dc = 2 * NA_KH - 1, 2 * NA_KW - 1
    onehot = (dc.reshape(1, -1) == np.arange(n_dc)[:, None]).astype(np.float32)
    t = jnp.dot(rpb.astype(F32).reshape(N_HEADS * n_dr, n_dc), onehot, precision=lax.Precision.HIGHEST)
    t = jnp.where(col_valid[None, None], t.reshape(N_HEADS, n_dr, GRID_W, GRID_W), NEG_INF)
    tab = jnp.stack([t[:, off:off + NA_KH] for off in range(NA_KH)])
    return jnp.transpose(tab, (0, 1, 3, 2, 4)).reshape(NA_KH, N_HEADS * GRID_W, NA_KH * GRID_W)


def _na_row_start(r, rows):
    return jnp.clip(r - NA_KH // 2, 0, rows - NA_KH)


def _na_attn_kernel(rows, q_ref, kl_ref, vl_ref, kc_ref, vc_ref, bias_ref, o_ref):
    lane_head = _natural_head(_lane_iota())
    for t in range(NA_ROWS_PER_STEP):
        r = pl.program_id(1) * NA_ROWS_PER_STEP + t
        first = _na_row_start(r, rows)
        start = pl.multiple_of(first * GRID_W, GRID_W)
        qs = _stack_heads(q_ref[0, t * GRID_W:(t + 1) * GRID_W, :], lane_head, range(N_HEADS))
        kl = kl_ref[0, pl.ds(start, NA_KH * GRID_W), :]
        vl = vl_ref[0, pl.ds(start, NA_KH * GRID_W), :]
        s_loc = _dot_nt(qs, kl) + bias_ref[first - r + NA_KH - 1]
        s_ctx = _dot_nt(qs, kc_ref[0])
        o = _softmax_pv([s_loc, s_ctx], [vl, vc_ref[0]])
        o_ref[0, t * GRID_W:(t + 1) * GRID_W, :] = _select_heads(o, GRID_W).astype(BF16)


def _neighbourhood_attention(bias, q, k, v, kc, vc):
    b, seq, w = q.shape
    rows = seq // GRID_W
    ctx_len = kc.shape[1]
    tq = NA_ROWS_PER_STEP * GRID_W
    whole = pl.BlockSpec((1, seq, w), lambda i, r: (i, 0, 0))
    ctx_spec = pl.BlockSpec((1, ctx_len, w), lambda i, r: (i, 0, 0))
    blk = pl.BlockSpec((1, tq, w), lambda i, r: (i, r, 0))
    return pl.pallas_call(
        functools.partial(_na_attn_kernel, rows),
        grid=(b, seq // tq),
        in_specs=[blk, whole, whole, ctx_spec, ctx_spec, _resident(bias.shape)],
        out_specs=blk,
        out_shape=jax.ShapeDtypeStruct((b, seq, w), BF16),
        compiler_params=_compiler_params(),
        name="neighbourhood_attention",
    )(q, k, v, kc, vc, bias)


def _ctx_attn_kernel(split_layout, has_sink, *refs):
    if has_sink:
        sink_ref, q_ref, k_ref, v_ref, o_ref = refs
    else:
        q_ref, k_ref, v_ref, o_ref = refs
    tq = q_ref.shape[1]
    lane = _lane_iota()
    group = _split_half_head(lane) if split_layout else _natural_head(lane)
    qs = _stack_heads(q_ref[0], group, range(N_HEADS))
    s = _dot_nt(qs, k_ref[0])
    sink_col = _sink_column(sink_ref, tq) if has_sink else None
    o = _softmax_pv([s], [v_ref[0]], sink_col)
    o_ref[0] = _select_heads(o, tq).astype(BF16)


def _context_attention(q, k, v, sink=None, split_layout=False):
    b, n, w = q.shape
    spec = pl.BlockSpec((1, n, w), lambda i: (i, 0, 0))
    in_specs = [spec, spec, spec]
    args = [q, k, v]
    if sink is not None:
        in_specs = [pl.BlockSpec(memory_space=pltpu.SMEM)] + in_specs
        args = [sink] + args
    return pl.pallas_call(
        functools.partial(_ctx_attn_kernel, split_layout, sink is not None),
        grid=(b,),
        in_specs=in_specs,
        out_specs=spec,
        out_shape=jax.ShapeDtypeStruct((b, n, w), BF16),
        compiler_params=_compiler_params(),
        name="context_attention",
    )(*args)


def _diff_attn_kernel(has_local, lambda_init, lam_ref, subg_ref, q_ref, *refs):
    if has_local:
        kl_ref, vl_ref, kc_ref, vc_ref, o_ref = refs
    else:
        kc_ref, vc_ref, o_ref = refs
    tq = q_ref.shape[1]
    lv = lam_ref[...]
    lam = (jnp.exp(jnp.sum(lv[0:1] * lv[1:2], axis=-1, keepdims=True))
           - jnp.exp(jnp.sum(lv[2:3] * lv[3:4], axis=-1, keepdims=True)) + lambda_init)
    q = q_ref[0]
    lane = _lane_iota()
    group = jnp.right_shift(jnp.bitwise_and(lane, 127), 4)
    head = _natural_head(lane)
    exp2_scale = (C_QK_DIM ** -0.5) * math.log2(math.e)
    keys = ([kl_ref] if has_local else []) + [kc_ref]
    vals = ([vl_ref] if has_local else []) + [vc_ref]

    out = None
    for h in range(N_HEADS):
        qs = _stack_heads(q, group, (2 * h, 2 * h + 1))
        scores = [_dot_nt(qs, k_ref[0]) for k_ref in keys]
        m = functools.reduce(jnp.maximum, [jnp.max(s, axis=-1, keepdims=True) for s in scores])
        es = [jnp.exp2((s - m) * exp2_scale) for s in scores]
        l = functools.reduce(jnp.add, [jnp.sum(e, axis=-1, keepdims=True) for e in es])
        pv = functools.reduce(jnp.add, [_dot(e.astype(BF16), v_ref[0]) for e, v_ref in zip(es, vals)])
        inv = 1.0 / l
        o = pv[:tq] * inv[:tq] - pv[tq:] * (lam * inv[tq:])
        sel = head == h
        ms = jnp.sum(jnp.where(sel, o * o, 0.0), axis=-1, keepdims=True) * (1.0 / HEAD_DIM)
        y = o * lax.rsqrt(ms + EPS)
        out = y if out is None else jnp.where(sel, y, out)
    o_ref[0] = (out * subg_ref[...] * (1.0 - lambda_init)).astype(BF16)


def _diff_attention(lam_vecs, subg, lambda_init, q, kc, vc, k=None, v=None):
    b, n, w = q.shape
    ctx_len = kc.shape[1]
    has_local = k is not None
    tq = DIFF_TQ if has_local else n
    blk = pl.BlockSpec((1, tq, w), lambda i, j: (i, j, 0))
    ctx_spec = pl.BlockSpec((1, ctx_len, w), lambda i, j: (i, 0, 0))
    in_specs = [pl.BlockSpec(lam_vecs.shape, lambda i, j: (0, 0)),
                pl.BlockSpec((1, w), lambda i, j: (0, 0)), blk]
    args = [lam_vecs, subg, q]
    if has_local:
        whole = pl.BlockSpec((1, n, w), lambda i, j: (i, 0, 0))
        in_specs += [whole, whole]
        args += [k, v]
    in_specs += [ctx_spec, ctx_spec]
    args += [kc, vc]
    return pl.pallas_call(
        functools.partial(_diff_attn_kernel, has_local, lambda_init),
        grid=(b, n // tq),
        in_specs=in_specs,
        out_specs=blk,
        out_shape=jax.ShapeDtypeStruct((b, n, w), BF16),
        compiler_params=_compiler_params(),
        name="diff_attention",
    )(*args)


def _conv_kernel(seq, u_ref, w_ref, b_ref, g_ref, beta_ref, o_ref, pad_ref):
    halo, tc, ch = CONV_HALO, CONV_TC, GROUP_W
    zeros = jnp.zeros((halo, ch), F32)
    pad_ref[0:halo, :] = zeros
    pad_ref[halo + seq:2 * halo + seq, :] = zeros

    def glu(i, carry):
        r0 = pl.multiple_of(i * tc, tc)
        u = u_ref[0, pl.ds(r0, tc), :]
        pad_ref[pl.ds(halo + r0, tc), :] = u[:, :ch] * _sigmoid(u[:, ch:])
        return carry

    lax.fori_loop(0, seq // tc, glu, 0)

    win_rows = tc + 2 * halo

    def chunk(i, carry):
        c0 = pl.multiple_of(i * tc, tc)
        win = pad_ref[pl.ds(c0, win_rows), :]
        acc = jnp.zeros((tc, ch), F32)
        for sub in range(8):
            shifted = win if sub == 0 else pltpu.roll(win, win_rows - sub, axis=0)
            for blk8 in range(win_rows // 8):
                tap = 8 * blk8 + sub - (halo - CONV_K // 2)
                if 0 <= tap < CONV_K:
                    acc = acc + shifted[8 * blk8:8 * blk8 + tc] * w_ref[tap:tap + 1, :]
        hcv = acc + b_ref[...]
        mu = jnp.mean(hcv, axis=-1, keepdims=True)
        cen = hcv - mu
        var = jnp.mean(cen * cen, axis=-1, keepdims=True)
        y = cen * lax.rsqrt(var + EPS) * g_ref[...] + beta_ref[...]
        o_ref[0, pl.ds(c0, tc), :] = (y * _sigmoid(y)).astype(BF16)
        return carry

    lax.fori_loop(0, seq // tc, chunk, 0, unroll=4)


def _conformer_conv(u, w, bias, ln_g, ln_b):
    b, seq, two_ch = u.shape
    ch = two_ch // 2
    vec = pl.BlockSpec((1, ch), lambda i: (0, 0))
    return pl.pallas_call(
        functools.partial(_conv_kernel, seq),
        grid=(b,),
        in_specs=[pl.BlockSpec((1, seq, two_ch), lambda i: (i, 0, 0)),
                  pl.BlockSpec((CONV_K, ch), lambda i: (0, 0)), vec, vec, vec],
        out_specs=pl.BlockSpec((1, seq, ch), lambda i: (i, 0, 0)),
        out_shape=jax.ShapeDtypeStruct((b, seq, ch), BF16),
        scratch_shapes=[pltpu.VMEM((seq + 2 * CONV_HALO, ch), F32)],
        compiler_params=_compiler_params(),
        name="conformer_conv",
    )(u, w, bias.reshape(1, ch), ln_g.reshape(1, ch), ln_b.reshape(1, ch))


def _ffn_kernel(final, x_ref, ya_ref, yb_ref, yc_ref, yd_ref, g1_ref, sh_ref, sc_ref, g2_ref, ng_ref,
                wo_ref, wg_ref, wu_ref, wd_ref, *rest):
    if final:
        fg_ref, o_ref, ycat_ref, act_ref = rest
    else:
        o_ref, ycat_ref, act_ref = rest
    for j, y_ref in enumerate((ya_ref, yb_ref, yc_ref, yd_ref)):
        ycat_ref[:, j * GROUP_W:(j + 1) * GROUP_W] = y_ref[...]
    x = x_ref[...] + g1_ref[0] * _dot(ycat_ref[...], wo_ref[...])
    r = lax.rsqrt(jnp.mean(x * x, axis=-1, keepdims=True) + EPS)
    h = (x * r) * ng_ref[...]
    hb = (h * (1.0 + sc_ref[0]) + sh_ref[0]).astype(BF16)
    for c in range(0, FFN_HIDDEN, FFN_CHUNK):
        gate = _dot(hb, wg_ref[:, c:c + FFN_CHUNK])
        up = _dot(hb, wu_ref[:, c:c + FFN_CHUNK])
        act_ref[:, c:c + FFN_CHUNK] = (gate * _sigmoid(gate) * up).astype(BF16)
    x = x + g2_ref[0] * _dot(act_ref[...], wd_ref[...])
    if final:
        r = lax.rsqrt(jnp.mean(x * x, axis=-1, keepdims=True) + EPS)
        x = (x * r) * fg_ref[...]
    o_ref[...] = x


def _out_projection_ffn(x, ys, g1, shift, scale, g2, norm_g, wo, wg, wu, wd, rows_per_mod, final_g=None):
    rows, d = x.shape
    tm = ROW_TILE
    tiles_per_mod = rows_per_mod // tm
    final = final_g is not None
    row_spec = pl.BlockSpec((tm, d), lambda i: (i, 0))
    y_spec = pl.BlockSpec((tm, GROUP_W), lambda i: (i, 0))
    mod_spec = pl.BlockSpec((1, 1, d), lambda i: (i // tiles_per_mod, 0, 0))
    vec_spec = pl.BlockSpec((1, d), lambda i: (0, 0))
    in_specs = [row_spec, y_spec, y_spec, y_spec, y_spec, mod_spec, mod_spec, mod_spec, mod_spec, vec_spec,
                _resident(wo.shape), _resident(wg.shape), _resident(wu.shape), _resident(wd.shape)]
    args = [x, *ys, g1, shift, scale, g2, norm_g.reshape(1, d), wo, wg, wu, wd]
    if final:
        in_specs.append(vec_spec)
        args.append(final_g.reshape(1, d))
    return pl.pallas_call(
        functools.partial(_ffn_kernel, final),
        grid=(rows // tm,),
        in_specs=in_specs,
        out_specs=row_spec,
        out_shape=jax.ShapeDtypeStruct((rows, d), F32),
        scratch_shapes=[pltpu.VMEM((tm, d), BF16), pltpu.VMEM((tm, FFN_HIDDEN), BF16)],
        compiler_params=_compiler_params(),
        name="out_projection_ffn",
    )(*args)


def _rope_tables(n_tok, dim):
    t = jnp.arange(n_tok)
    row = (t // GRID_W).astype(F32)
    col = (t % GRID_W).astype(F32)
    nf = dim // 4
    inv = ROPE_BASE ** (-jnp.arange(nf, dtype=F32) / nf)
    ang = jnp.concatenate([row[:, None] * inv, col[:, None] * inv], axis=-1)
    reps = 128 // (dim // 2)
    return jnp.tile(jnp.cos(ang), (1, reps)), jnp.tile(jnp.sin(ang), (1, reps))


def kernel(x, c, ctx, c_ctx, norm1_g, norm2_g, w_ada, b_ada, w_in, w_out, attn_sink, conv_w, conv_b,
           conv_ln_g, conv_ln_b, diff_lq1, diff_lk1, diff_lq2, diff_lk2, diff_subln_g, na_rpb,
           w_gate, w_up, w_down, final_g):
    batch, seq, d = x.shape
    ctx_len = ctx.shape[1]
    depth = w_ada.shape[0]

    cvec = jnp.zeros((ADA_ROWS, d), F32).at[:batch].set(c).at[batch].set(c_ctx)
    mods = _ada_table(cvec, w_ada, b_ada).reshape(depth, ADA_ROWS, 6, d)
    rope = _rope_tables(seq, HEAD_DIM) + _rope_tables(seq, C_QK_DIM)

    xl = x.reshape(batch * seq, d)
    xc = ctx.reshape(batch * ctx_len, d)
    for l in range(depth):
        ctx_needed = l < depth - 1
        lat = [mods[l, :batch, k][:, None, :] for k in range(6)]
        cmod = [mods[l, batch:batch + 1, k][:, None, :] for k in range(6)]
        lambda_init = 0.8 - 0.6 * math.exp(-0.3 * l)
        w_in_l = _relayout_w_in(w_in[l])
        wo, wg, wu, wd = (w_out[l].astype(BF16), w_gate[l].astype(BF16),
                          w_up[l].astype(BF16), w_down[l].astype(BF16))
        lam_vecs = jnp.stack([diff_lq1[l], diff_lk1[l], diff_lq2[l], diff_lk2[l]]).astype(F32)
        subg = jnp.tile(diff_subln_g[l], N_HEADS).reshape(1, GROUP_W)
        cw = conv_w[l].reshape(CONV_K, GROUP_W)

        pl_lat = _in_projection(xl, norm1_g[l], lat[0], lat[1], w_in_l, rope, seq)
        pl_ctx = _in_projection(xc, norm1_g[l], cmod[0], cmod[1], w_in_l, None, batch * ctx_len)
        qa, ka, va, ub, qc, kc, vc, qd, kd, vd = [t.reshape(batch, seq, -1) for t in pl_lat]
        qa_c, ka_c, va_c, ub_c, qc_c, kc_c, vc_c, qd_c, kd_c, vd_c = [
            t.reshape(batch, ctx_len, -1) for t in pl_ctx]

        y_a = _window_attention(attn_sink[l], qa, ka, va, ka_c, va_c)
        y_b = _conformer_conv(ub, cw, conv_b[l], conv_ln_g[l], conv_ln_b[l])
        y_c = _diff_attention(lam_vecs, subg, lambda_init, qc, kc_c, vc_c, kc, vc)
        y_d = _neighbourhood_attention(_na_bias_table(na_rpb[l]), qd, kd, vd, kd_c, vd_c)
        ys = [t.reshape(batch * seq, GROUP_W) for t in (y_a, y_b, y_c, y_d)]
        xl = _out_projection_ffn(xl, ys, lat[2], lat[3], lat[4], lat[5], norm2_g[l], wo, wg, wu, wd, seq,
                                 final_g=None if ctx_needed else final_g)
        if ctx_needed:
            yc_a = _context_attention(qa_c, ka_c, va_c, sink=attn_sink[l], split_layout=True)
            yc_b = _conformer_conv(ub_c, cw, conv_b[l], conv_ln_g[l], conv_ln_b[l])
            yc_c = _diff_attention(lam_vecs, subg, lambda_init, qc_c, kc_c, vc_c)
            yc_d = _context_attention(qd_c, kd_c, vd_c)
            ycs = [t.reshape(batch * ctx_len, GROUP_W) for t in (yc_a, yc_b, yc_c, yc_d)]
            xc = _out_projection_ffn(xc, ycs, cmod[2], cmod[3], cmod[4], cmod[5], norm2_g[l],
                                     wo, wg, wu, wd, batch * ctx_len)
    return xl.reshape(batch, seq, d)
```

```python
import functools
import math

import numpy as np
import jax
import jax.numpy as jnp
from jax import lax
from jax.experimental import pallas as pl
from jax.experimental.pallas import tpu as pltpu

F32 = jnp.float32
BF16 = jnp.bfloat16

D_MODEL = 1024
DEPTH = 2
GRID_W = 64
HEAD_DIM = 64
GROUP_W = 256
N_HEADS = 4
A_KV_HEADS = 2
A_WINDOW = 128
BLK = 128
CONV_K = 31
C_QK_DIM = 32
NA_KH = 8
NA_KW = 16
FFN_HIDDEN = 2816
ROPE_BASE = 10000.0
EPS = 1e-6
NEG_INF = -1e30

VMEM_LIMIT = 56 * 1024 * 1024
ADA_ROWS = 16
ADA_TN = 1536
ROW_TILE = 512
FFN_CHUNK = 256
CONV_TC = 64
CONV_HALO = 16
DIFF_TQ = 128
DIFF_Q_SCALE = (C_QK_DIM ** -0.5) * math.log2(math.e)
WIN_BLOCKS_PER_STEP = 2
NA_ROWS_PER_STEP = 4


def _compiler_params():
    return pltpu.CompilerParams(vmem_limit_bytes=VMEM_LIMIT)


def _resident(shape):
    return pl.BlockSpec(shape, lambda *_: (0,) * len(shape), pipeline_mode=pl.Buffered(1))


def _sigmoid(v):
    return 1.0 / (1.0 + jnp.exp(-v))


def _dot(a, b):
    return jnp.dot(a, b, preferred_element_type=F32)


def _dot_nt(a, b):
    return lax.dot_general(a, b, (((1,), (1,)), ((), ())), preferred_element_type=F32)


def _ada_kernel(c_ref, w_ref, b_ref, o_ref):
    cv = c_ref[...]
    s = cv * _sigmoid(cv)
    o_ref[0] = _dot(s.astype(BF16), w_ref[0].astype(BF16)) + b_ref[0]


def _ada_table(cvec, w_ada, b_ada):
    depth, d, n = w_ada.shape
    return pl.pallas_call(
        _ada_kernel,
        grid=(depth, n // ADA_TN),
        in_specs=[
            pl.BlockSpec((ADA_ROWS, d), lambda l, j: (0, 0)),
            pl.BlockSpec((1, d, ADA_TN), lambda l, j: (l, 0, j)),
            pl.BlockSpec((1, 1, ADA_TN), lambda l, j: (l, 0, j)),
        ],
        out_specs=pl.BlockSpec((1, ADA_ROWS, ADA_TN), lambda l, j: (l, 0, j)),
        out_shape=jax.ShapeDtypeStruct((depth, ADA_ROWS, n), F32),
        compiler_params=_compiler_params(),
        name="ada_table",
    )(cvec, w_ada, b_ada.reshape(depth, 1, n))


def _relayout_w_in(w):
    d = w.shape[0]
    w = w.astype(BF16)

    def split_halves(cols, groups, half):
        return jnp.transpose(cols.reshape(d, groups, 2, half), (0, 2, 1, 3)).reshape(d, GROUP_W)

    qa = split_halves(w[:, 0:256], N_HEADS, 32)
    ka = split_halves(jnp.repeat(w[:, 256:384].reshape(d, A_KV_HEADS, HEAD_DIM), 2, axis=1), N_HEADS, 32)
    va = jnp.repeat(w[:, 384:512].reshape(d, A_KV_HEADS, HEAD_DIM), 2, axis=1).reshape(d, GROUP_W)
    qc = split_halves(w[:, 1024:1280], 2 * N_HEADS, 16)
    kc = split_halves(w[:, 1280:1536], 2 * N_HEADS, 16)
    return jnp.concatenate([qa, ka, va, w[:, 512:1024], qc, kc, w[:, 1536:]], axis=1)


PROJ_WIDTH = 9 * GROUP_W + 512
OFF_QA, OFF_KA, OFF_VA, OFF_UB, OFF_QC, OFF_KC, OFF_VC, OFF_QD, OFF_KD, OFF_VD = (
    0, 256, 512, 768, 1280, 1536, 1792, 2048, 2304, 2560)


def _inproj_kernel(rope, x_ref, g_ref, sh_ref, sc_ref, w_ref, *rest):
    if rope:
        ca_ref, sa_ref, cc_ref, sc2_ref = rest[:4]
        rest = rest[4:]
    qa_o, ka_o, va_o, ub_o, qc_o, kc_o, vc_o, qd_o, kd_o, vd_o = rest
    x = x_ref[...]
    r = lax.rsqrt(jnp.mean(x * x, axis=-1, keepdims=True) + EPS)
    h = (x * r) * g_ref[...]
    h = h * (1.0 + sc_ref[0]) + sh_ref[0]
    hb = h.astype(BF16)

    def proj(off, width=GROUP_W):
        return _dot(hb, w_ref[:, off:off + width])

    def store_rot(o_ref, y, c_ref, s_ref, scale):
        if rope:
            x1, x2 = y[:, :128], y[:, 128:]
            cs, sn = c_ref[...], s_ref[...]
            y1, y2 = x1 * cs - x2 * sn, x1 * sn + x2 * cs
        else:
            y1, y2 = y[:, :128], y[:, 128:]
        o_ref[:, :128] = (y1 * scale).astype(BF16)
        o_ref[:, 128:] = (y2 * scale).astype(BF16)

    ca = sa = cc = sc2 = None
    if rope:
        ca, sa, cc, sc2 = ca_ref, sa_ref, cc_ref, sc2_ref
    store_rot(qa_o, proj(OFF_QA), ca, sa, HEAD_DIM ** -0.5)
    store_rot(ka_o, proj(OFF_KA), ca, sa, 1.0)
    va_o[...] = proj(OFF_VA).astype(BF16)
    ub_o[...] = proj(OFF_UB, 512)
    store_rot(qc_o, proj(OFF_QC), cc, sc2, DIFF_Q_SCALE)
    store_rot(kc_o, proj(OFF_KC), cc, sc2, 1.0)
    vc_o[...] = proj(OFF_VC).astype(BF16)
    qd_o[...] = (proj(OFF_QD) * (HEAD_DIM ** -0.5)).astype(BF16)
    kd_o[...] = proj(OFF_KD).astype(BF16)
    vd_o[...] = proj(OFF_VD).astype(BF16)


def _mod_spec(d, chunk, mod_rows, tm):
    first_row, rows_per_mod = mod_rows
    tiles_per_mod = rows_per_mod // tm
    return pl.BlockSpec((1, 1, d), lambda i: ((first_row + i // tiles_per_mod) * 6 + chunk, 0, 0))


def _in_projection(x, g, mods, mod_rows, w, rope_tables):
    rows, d = x.shape
    tm = ROW_TILE
    rope = rope_tables is not None
    in_specs = [
        pl.BlockSpec((tm, d), lambda i: (i, 0)),
        pl.BlockSpec((1, d), lambda i: (0, 0)),
        _mod_spec(d, 0, mod_rows, tm), _mod_spec(d, 1, mod_rows, tm),
        _resident((d, PROJ_WIDTH)),
    ]
    args = [x, g.reshape(1, d), mods, mods, w]
    if rope:
        tiles_per_seq = rope_tables[0].shape[0] // tm
        tab_spec = pl.BlockSpec((tm, 128), lambda i: (i % tiles_per_seq, 0))
        in_specs += [tab_spec] * 4
        args += list(rope_tables)
    narrow = pl.BlockSpec((tm, GROUP_W), lambda i: (i, 0))
    wide = pl.BlockSpec((tm, 512), lambda i: (i, 0))
    out_specs = [narrow, narrow, narrow, wide] + [narrow] * 6
    bf = jax.ShapeDtypeStruct((rows, GROUP_W), BF16)
    out_shape = [bf, bf, bf, jax.ShapeDtypeStruct((rows, 512), F32)] + [bf] * 6
    return pl.pallas_call(
        functools.partial(_inproj_kernel, rope),
        grid=(rows // tm,),
        in_specs=in_specs,
        out_specs=out_specs,
        out_shape=out_shape,
        compiler_params=_compiler_params(),
        name="in_projection",
    )(*args)


def _lane_iota():
    return lax.broadcasted_iota(jnp.int32, (1, GROUP_W), 1)


def _split_half_head(lane):
    return jnp.right_shift(jnp.bitwise_and(lane, 127), 5)


def _natural_head(lane):
    return jnp.right_shift(lane, 6)


def _stack_heads(q, lane_group, groups):
    zero = jnp.zeros_like(q)
    return jnp.concatenate([jnp.where(lane_group == g, q, zero) for g in groups], axis=0)


def _softmax_pv(scores, values, sink_col=None):
    m = functools.reduce(jnp.maximum, [jnp.max(s, axis=-1, keepdims=True) for s in scores])
    if sink_col is not None:
        m = jnp.maximum(m, sink_col)
    es = [jnp.exp(s - m) for s in scores]
    l = functools.reduce(jnp.add, [jnp.sum(e, axis=-1, keepdims=True) for e in es])
    if sink_col is not None:
        l = l + jnp.exp(sink_col - m)
    o = functools.reduce(jnp.add, [_dot(e.astype(BF16), v) for e, v in zip(es, values)])
    return o / l


def _select_heads(o, tq):
    head = _natural_head(_lane_iota())
    out = o[0:tq]
    for h in range(1, N_HEADS):
        out = jnp.where(head == h, o[h * tq:(h + 1) * tq], out)
    return out


def _sink_column(sink_ref, tq):
    row = lax.broadcasted_iota(jnp.int32, (N_HEADS * tq, 1), 0)
    col = jnp.full((N_HEADS * tq, 1), sink_ref[N_HEADS - 1], F32)
    for h in range(N_HEADS - 2, -1, -1):
        col = jnp.where(row < (h + 1) * tq, sink_ref[h], col)
    return col


def _win_attn_kernel(seq, sink_ref, q_ref, kl_ref, vl_ref, kc_ref, vc_ref, o_ref):
    lane_head = _split_half_head(_lane_iota())
    sink_col = _sink_column(sink_ref, BLK)
    shape = (N_HEADS * BLK, 3 * BLK)
    rel = (lax.broadcasted_iota(jnp.int32, shape, 1)
           - jnp.bitwise_and(lax.broadcasted_iota(jnp.int32, shape, 0), BLK - 1))
    for t in range(WIN_BLOCKS_PER_STEP):
        n = pl.program_id(1) * WIN_BLOCKS_PER_STEP + t
        start = pl.multiple_of(jnp.clip((n - 1) * BLK, 0, seq - 3 * BLK), BLK)
        qs = _stack_heads(q_ref[0, t * BLK:(t + 1) * BLK, :], lane_head, range(N_HEADS))
        kl = kl_ref[0, pl.ds(start, 3 * BLK), :]
        vl = vl_ref[0, pl.ds(start, 3 * BLK), :]
        s_loc = _dot_nt(qs, kl)
        s_ctx = _dot_nt(qs, kc_ref[0])
        dist = rel + (start - n * BLK)
        s_loc = jnp.where(jnp.abs(dist) <= A_WINDOW, s_loc, NEG_INF)
        o = _softmax_pv([s_loc, s_ctx], [vl, vc_ref[0]], sink_col)
        o_ref[0, t * BLK:(t + 1) * BLK, :] = _select_heads(o, BLK).astype(BF16)


def _window_attention(sink, q, k, v, kc, vc):
    b, seq, w = q.shape
    ctx_len = kc.shape[1]
    tq = WIN_BLOCKS_PER_STEP * BLK
    whole = pl.BlockSpec((1, seq, w), lambda i, n: (i, 0, 0))
    ctx_spec = pl.BlockSpec((1, ctx_len, w), lambda i, n: (i, 0, 0))
    blk = pl.BlockSpec((1, tq, w), lambda i, n: (i, n, 0))
    return pl.pallas_call(
        functools.partial(_win_attn_kernel, seq),
        grid=(b, seq // tq),
        in_specs=[pl.BlockSpec(memory_space=pltpu.SMEM), blk, whole, whole, ctx_spec, ctx_spec],
        out_specs=blk,
        out_shape=jax.ShapeDtypeStruct((b, seq, w), BF16),
        compiler_params=_compiler_params(),
        name="window_attention",
    )(sink, q, k, v, kc, vc)


def _na_bias_table(rpb):
    cq = np.arange(GRID_W)
    cs = np.clip(cq - NA_KW // 2, 0, GRID_W - NA_KW)
    col_valid = (cq[None, :] >= cs[:, None]) & (cq[None, :] < cs[:, None] + NA_KW)
    dc = np.clip(cq[None, :] - cq[:, None], -(NA_KW - 1), NA_KW - 1) + (NA_KW - 1)
    n_dr, n_dc = 2 * NA_KH - 1, 2 * NA_KW - 1
    onehot = (dc.reshape(1, -1) == np.arange(n_dc)[:, None]).astype(np.float32)
    t = jnp.dot(rpb.astype(F32).reshape(N_HEADS * n_dr, n_dc), onehot, precision=lax.Precision.HIGHEST)
    t = jnp.where(col_valid[None, None], t.reshape(N_HEADS, n_dr, GRID_W, GRID_W), NEG_INF)
    tab = jnp.stack([t[:, off:off + NA_KH] for off in range(NA_KH)])
    return jnp.transpose(tab, (0, 1, 3, 2, 4)).reshape(NA_KH, N_HEADS * GRID_W, NA_KH * GRID_W)


def _na_row_start(r, rows):
    return jnp.clip(r - NA_KH // 2, 0, rows - NA_KH)


def _na_attn_kernel(rows, q_ref, kl_ref, vl_ref, kc_ref, vc_ref, bias_ref, o_ref):
    lane_head = _natural_head(_lane_iota())
    for t in range(NA_ROWS_PER_STEP):
        r = pl.program_id(1) * NA_ROWS_PER_STEP + t
        first = _na_row_start(r, rows)
        start = pl.multiple_of(first * GRID_W, GRID_W)
        qs = _stack_heads(q_ref[0, t * GRID_W:(t + 1) * GRID_W, :], lane_head, range(N_HEADS))
        kl = kl_ref[0, pl.ds(start, NA_KH * GRID_W), :]
        vl = vl_ref[0, pl.ds(start, NA_KH * GRID_W), :]
        s_loc = _dot_nt(qs, kl) + bias_ref[first - r + NA_KH - 1]
        s_ctx = _dot_nt(qs, kc_ref[0])
        o = _softmax_pv([s_loc, s_ctx], [vl, vc_ref[0]])
        o_ref[0, t * GRID_W:(t + 1) * GRID_W, :] = _select_heads(o, GRID_W).astype(BF16)


def _neighbourhood_attention(bias, q, k, v, kc, vc):
    b, seq, w = q.shape
    rows = seq // GRID_W
    ctx_len = kc.shape[1]
    tq = NA_ROWS_PER_STEP * GRID_W
    whole = pl.BlockSpec((1, seq, w), lambda i, r: (i, 0, 0))
    ctx_spec = pl.BlockSpec((1, ctx_len, w), lambda i, r: (i, 0, 0))
    blk = pl.BlockSpec((1, tq, w), lambda i, r: (i, r, 0))
    return pl.pallas_call(
        functools.partial(_na_attn_kernel, rows),
        grid=(b, seq // tq),
        in_specs=[blk, whole, whole, ctx_spec, ctx_spec, _resident(bias.shape)],
        out_specs=blk,
        out_shape=jax.ShapeDtypeStruct((b, seq, w), BF16),
        compiler_params=_compiler_params(),
        name="neighbourhood_attention",
    )(q, k, v, kc, vc, bias)


def _ctx_attn_kernel(split_layout, has_sink, *refs):
    if has_sink:
        sink_ref, q_ref, k_ref, v_ref, o_ref = refs
    else:
        q_ref, k_ref, v_ref, o_ref = refs
    tq = q_ref.shape[1]
    lane = _lane_iota()
    group = _split_half_head(lane) if split_layout else _natural_head(lane)
    qs = _stack_heads(q_ref[0], group, range(N_HEADS))
    s = _dot_nt(qs, k_ref[0])
    sink_col = _sink_column(sink_ref, tq) if has_sink else None
    o = _softmax_pv([s], [v_ref[0]], sink_col)
    o_ref[0] = _select_heads(o, tq).astype(BF16)


def _context_attention(q, k, v, sink=None, split_layout=False):
    b, n, w = q.shape
    spec = pl.BlockSpec((1, n, w), lambda i: (i, 0, 0))
    in_specs = [spec, spec, spec]
    args = [q, k, v]
    if sink is not None:
        in_specs = [pl.BlockSpec(memory_space=pltpu.SMEM)] + in_specs
        args = [sink] + args
    return pl.pallas_call(
        functools.partial(_ctx_attn_kernel, split_layout, sink is not None),
        grid=(b,),
        in_specs=in_specs,
        out_specs=spec,
        out_shape=jax.ShapeDtypeStruct((b, n, w), BF16),
        compiler_params=_compiler_params(),
        name="context_attention",
    )(*args)


def _diff_attn_kernel(has_local, lambda_init, lam_ref, subg_ref, qt_ref, *refs):
    if has_local:
        kl_ref, vtl_ref, kc_ref, vtc_ref, o_ref = refs
    else:
        kc_ref, vtc_ref, o_ref = refs
    tq = qt_ref.shape[2]
    lv = lam_ref[...]
    lam = (jnp.exp(jnp.sum(lv[0:1] * lv[1:2], axis=-1, keepdims=True))
           - jnp.exp(jnp.sum(lv[2:3] * lv[3:4], axis=-1, keepdims=True)) + lambda_init)
    qt = qt_ref[0]
    chan = lax.broadcasted_iota(jnp.int32, (GROUP_W, 1), 0)
    group = jnp.right_shift(jnp.bitwise_and(chan, 127), 4)
    zero = jnp.zeros_like(qt)
    keys = ([kl_ref] if has_local else []) + [kc_ref]
    vals = ([vtl_ref] if has_local else []) + [vtc_ref]
    sum_rows = 16

    def head_scores(h):
        qs = jnp.concatenate([jnp.where(group == 2 * h, qt, zero),
                              jnp.where(group == 2 * h + 1, qt, zero)], axis=1)
        return [_dot(k_ref[0], qs) for k_ref in keys]

    def column_max(s, slab=64):
        part = jnp.max(s.reshape(s.shape[0] // slab, slab, s.shape[1]), axis=0)
        return jnp.max(part, axis=0, keepdims=True)

    ys = []
    nxt = head_scores(0)
    for h in range(N_HEADS):
        scores = nxt
        if h + 1 < N_HEADS:
            nxt = head_scores(h + 1)
        m = functools.reduce(jnp.maximum, [column_max(s) for s in scores])
        pv = None
        for s, vt_ref in zip(scores, vals):
            e = jnp.exp2(s - m).astype(BF16)
            vt = vt_ref[0, h * HEAD_DIM:(h + 1) * HEAD_DIM, :]
            lhs = jnp.concatenate([vt, jnp.ones((sum_rows, vt.shape[1]), BF16)], axis=0)
            p = _dot(lhs, e)
            pv = p if pv is None else pv + p
        inv = 1.0 / pv[HEAD_DIM:HEAD_DIM + 1]
        pv = pv[:HEAD_DIM]
        o = pv[:, :tq] * inv[:, :tq] - pv[:, tq:] * (lam * inv[:, tq:])
        ms = jnp.mean(o * o, axis=0, keepdims=True)
        ys.append(o * lax.rsqrt(ms + EPS))
    out = jnp.concatenate(ys, axis=0).T
    o_ref[0] = (out * subg_ref[...] * (1.0 - lambda_init)).astype(BF16)


def _diff_attention(lam_vecs, subg, lambda_init, qt, kc, vtc, k=None, vt=None):
    b, w, n = qt.shape
    ctx_len = kc.shape[1]
    has_local = k is not None
    tq = DIFF_TQ
    q_spec = pl.BlockSpec((1, w, tq), lambda i, j: (i, 0, j))
    in_specs = [pl.BlockSpec(lam_vecs.shape, lambda i, j: (0, 0)),
                pl.BlockSpec((1, w), lambda i, j: (0, 0)), q_spec]
    args = [lam_vecs, subg, qt]
    if has_local:
        in_specs += [pl.BlockSpec((1, n, w), lambda i, j: (i, 0, 0)),
                     pl.BlockSpec((1, w, n), lambda i, j: (i, 0, 0))]
        args += [k, vt]
    in_specs += [pl.BlockSpec((1, ctx_len, w), lambda i, j: (i, 0, 0)),
                 pl.BlockSpec((1, w, ctx_len), lambda i, j: (i, 0, 0))]
    args += [kc, vtc]
    return pl.pallas_call(
        functools.partial(_diff_attn_kernel, has_local, lambda_init),
        grid=(b, n // tq),
        in_specs=in_specs,
        out_specs=pl.BlockSpec((1, tq, w), lambda i, j: (i, j, 0)),
        out_shape=jax.ShapeDtypeStruct((b, n, w), BF16),
        compiler_params=_compiler_params(),
        name="diff_attention",
    )(*args)


def _conv_kernel(seq, u_ref, w_ref, b_ref, g_ref, beta_ref, o_ref, pad_ref):
    halo, tc, ch = CONV_HALO, CONV_TC, GROUP_W
    zeros = jnp.zeros((halo, ch), F32)
    pad_ref[0:halo, :] = zeros
    pad_ref[halo + seq:2 * halo + seq, :] = zeros

    def glu(i, carry):
        r0 = pl.multiple_of(i * tc, tc)
        u = u_ref[0, pl.ds(r0, tc), :]
        pad_ref[pl.ds(halo + r0, tc), :] = u[:, :ch] * _sigmoid(u[:, ch:])
        return carry

    lax.fori_loop(0, seq // tc, glu, 0)

    win_rows = tc + 2 * halo

    def chunk(i, carry):
        c0 = pl.multiple_of(i * tc, tc)
        win = pad_ref[pl.ds(c0, win_rows), :]
        acc = jnp.zeros((tc, ch), F32)
        for sub in range(8):
            shifted = win if sub == 0 else pltpu.roll(win, win_rows - sub, axis=0)
            for blk8 in range(win_rows // 8):
                tap = 8 * blk8 + sub - (halo - CONV_K // 2)
                if 0 <= tap < CONV_K:
                    acc = acc + shifted[8 * blk8:8 * blk8 + tc] * w_ref[tap:tap + 1, :]
        hcv = acc + b_ref[...]
        mu = jnp.mean(hcv, axis=-1, keepdims=True)
        cen = hcv - mu
        var = jnp.mean(cen * cen, axis=-1, keepdims=True)
        y = cen * lax.rsqrt(var + EPS) * g_ref[...] + beta_ref[...]
        o_ref[0, pl.ds(c0, tc), :] = (y * _sigmoid(y)).astype(BF16)
        return carry

    lax.fori_loop(0, seq // tc, chunk, 0, unroll=4)


def _conformer_conv(u, w, bias, ln_g, ln_b):
    b, seq, two_ch = u.shape
    ch = two_ch // 2
    vec = pl.BlockSpec((1, ch), lambda i: (0, 0))
    return pl.pallas_call(
        functools.partial(_conv_kernel, seq),
        grid=(b,),
        in_specs=[pl.BlockSpec((1, seq, two_ch), lambda i: (i, 0, 0)),
                  pl.BlockSpec((CONV_K, ch), lambda i: (0, 0)), vec, vec, vec],
        out_specs=pl.BlockSpec((1, seq, ch), lambda i: (i, 0, 0)),
        out_shape=jax.ShapeDtypeStruct((b, seq, ch), BF16),
        scratch_shapes=[pltpu.VMEM((seq + 2 * CONV_HALO, ch), F32)],
        compiler_params=_compiler_params(),
        name="conformer_conv",
    )(u, w, bias.reshape(1, ch), ln_g.reshape(1, ch), ln_b.reshape(1, ch))


def _ffn_kernel(final, x_ref, ya_ref, yb_ref, yc_ref, yd_ref, g1_ref, sh_ref, sc_ref, g2_ref, ng_ref,
                wo_ref, wg_ref, wu_ref, wd_ref, *rest):
    if final:
        fg_ref, o_ref, ycat_ref, act_ref = rest
    else:
        o_ref, ycat_ref, act_ref = rest
    for j, y_ref in enumerate((ya_ref, yb_ref, yc_ref, yd_ref)):
        ycat_ref[:, j * GROUP_W:(j + 1) * GROUP_W] = y_ref[...]
    x = x_ref[...] + g1_ref[0] * _dot(ycat_ref[...], wo_ref[...])
    r = lax.rsqrt(jnp.mean(x * x, axis=-1, keepdims=True) + EPS)
    h = (x * r) * ng_ref[...]
    hb = (h * (1.0 + sc_ref[0]) + sh_ref[0]).astype(BF16)
    for c in range(0, FFN_HIDDEN, FFN_CHUNK):
        gate = _dot(hb, wg_ref[:, c:c + FFN_CHUNK])
        up = _dot(hb, wu_ref[:, c:c + FFN_CHUNK])
        act_ref[:, c:c + FFN_CHUNK] = (gate * _sigmoid(gate) * up).astype(BF16)
    x = x + g2_ref[0] * _dot(act_ref[...], wd_ref[...])
    if final:
        r = lax.rsqrt(jnp.mean(x * x, axis=-1, keepdims=True) + EPS)
        x = (x * r) * fg_ref[...]
    o_ref[...] = x


def _out_projection_ffn(x, ys, mods, mod_rows, norm_g, wo, wg, wu, wd, final_g=None):
    rows, d = x.shape
    tm = ROW_TILE
    final = final_g is not None
    row_spec = pl.BlockSpec((tm, d), lambda i: (i, 0))
    y_spec = pl.BlockSpec((tm, GROUP_W), lambda i: (i, 0))
    vec_spec = pl.BlockSpec((1, d), lambda i: (0, 0))
    in_specs = ([row_spec, y_spec, y_spec, y_spec, y_spec]
                + [_mod_spec(d, chunk, mod_rows, tm) for chunk in (2, 3, 4, 5)]
                + [vec_spec, _resident(wo.shape), _resident(wg.shape), _resident(wu.shape), _resident(wd.shape)])
    args = [x, *ys, mods, mods, mods, mods, norm_g.reshape(1, d), wo, wg, wu, wd]
    if final:
        in_specs.append(vec_spec)
        args.append(final_g.reshape(1, d))
    return pl.pallas_call(
        functools.partial(_ffn_kernel, final),
        grid=(rows // tm,),
        in_specs=in_specs,
        out_specs=row_spec,
        out_shape=jax.ShapeDtypeStruct((rows, d), F32),
        scratch_shapes=[pltpu.VMEM((tm, d), BF16), pltpu.VMEM((tm, FFN_HIDDEN), BF16)],
        compiler_params=_compiler_params(),
        name="out_projection_ffn",
    )(*args)


def _rope_tables(n_tok, dim):
    t = jnp.arange(n_tok)
    row = (t // GRID_W).astype(F32)
    col = (t % GRID_W).astype(F32)
    nf = dim // 4
    inv = ROPE_BASE ** (-jnp.arange(nf, dtype=F32) / nf)
    ang = jnp.concatenate([row[:, None] * inv, col[:, None] * inv], axis=-1)
    reps = 128 // (dim // 2)
    return jnp.tile(jnp.cos(ang), (1, reps)), jnp.tile(jnp.sin(ang), (1, reps))


def kernel(x, c, ctx, c_ctx, norm1_g, norm2_g, w_ada, b_ada, w_in, w_out, attn_sink, conv_w, conv_b,
           conv_ln_g, conv_ln_b, diff_lq1, diff_lk1, diff_lq2, diff_lk2, diff_subln_g, na_rpb,
           w_gate, w_up, w_down, final_g):
    batch, seq, d = x.shape
    ctx_len = ctx.shape[1]
    depth = w_ada.shape[0]

    cvec = jnp.zeros((ADA_ROWS, d), F32).at[:batch].set(c).at[batch].set(c_ctx)
    mods_all = _ada_table(cvec, w_ada, b_ada).reshape(depth, ADA_ROWS * 6, 1, d)
    rope = _rope_tables(seq, HEAD_DIM) + _rope_tables(seq, C_QK_DIM)
    lat_rows = (0, seq)
    ctx_rows = (batch, batch * ctx_len)

    def channel_major(t):
        return jnp.swapaxes(t, 1, 2)

    xl = x.reshape(batch * seq, d)
    xc = ctx.reshape(batch * ctx_len, d)
    for l in range(depth):
        ctx_needed = l < depth - 1
        mods = mods_all[l]
        lambda_init = 0.8 - 0.6 * math.exp(-0.3 * l)
        w_in_l = _relayout_w_in(w_in[l])
        wo, wg, wu, wd = (w_out[l].astype(BF16), w_gate[l].astype(BF16),
                          w_up[l].astype(BF16), w_down[l].astype(BF16))
        lam_vecs = jnp.stack([diff_lq1[l], diff_lk1[l], diff_lq2[l], diff_lk2[l]]).astype(F32)
        subg = jnp.tile(diff_subln_g[l], N_HEADS).reshape(1, GROUP_W)
        cw = conv_w[l].reshape(CONV_K, GROUP_W)

        pl_lat = _in_projection(xl, norm1_g[l], mods, lat_rows, w_in_l, rope)
        pl_ctx = _in_projection(xc, norm1_g[l], mods, ctx_rows, w_in_l, None)
        qa, ka, va, ub, qc, kc, vc, qd, kd, vd = [t.reshape(batch, seq, -1) for t in pl_lat]
        qa_c, ka_c, va_c, ub_c, qc_c, kc_c, vc_c, qd_c, kd_c, vd_c = [
            t.reshape(batch, ctx_len, -1) for t in pl_ctx]
        vtc_c = channel_major(vc_c)

        y_a = _window_attention(attn_sink[l], qa, ka, va, ka_c, va_c)
        y_b = _conformer_conv(ub, cw, conv_b[l], conv_ln_g[l], conv_ln_b[l])
        y_c = _diff_attention(lam_vecs, subg, lambda_init, channel_major(qc), kc_c, vtc_c, kc, channel_major(vc))
        y_d = _neighbourhood_attention(_na_bias_table(na_rpb[l]), qd, kd, vd, kd_c, vd_c)
        ys = [t.reshape(batch * seq, GROUP_W) for t in (y_a, y_b, y_c, y_d)]
        xl = _out_projection_ffn(xl, ys, mods, lat_rows, norm2_g[l], wo, wg, wu, wd,
                                 final_g=None if ctx_needed else final_g)
        if ctx_needed:
            yc_a = _context_attention(qa_c, ka_c, va_c, sink=attn_sink[l], split_layout=True)
            yc_b = _conformer_conv(ub_c, cw, conv_b[l], conv_ln_g[l], conv_ln_b[l])
            yc_c = _diff_attention(lam_vecs, subg, lambda_init, channel_major(qc_c), kc_c, vtc_c)
            yc_d = _context_attention(qd_c, kd_c, vd_c)
            ycs = [t.reshape(batch * ctx_len, GROUP_W) for t in (yc_a, yc_b, yc_c, yc_d)]
            xc = _out_projection_ffn(xc, ycs, mods, ctx_rows, norm2_g[l], wo, wg, wu, wd)
    return xl.reshape(batch, seq, d)
```

```python
import functools
import math

import numpy as np
import jax
import jax.numpy as jnp
from jax import lax
from jax.experimental import pallas as pl
from jax.experimental.pallas import tpu as pltpu

F32 = jnp.float32
BF16 = jnp.bfloat16

D_MODEL = 1024
DEPTH = 2
GRID_W = 64
HEAD_DIM = 64
GROUP_W = 256
N_HEADS = 4
A_KV_HEADS = 2
A_WINDOW = 128
BLK = 128
CONV_K = 31
C_QK_DIM = 32
NA_KH = 8
NA_KW = 16
FFN_HIDDEN = 2816
ROPE_BASE = 10000.0
EPS = 1e-6
NEG_INF = -1e30

VMEM_LIMIT = 56 * 1024 * 1024
ADA_ROWS = 16
ADA_TN = 1536
ROW_TILE = 512
FFN_CHUNK = 256
CONV_TC = 64
CONV_HALO = 16
DIFF_TQ = 128
LOG2E = math.log2(math.e)
ATTN_Q_SCALE = (HEAD_DIM ** -0.5) * LOG2E
DIFF_Q_SCALE = (C_QK_DIM ** -0.5) * LOG2E
DIFF_BLOCKS_PER_STEP = 4
DIFF_KEY_SEG = 1024
WIN_BLOCKS_PER_STEP = 4
NA_ROWS_PER_STEP = 8


def _compiler_params():
    return pltpu.CompilerParams(vmem_limit_bytes=VMEM_LIMIT)


def _resident(shape):
    return pl.BlockSpec(shape, lambda *_: (0,) * len(shape), pipeline_mode=pl.Buffered(1))


def _sigmoid(v):
    return 1.0 / (1.0 + jnp.exp(-v))


def _dot(a, b):
    return jnp.dot(a, b, preferred_element_type=F32)


def _dot_nt(a, b):
    return lax.dot_general(a, b, (((1,), (1,)), ((), ())), preferred_element_type=F32)


def _ada_kernel(c_ref, w_ref, b_ref, o_ref):
    cv = c_ref[...]
    s = cv * _sigmoid(cv)
    o_ref[0] = _dot(s.astype(BF16), w_ref[0].astype(BF16)) + b_ref[0]


def _ada_table(cvec, w_ada, b_ada):
    depth, d, n = w_ada.shape
    return pl.pallas_call(
        _ada_kernel,
        grid=(depth, n // ADA_TN),
        in_specs=[
            pl.BlockSpec((ADA_ROWS, d), lambda l, j: (0, 0)),
            pl.BlockSpec((1, d, ADA_TN), lambda l, j: (l, 0, j)),
            pl.BlockSpec((1, 1, ADA_TN), lambda l, j: (l, 0, j)),
        ],
        out_specs=pl.BlockSpec((1, ADA_ROWS, ADA_TN), lambda l, j: (l, 0, j)),
        out_shape=jax.ShapeDtypeStruct((depth, ADA_ROWS, n), F32),
        compiler_params=_compiler_params(),
        name="ada_table",
    )(cvec, w_ada, b_ada.reshape(depth, 1, n))


def _relayout_w_in(w):
    d = w.shape[0]
    w = w.astype(BF16)

    def split_halves(cols, groups, half):
        return jnp.transpose(cols.reshape(d, groups, 2, half), (0, 2, 1, 3)).reshape(d, GROUP_W)

    qa = split_halves(w[:, 0:256], N_HEADS, 32)
    ka = split_halves(jnp.repeat(w[:, 256:384].reshape(d, A_KV_HEADS, HEAD_DIM), 2, axis=1), N_HEADS, 32)
    va = jnp.repeat(w[:, 384:512].reshape(d, A_KV_HEADS, HEAD_DIM), 2, axis=1).reshape(d, GROUP_W)
    qc = split_halves(w[:, 1024:1280], 2 * N_HEADS, 16)
    kc = split_halves(w[:, 1280:1536], 2 * N_HEADS, 16)
    return jnp.concatenate([qa, ka, va, w[:, 512:1024], qc, kc, w[:, 1536:]], axis=1)


PROJ_WIDTH = 9 * GROUP_W + 512
OFF_QA, OFF_KA, OFF_VA, OFF_UB, OFF_QC, OFF_KC, OFF_VC, OFF_QD, OFF_KD, OFF_VD = (
    0, 256, 512, 768, 1280, 1536, 1792, 2048, 2304, 2560)


def _inproj_kernel(rope, x_ref, g_ref, sh_ref, sc_ref, w_ref, *rest):
    if rope:
        ca_ref, sa_ref, cc_ref, sc2_ref = rest[:4]
        rest = rest[4:]
    qa_o, ka_o, va_o, ub_o, qc_o, kc_o, vc_o, qd_o, kd_o, vd_o = rest
    x = x_ref[...]
    r = lax.rsqrt(jnp.mean(x * x, axis=-1, keepdims=True) + EPS)
    h = (x * r) * g_ref[...]
    h = h * (1.0 + sc_ref[0]) + sh_ref[0]
    hb = h.astype(BF16)

    def proj(off, width=GROUP_W):
        return _dot(hb, w_ref[:, off:off + width])

    def store_rot(o_ref, y, c_ref, s_ref, scale):
        if rope:
            x1, x2 = y[:, :128], y[:, 128:]
            cs, sn = c_ref[...], s_ref[...]
            y1, y2 = x1 * cs - x2 * sn, x1 * sn + x2 * cs
        else:
            y1, y2 = y[:, :128], y[:, 128:]
        o_ref[:, :128] = (y1 * scale).astype(BF16)
        o_ref[:, 128:] = (y2 * scale).astype(BF16)

    ca = sa = cc = sc2 = None
    if rope:
        ca, sa, cc, sc2 = ca_ref, sa_ref, cc_ref, sc2_ref
    store_rot(qa_o, proj(OFF_QA), ca, sa, ATTN_Q_SCALE)
    store_rot(ka_o, proj(OFF_KA), ca, sa, 1.0)
    va_o[...] = proj(OFF_VA).astype(BF16)
    ub_o[...] = proj(OFF_UB, 512)
    store_rot(qc_o, proj(OFF_QC), cc, sc2, DIFF_Q_SCALE)
    store_rot(kc_o, proj(OFF_KC), cc, sc2, 1.0)
    vc_o[...] = proj(OFF_VC).astype(BF16)
    qd_o[...] = (proj(OFF_QD) * ATTN_Q_SCALE).astype(BF16)
    kd_o[...] = proj(OFF_KD).astype(BF16)
    vd_o[...] = proj(OFF_VD).astype(BF16)


def _mod_spec(d, chunk, mod_rows, tm):
    first_row, rows_per_mod = mod_rows
    tiles_per_mod = rows_per_mod // tm
    return pl.BlockSpec((1, 1, d), lambda i: ((first_row + i // tiles_per_mod) * 6 + chunk, 0, 0))


def _in_projection(x, g, mods, mod_rows, w, rope_tables):
    rows, d = x.shape
    tm = ROW_TILE
    rope = rope_tables is not None
    in_specs = [
        pl.BlockSpec((tm, d), lambda i: (i, 0)),
        pl.BlockSpec((1, d), lambda i: (0, 0)),
        _mod_spec(d, 0, mod_rows, tm), _mod_spec(d, 1, mod_rows, tm),
        _resident((d, PROJ_WIDTH)),
    ]
    args = [x, g.reshape(1, d), mods, mods, w]
    if rope:
        tiles_per_seq = rope_tables[0].shape[0] // tm
        tab_spec = pl.BlockSpec((tm, 128), lambda i: (i % tiles_per_seq, 0))
        in_specs += [tab_spec] * 4
        args += list(rope_tables)
    narrow = pl.BlockSpec((tm, GROUP_W), lambda i: (i, 0))
    wide = pl.BlockSpec((tm, 512), lambda i: (i, 0))
    out_specs = [narrow, narrow, narrow, wide] + [narrow] * 6
    bf = jax.ShapeDtypeStruct((rows, GROUP_W), BF16)
    out_shape = [bf, bf, bf, jax.ShapeDtypeStruct((rows, 512), F32)] + [bf] * 6
    return pl.pallas_call(
        functools.partial(_inproj_kernel, rope),
        grid=(rows // tm,),
        in_specs=in_specs,
        out_specs=out_specs,
        out_shape=out_shape,
        compiler_params=_compiler_params(),
        name="in_projection",
    )(*args)


def _lane_iota():
    return lax.broadcasted_iota(jnp.int32, (1, GROUP_W), 1)


def _split_half_head(lane):
    return jnp.right_shift(jnp.bitwise_and(lane, 127), 5)


def _natural_head(lane):
    return jnp.right_shift(lane, 6)


def _stack_heads(q, lane_group, groups):
    zero = jnp.zeros_like(q)
    return jnp.concatenate([jnp.where(lane_group == g, q, zero) for g in groups], axis=0)


def _softmax_pv(scores, values, sink_col=None):
    m = functools.reduce(jnp.maximum, [jnp.max(s, axis=-1, keepdims=True) for s in scores])
    if sink_col is not None:
        m = jnp.maximum(m, sink_col)
    es = [jnp.exp2(s - m) for s in scores]
    l = functools.reduce(jnp.add, [jnp.sum(e, axis=-1, keepdims=True) for e in es])
    if sink_col is not None:
        l = l + jnp.exp2(sink_col - m)
    o = functools.reduce(jnp.add, [_dot(e.astype(BF16), v) for e, v in zip(es, values)])
    return o / l


def _select_heads(o, tq):
    head = _natural_head(_lane_iota())
    out = o[0:tq]
    for h in range(1, N_HEADS):
        out = jnp.where(head == h, o[h * tq:(h + 1) * tq], out)
    return out


def _sink_column(sink_ref, tq):
    row = lax.broadcasted_iota(jnp.int32, (N_HEADS * tq, 1), 0)
    col = jnp.full((N_HEADS * tq, 1), sink_ref[N_HEADS - 1], F32)
    for h in range(N_HEADS - 2, -1, -1):
        col = jnp.where(row < (h + 1) * tq, sink_ref[h], col)
    return col * LOG2E


def _win_attn_kernel(seq, sink_ref, q_ref, kl_ref, vl_ref, kc_ref, vc_ref, o_ref):
    lane_head = _split_half_head(_lane_iota())
    sink_col = _sink_column(sink_ref, BLK)
    shape = (BLK, 3 * BLK)
    rel = lax.broadcasted_iota(jnp.int32, shape, 1) - lax.broadcasted_iota(jnp.int32, shape, 0)
    for t in range(WIN_BLOCKS_PER_STEP):
        n = pl.program_id(1) * WIN_BLOCKS_PER_STEP + t
        start = pl.multiple_of(jnp.clip((n - 1) * BLK, 0, seq - 3 * BLK), BLK)
        qs = _stack_heads(q_ref[0, t * BLK:(t + 1) * BLK, :], lane_head, range(N_HEADS))
        kl = kl_ref[0, pl.ds(start, 3 * BLK), :]
        vl = vl_ref[0, pl.ds(start, 3 * BLK), :]
        s_loc = _dot_nt(qs, kl)
        s_ctx = _dot_nt(qs, kc_ref[0])
        dist = rel + (start - n * BLK)
        band = jnp.where(jnp.abs(dist) <= A_WINDOW, 0.0, NEG_INF)
        s_loc = (s_loc.reshape(N_HEADS, BLK, 3 * BLK) + band[None]).reshape(N_HEADS * BLK, 3 * BLK)
        o = _softmax_pv([s_loc, s_ctx], [vl, vc_ref[0]], sink_col)
        o_ref[0, t * BLK:(t + 1) * BLK, :] = _select_heads(o, BLK).astype(BF16)


def _window_attention(sink, q, k, v, kc, vc):
    b, seq, w = q.shape
    ctx_len = kc.shape[1]
    tq = WIN_BLOCKS_PER_STEP * BLK
    whole = pl.BlockSpec((1, seq, w), lambda i, n: (i, 0, 0))
    ctx_spec = pl.BlockSpec((1, ctx_len, w), lambda i, n: (i, 0, 0))
    blk = pl.BlockSpec((1, tq, w), lambda i, n: (i, n, 0))
    return pl.pallas_call(
        functools.partial(_win_attn_kernel, seq),
        grid=(b, seq // tq),
        in_specs=[pl.BlockSpec(memory_space=pltpu.SMEM), blk, whole, whole, ctx_spec, ctx_spec],
        out_specs=blk,
        out_shape=jax.ShapeDtypeStruct((b, seq, w), BF16),
        compiler_params=_compiler_params(),
        name="window_attention",
    )(sink, q, k, v, kc, vc)


def _na_bias_table(rpb):
    cq = np.arange(GRID_W)
    cs = np.clip(cq - NA_KW // 2, 0, GRID_W - NA_KW)
    col_valid = (cq[None, :] >= cs[:, None]) & (cq[None, :] < cs[:, None] + NA_KW)
    dc = np.clip(cq[None, :] - cq[:, None], -(NA_KW - 1), NA_KW - 1) + (NA_KW - 1)
    n_dr, n_dc = 2 * NA_KH - 1, 2 * NA_KW - 1
    onehot = (dc.reshape(1, -1) == np.arange(n_dc)[:, None]).astype(np.float32)
    t = jnp.dot(rpb.astype(F32).reshape(N_HEADS * n_dr, n_dc), onehot, precision=lax.Precision.HIGHEST)
    t = jnp.where(col_valid[None, None], t.reshape(N_HEADS, n_dr, GRID_W, GRID_W) * LOG2E, NEG_INF)
    tab = jnp.stack([t[:, off:off + NA_KH] for off in range(NA_KH)])
    return jnp.transpose(tab, (0, 1, 3, 2, 4)).reshape(NA_KH, N_HEADS * GRID_W, NA_KH * GRID_W)


def _na_row_start(r, rows):
    return jnp.clip(r - NA_KH // 2, 0, rows - NA_KH)


def _na_attn_kernel(rows, q_ref, kl_ref, vl_ref, kc_ref, vc_ref, bias_ref, o_ref):
    lane_head = _natural_head(_lane_iota())
    for t in range(NA_ROWS_PER_STEP):
        r = pl.program_id(1) * NA_ROWS_PER_STEP + t
        first = _na_row_start(r, rows)
        start = pl.multiple_of(first * GRID_W, GRID_W)
        qs = _stack_heads(q_ref[0, t * GRID_W:(t + 1) * GRID_W, :], lane_head, range(N_HEADS))
        kl = kl_ref[0, pl.ds(start, NA_KH * GRID_W), :]
        vl = vl_ref[0, pl.ds(start, NA_KH * GRID_W), :]
        s_loc = _dot_nt(qs, kl) + bias_ref[first - r + NA_KH - 1]
        s_ctx = _dot_nt(qs, kc_ref[0])
        o = _softmax_pv([s_loc, s_ctx], [vl, vc_ref[0]])
        o_ref[0, t * GRID_W:(t + 1) * GRID_W, :] = _select_heads(o, GRID_W).astype(BF16)


def _neighbourhood_attention(bias, q, k, v, kc, vc):
    b, seq, w = q.shape
    rows = seq // GRID_W
    ctx_len = kc.shape[1]
    tq = NA_ROWS_PER_STEP * GRID_W
    whole = pl.BlockSpec((1, seq, w), lambda i, r: (i, 0, 0))
    ctx_spec = pl.BlockSpec((1, ctx_len, w), lambda i, r: (i, 0, 0))
    blk = pl.BlockSpec((1, tq, w), lambda i, r: (i, r, 0))
    return pl.pallas_call(
        functools.partial(_na_attn_kernel, rows),
        grid=(b, seq // tq),
        in_specs=[blk, whole, whole, ctx_spec, ctx_spec, _resident(bias.shape)],
        out_specs=blk,
        out_shape=jax.ShapeDtypeStruct((b, seq, w), BF16),
        compiler_params=_compiler_params(),
        name="neighbourhood_attention",
    )(q, k, v, kc, vc, bias)


def _ctx_attn_kernel(split_layout, has_sink, *refs):
    if has_sink:
        sink_ref, q_ref, k_ref, v_ref, o_ref = refs
    else:
        q_ref, k_ref, v_ref, o_ref = refs
    tq = q_ref.shape[1]
    lane = _lane_iota()
    group = _split_half_head(lane) if split_layout else _natural_head(lane)
    qs = _stack_heads(q_ref[0], group, range(N_HEADS))
    s = _dot_nt(qs, k_ref[0])
    sink_col = _sink_column(sink_ref, tq) if has_sink else None
    o = _softmax_pv([s], [v_ref[0]], sink_col)
    o_ref[0] = _select_heads(o, tq).astype(BF16)


def _context_attention(q, k, v, sink=None, split_layout=False):
    b, n, w = q.shape
    spec = pl.BlockSpec((1, n, w), lambda i: (i, 0, 0))
    in_specs = [spec, spec, spec]
    args = [q, k, v]
    if sink is not None:
        in_specs = [pl.BlockSpec(memory_space=pltpu.SMEM)] + in_specs
        args = [sink] + args
    return pl.pallas_call(
        functools.partial(_ctx_attn_kernel, split_layout, sink is not None),
        grid=(b,),
        in_specs=in_specs,
        out_specs=spec,
        out_shape=jax.ShapeDtypeStruct((b, n, w), BF16),
        compiler_params=_compiler_params(),
        name="context_attention",
    )(*args)


def _diff_attn_kernel(has_local, lambda_init, lam_ref, subg_ref, qt_ref, *refs):
    if has_local:
        kl_ref, vtl_ref, kc_ref, vtc_ref, o_ref = refs
    else:
        kc_ref, vtc_ref, o_ref = refs
    tq = DIFF_TQ
    n_blocks = qt_ref.shape[2] // tq
    lv = lam_ref[...]
    lam = (jnp.exp(jnp.sum(lv[0:1] * lv[1:2], axis=-1, keepdims=True))
           - jnp.exp(jnp.sum(lv[2:3] * lv[3:4], axis=-1, keepdims=True)) + lambda_init)
    chan = lax.broadcasted_iota(jnp.int32, (GROUP_W, 1), 0)
    group = jnp.right_shift(jnp.bitwise_and(chan, 127), 4)
    zero = jnp.zeros((GROUP_W, tq), BF16)
    segs = [(kc_ref, vtc_ref, 0, kc_ref.shape[1])]
    if has_local:
        n_loc = kl_ref.shape[1]
        segs = [(kl_ref, vtl_ref, k0, DIFF_KEY_SEG) for k0 in range(0, n_loc, DIFF_KEY_SEG)] + segs
    sum_rows = 16

    def head_scores(unit):
        blk, h = unit
        qt = qt_ref[0, :, blk * tq:(blk + 1) * tq]
        qs = jnp.concatenate([jnp.where(group == 2 * h, qt, zero),
                              jnp.where(group == 2 * h + 1, qt, zero)], axis=1)
        return [_dot(k_ref[0, k0:k0 + nk, :], qs) for k_ref, _, k0, nk in segs]

    def column_max(s, slab=64):
        part = jnp.max(s.reshape(s.shape[0] // slab, slab, s.shape[1]), axis=0)
        return jnp.max(part, axis=0, keepdims=True)

    ys = []
    units = [(blk, h) for blk in range(n_blocks) for h in range(N_HEADS)]
    nxt = head_scores(units[0])
    for ui, (blk, h) in enumerate(units):
        scores = nxt
        if ui + 1 < len(units):
            nxt = head_scores(units[ui + 1])
        m = functools.reduce(jnp.maximum, [column_max(s) for s in scores])
        pv = None
        for s, (_, vt_ref, k0, nk) in zip(scores, segs):
            e = jnp.exp2(s - m).astype(BF16)
            vt = vt_ref[0, h * HEAD_DIM:(h + 1) * HEAD_DIM, k0:k0 + nk]
            lhs = jnp.concatenate([vt, jnp.ones((sum_rows, nk), BF16)], axis=0)
            p = _dot(lhs, e)
            pv = p if pv is None else pv + p
        inv = 1.0 / pv[HEAD_DIM:HEAD_DIM + 1]
        pv = pv[:HEAD_DIM]
        o = pv[:, :tq] * inv[:, :tq] - pv[:, tq:] * (lam * inv[:, tq:])
        ms = jnp.mean(o * o, axis=0, keepdims=True)
        ys.append(o * lax.rsqrt(ms + EPS))
        if h == N_HEADS - 1:
            out = jnp.concatenate(ys, axis=0).T
            o_ref[0, blk * tq:(blk + 1) * tq, :] = (out * subg_ref[...] * (1.0 - lambda_init)).astype(BF16)
            ys = []


def _diff_attention(lam_vecs, subg, lambda_init, qt, kc, vtc, k=None, vt=None):
    b, w, n = qt.shape
    ctx_len = kc.shape[1]
    has_local = k is not None
    tq = min(DIFF_TQ * DIFF_BLOCKS_PER_STEP, n)
    q_spec = pl.BlockSpec((1, w, tq), lambda i, j: (i, 0, j))
    in_specs = [pl.BlockSpec(lam_vecs.shape, lambda i, j: (0, 0)),
                pl.BlockSpec((1, w), lambda i, j: (0, 0)), q_spec]
    args = [lam_vecs, subg, qt]
    if has_local:
        in_specs += [pl.BlockSpec((1, n, w), lambda i, j: (i, 0, 0)),
                     pl.BlockSpec((1, w, n), lambda i, j: (i, 0, 0))]
        args += [k, vt]
    in_specs += [pl.BlockSpec((1, ctx_len, w), lambda i, j: (i, 0, 0)),
                 pl.BlockSpec((1, w, ctx_len), lambda i, j: (i, 0, 0))]
    args += [kc, vtc]
    return pl.pallas_call(
        functools.partial(_diff_attn_kernel, has_local, lambda_init),
        grid=(b, n // tq),
        in_specs=in_specs,
        out_specs=pl.BlockSpec((1, tq, w), lambda i, j: (i, j, 0)),
        out_shape=jax.ShapeDtypeStruct((b, n, w), BF16),
        compiler_params=_compiler_params(),
        name="diff_attention",
    )(*args)


def _conv_kernel(seq, u_ref, w_ref, b_ref, g_ref, beta_ref, o_ref, pad_ref):
    halo, tc, ch = CONV_HALO, CONV_TC, GROUP_W
    zeros = jnp.zeros((halo, ch), F32)
    pad_ref[0:halo, :] = zeros
    pad_ref[halo + seq:2 * halo + seq, :] = zeros

    def glu(i, carry):
        r0 = pl.multiple_of(i * tc, tc)
        u = u_ref[0, pl.ds(r0, tc), :]
        pad_ref[pl.ds(halo + r0, tc), :] = u[:, :ch] * _sigmoid(u[:, ch:])
        return carry

    lax.fori_loop(0, seq // tc, glu, 0)

    win_rows = tc + 2 * halo

    def chunk(i, carry):
        c0 = pl.multiple_of(i * tc, tc)
        win = pad_ref[pl.ds(c0, win_rows), :]
        acc = jnp.zeros((tc, ch), F32)
        for sub in range(8):
            shifted = win if sub == 0 else pltpu.roll(win, win_rows - sub, axis=0)
            for blk8 in range(win_rows // 8):
                tap = 8 * blk8 + sub - (halo - CONV_K // 2)
                if 0 <= tap < CONV_K:
                    acc = acc + shifted[8 * blk8:8 * blk8 + tc] * w_ref[tap:tap + 1, :]
        hcv = acc + b_ref[...]
        mu = jnp.mean(hcv, axis=-1, keepdims=True)
        cen = hcv - mu
        var = jnp.mean(cen * cen, axis=-1, keepdims=True)
        y = cen * lax.rsqrt(var + EPS) * g_ref[...] + beta_ref[...]
        o_ref[0, pl.ds(c0, tc), :] = (y * _sigmoid(y)).astype(BF16)
        return carry

    lax.fori_loop(0, seq // tc, chunk, 0, unroll=4)


def _conformer_conv(u, w, bias, ln_g, ln_b):
    b, seq, two_ch = u.shape
    ch = two_ch // 2
    vec = pl.BlockSpec((1, ch), lambda i: (0, 0))
    return pl.pallas_call(
        functools.partial(_conv_kernel, seq),
        grid=(b,),
        in_specs=[pl.BlockSpec((1, seq, two_ch), lambda i: (i, 0, 0)),
                  pl.BlockSpec((CONV_K, ch), lambda i: (0, 0)), vec, vec, vec],
        out_specs=pl.BlockSpec((1, seq, ch), lambda i: (i, 0, 0)),
        out_shape=jax.ShapeDtypeStruct((b, seq, ch), BF16),
        scratch_shapes=[pltpu.VMEM((seq + 2 * CONV_HALO, ch), F32)],
        compiler_params=_compiler_params(),
        name="conformer_conv",
    )(u, w, bias.reshape(1, ch), ln_g.reshape(1, ch), ln_b.reshape(1, ch))


def _ffn_kernel(final, x_ref, ya_ref, yb_ref, yc_ref, yd_ref, g1_ref, sh_ref, sc_ref, g2_ref, ng_ref,
                wo_ref, wg_ref, wu_ref, wd_ref, *rest):
    if final:
        fg_ref, o_ref, ycat_ref, act_ref = rest
    else:
        o_ref, ycat_ref, act_ref = rest
    for j, y_ref in enumerate((ya_ref, yb_ref, yc_ref, yd_ref)):
        ycat_ref[:, j * GROUP_W:(j + 1) * GROUP_W] = y_ref[...]
    x = x_ref[...] + g1_ref[0] * _dot(ycat_ref[...], wo_ref[...])
    r = lax.rsqrt(jnp.mean(x * x, axis=-1, keepdims=True) + EPS)
    h = (x * r) * ng_ref[...]
    hb = (h * (1.0 + sc_ref[0]) + sh_ref[0]).astype(BF16)
    for c in range(0, FFN_HIDDEN, FFN_CHUNK):
        gate = _dot(hb, wg_ref[:, c:c + FFN_CHUNK])
        up = _dot(hb, wu_ref[:, c:c + FFN_CHUNK])
        act_ref[:, c:c + FFN_CHUNK] = (gate * _sigmoid(gate) * up).astype(BF16)
    x = x + g2_ref[0] * _dot(act_ref[...], wd_ref[...])
    if final:
        r = lax.rsqrt(jnp.mean(x * x, axis=-1, keepdims=True) + EPS)
        x = (x * r) * fg_ref[...]
    o_ref[...] = x


def _out_projection_ffn(x, ys, mods, mod_rows, norm_g, wo, wg, wu, wd, final_g=None):
    rows, d = x.shape
    tm = ROW_TILE
    final = final_g is not None
    row_spec = pl.BlockSpec((tm, d), lambda i: (i, 0))
    y_spec = pl.BlockSpec((tm, GROUP_W), lambda i: (i, 0))
    vec_spec = pl.BlockSpec((1, d), lambda i: (0, 0))
    in_specs = ([row_spec, y_spec, y_spec, y_spec, y_spec]
                + [_mod_spec(d, chunk, mod_rows, tm) for chunk in (2, 3, 4, 5)]
                + [vec_spec, _resident(wo.shape), _resident(wg.shape), _resident(wu.shape), _resident(wd.shape)])
    args = [x, *ys, mods, mods, mods, mods, norm_g.reshape(1, d), wo, wg, wu, wd]
    if final:
        in_specs.append(vec_spec)
        args.append(final_g.reshape(1, d))
    return pl.pallas_call(
        functools.partial(_ffn_kernel, final),
        grid=(rows // tm,),
        in_specs=in_specs,
        out_specs=row_spec,
        out_shape=jax.ShapeDtypeStruct((rows, d), F32),
        scratch_shapes=[pltpu.VMEM((tm, d), BF16), pltpu.VMEM((tm, FFN_HIDDEN), BF16)],
        compiler_params=_compiler_params(),
        name="out_projection_ffn",
    )(*args)


def _rope_tables(n_tok, dim):
    t = jnp.arange(n_tok)
    row = (t // GRID_W).astype(F32)
    col = (t % GRID_W).astype(F32)
    nf = dim // 4
    inv = ROPE_BASE ** (-jnp.arange(nf, dtype=F32) / nf)
    ang = jnp.concatenate([row[:, None] * inv, col[:, None] * inv], axis=-1)
    reps = 128 // (dim // 2)
    return jnp.tile(jnp.cos(ang), (1, reps)), jnp.tile(jnp.sin(ang), (1, reps))


def kernel(x, c, ctx, c_ctx, norm1_g, norm2_g, w_ada, b_ada, w_in, w_out, attn_sink, conv_w, conv_b,
           conv_ln_g, conv_ln_b, diff_lq1, diff_lk1, diff_lq2, diff_lk2, diff_subln_g, na_rpb,
           w_gate, w_up, w_down, final_g):
    batch, seq, d = x.shape
    ctx_len = ctx.shape[1]
    depth = w_ada.shape[0]

    cvec = jnp.zeros((ADA_ROWS, d), F32).at[:batch].set(c).at[batch].set(c_ctx)
    mods_all = _ada_table(cvec, w_ada, b_ada).reshape(depth, ADA_ROWS * 6, 1, d)
    rope = _rope_tables(seq, HEAD_DIM) + _rope_tables(seq, C_QK_DIM)
    lat_rows = (0, seq)
    ctx_rows = (batch, batch * ctx_len)

    def channel_major(t):
        return jnp.swapaxes(t, 1, 2)

    xl = x.reshape(batch * seq, d)
    xc = ctx.reshape(batch * ctx_len, d)
    for l in range(depth):
        ctx_needed = l < depth - 1
        mods = mods_all[l]
        lambda_init = 0.8 - 0.6 * math.exp(-0.3 * l)
        w_in_l = _relayout_w_in(w_in[l])
        wo, wg, wu, wd = (w_out[l].astype(BF16), w_gate[l].astype(BF16),
                          w_up[l].astype(BF16), w_down[l].astype(BF16))
        lam_vecs = jnp.stack([diff_lq1[l], diff_lk1[l], diff_lq2[l], diff_lk2[l]]).astype(F32)
        subg = jnp.tile(diff_subln_g[l], N_HEADS).reshape(1, GROUP_W)
        cw = conv_w[l].reshape(CONV_K, GROUP_W)

        pl_lat = _in_projection(xl, norm1_g[l], mods, lat_rows, w_in_l, rope)
        pl_ctx = _in_projection(xc, norm1_g[l], mods, ctx_rows, w_in_l, None)
        qa, ka, va, ub, qc, kc, vc, qd, kd, vd = [t.reshape(batch, seq, -1) for t in pl_lat]
        qa_c, ka_c, va_c, ub_c, qc_c, kc_c, vc_c, qd_c, kd_c, vd_c = [
            t.reshape(batch, ctx_len, -1) for t in pl_ctx]
        vtc_c = channel_major(vc_c)

        y_a = _window_attention(attn_sink[l], qa, ka, va, ka_c, va_c)
        y_b = _conformer_conv(ub, cw, conv_b[l], conv_ln_g[l], conv_ln_b[l])
        y_c = _diff_attention(lam_vecs, subg, lambda_init, channel_major(qc), kc_c, vtc_c, kc, channel_major(vc))
        y_d = _neighbourhood_attention(_na_bias_table(na_rpb[l]), qd, kd, vd, kd_c, vd_c)
        ys = [t.reshape(batch * seq, GROUP_W) for t in (y_a, y_b, y_c, y_d)]
        xl = _out_projection_ffn(xl, ys, mods, lat_rows, norm2_g[l], wo, wg, wu, wd,
                                 final_g=None if ctx_needed else final_g)
        if ctx_needed:
            yc_a = _context_attention(qa_c, ka_c, va_c, sink=attn_sink[l], split_layout=True)
            yc_b = _conformer_conv(ub_c, cw, conv_b[l], conv_ln_g[l], conv_ln_b[l])
            yc_c = _diff_attention(lam_vecs, subg, lambda_init, channel_major(qc_c), kc_c, vtc_c)
            yc_d = _context_attention(qd_c, kd_c, vd_c)
            ycs = [t.reshape(batch * ctx_len, GROUP_W) for t in (yc_a, yc_b, yc_c, yc_d)]
            xc = _out_projection_ffn(xc, ycs, mods, ctx_rows, norm2_g[l], wo, wg, wu, wd)
    return xl.reshape(batch, seq, d)
```

```python
import functools
import math

import numpy as np
import jax
import jax.numpy as jnp
from jax import lax
from jax.experimental import pallas as pl
from jax.experimental.pallas import tpu as pltpu

F32 = jnp.float32
BF16 = jnp.bfloat16

D_MODEL = 1024
DEPTH = 2
GRID_W = 64
HEAD_DIM = 64
GROUP_W = 256
N_HEADS = 4
A_KV_HEADS = 2
A_WINDOW = 128
BLK = 128
CONV_K = 31
C_QK_DIM = 32
NA_KH = 8
NA_KW = 16
FFN_HIDDEN = 2816
ROPE_BASE = 10000.0
EPS = 1e-6
NEG_INF = -1e30

VMEM_LIMIT = 56 * 1024 * 1024
ADA_ROWS = 16
ADA_TN = 1536
ROW_TILE = 512
FFN_CHUNK = 256
CONV_TC = 64
CONV_HALO = 16
DIFF_TQ = 512
LOG2E = math.log2(math.e)
ATTN_Q_SCALE = (HEAD_DIM ** -0.5) * LOG2E
DIFF_Q_SCALE = (C_QK_DIM ** -0.5) * LOG2E
DIFF_BLOCKS_PER_STEP = 2
DIFF_KEY_SEG = 1024
WIN_BLOCKS_PER_STEP = 4
NA_ROWS_PER_STEP = 8


def _compiler_params():
    return pltpu.CompilerParams(vmem_limit_bytes=VMEM_LIMIT)


def _resident(shape):
    return pl.BlockSpec(shape, lambda *_: (0,) * len(shape), pipeline_mode=pl.Buffered(1))


def _sigmoid(v):
    return 1.0 / (1.0 + jnp.exp(-v))


def _dot(a, b):
    return jnp.dot(a, b, preferred_element_type=F32)


def _dot_nt(a, b):
    return lax.dot_general(a, b, (((1,), (1,)), ((), ())), preferred_element_type=F32)


def _ada_kernel(c_ref, w_ref, b_ref, o_ref):
    cv = c_ref[...]
    s = cv * _sigmoid(cv)
    o_ref[0] = _dot(s.astype(BF16), w_ref[0].astype(BF16)) + b_ref[0]


def _ada_table(cvec, w_ada, b_ada):
    depth, d, n = w_ada.shape
    return pl.pallas_call(
        _ada_kernel,
        grid=(depth, n // ADA_TN),
        in_specs=[
            pl.BlockSpec((ADA_ROWS, d), lambda l, j: (0, 0)),
            pl.BlockSpec((1, d, ADA_TN), lambda l, j: (l, 0, j)),
            pl.BlockSpec((1, 1, ADA_TN), lambda l, j: (l, 0, j)),
        ],
        out_specs=pl.BlockSpec((1, ADA_ROWS, ADA_TN), lambda l, j: (l, 0, j)),
        out_shape=jax.ShapeDtypeStruct((depth, ADA_ROWS, n), F32),
        compiler_params=_compiler_params(),
        name="ada_table",
    )(cvec, w_ada, b_ada.reshape(depth, 1, n))


def _relayout_w_in(w):
    d = w.shape[0]
    w = w.astype(BF16)

    def split_halves(cols, groups, half):
        return jnp.transpose(cols.reshape(d, groups, 2, half), (0, 2, 1, 3)).reshape(d, GROUP_W)

    qa = split_halves(w[:, 0:256], N_HEADS, 32)
    ka = split_halves(jnp.repeat(w[:, 256:384].reshape(d, A_KV_HEADS, HEAD_DIM), 2, axis=1), N_HEADS, 32)
    va = jnp.repeat(w[:, 384:512].reshape(d, A_KV_HEADS, HEAD_DIM), 2, axis=1).reshape(d, GROUP_W)
    qc = split_halves(w[:, 1024:1280], 2 * N_HEADS, 16)
    kc = split_halves(w[:, 1280:1536], 2 * N_HEADS, 16)
    return jnp.concatenate([qa, ka, va, w[:, 512:1024], qc, kc, w[:, 1536:]], axis=1)


PROJ_WIDTH = 9 * GROUP_W + 512
OFF_QA, OFF_KA, OFF_VA, OFF_UB, OFF_QC, OFF_KC, OFF_VC, OFF_QD, OFF_KD, OFF_VD = (
    0, 256, 512, 768, 1280, 1536, 1792, 2048, 2304, 2560)


def _inproj_kernel(rope, x_ref, g_ref, sh_ref, sc_ref, w_ref, *rest):
    if rope:
        ca_ref, sa_ref, cc_ref, sc2_ref = rest[:4]
        rest = rest[4:]
    qa_o, ka_o, va_o, ub_o, qc_o, kc_o, vc_o, qd_o, kd_o, vd_o = rest
    x = x_ref[...]
    r = lax.rsqrt(jnp.mean(x * x, axis=-1, keepdims=True) + EPS)
    h = (x * r) * g_ref[...]
    h = h * (1.0 + sc_ref[0]) + sh_ref[0]
    hb = h.astype(BF16)

    def proj(off, width=GROUP_W):
        return _dot(hb, w_ref[:, off:off + width])

    def store_rot(o_ref, y, c_ref, s_ref, scale, channel_major=False):
        if rope:
            x1, x2 = y[:, :128], y[:, 128:]
            cs, sn = c_ref[...], s_ref[...]
            y1, y2 = x1 * cs - x2 * sn, x1 * sn + x2 * cs
        else:
            y1, y2 = y[:, :128], y[:, 128:]
        if channel_major:
            o_ref[:128, :] = (y1 * scale).T.astype(BF16)
            o_ref[128:, :] = (y2 * scale).T.astype(BF16)
        else:
            o_ref[:, :128] = (y1 * scale).astype(BF16)
            o_ref[:, 128:] = (y2 * scale).astype(BF16)

    ca = sa = cc = sc2 = None
    if rope:
        ca, sa, cc, sc2 = ca_ref, sa_ref, cc_ref, sc2_ref
    store_rot(qa_o, proj(OFF_QA), ca, sa, ATTN_Q_SCALE)
    store_rot(ka_o, proj(OFF_KA), ca, sa, 1.0)
    va_o[...] = proj(OFF_VA).astype(BF16)
    ub_o[...] = proj(OFF_UB, 512)
    store_rot(qc_o, proj(OFF_QC), cc, sc2, DIFF_Q_SCALE, channel_major=True)
    store_rot(kc_o, proj(OFF_KC), cc, sc2, 1.0)
    vc_o[...] = proj(OFF_VC).T.astype(BF16)
    qd_o[...] = (proj(OFF_QD) * ATTN_Q_SCALE).astype(BF16)
    kd_o[...] = proj(OFF_KD).astype(BF16)
    vd_o[...] = proj(OFF_VD).astype(BF16)


def _mod_spec(d, chunk, mod_rows, tm):
    first_row, rows_per_mod = mod_rows
    tiles_per_mod = rows_per_mod // tm
    return pl.BlockSpec((1, 1, d), lambda i: ((first_row + i // tiles_per_mod) * 6 + chunk, 0, 0))


def _in_projection(x, g, mods, mod_rows, w, rope_tables):
    rows, d = x.shape
    tm = ROW_TILE
    rope = rope_tables is not None
    in_specs = [
        pl.BlockSpec((tm, d), lambda i: (i, 0)),
        pl.BlockSpec((1, d), lambda i: (0, 0)),
        _mod_spec(d, 0, mod_rows, tm), _mod_spec(d, 1, mod_rows, tm),
        _resident((d, PROJ_WIDTH)),
    ]
    args = [x, g.reshape(1, d), mods, mods, w]
    if rope:
        tiles_per_seq = rope_tables[0].shape[0] // tm
        tab_spec = pl.BlockSpec((tm, 128), lambda i: (i % tiles_per_seq, 0))
        in_specs += [tab_spec] * 4
        args += list(rope_tables)
    narrow = pl.BlockSpec((tm, GROUP_W), lambda i: (i, 0))
    wide = pl.BlockSpec((tm, 512), lambda i: (i, 0))
    chan_major = pl.BlockSpec((GROUP_W, tm), lambda i: (0, i))
    out_specs = [narrow, narrow, narrow, wide, chan_major, narrow, chan_major, narrow, narrow, narrow]
    bf = jax.ShapeDtypeStruct((rows, GROUP_W), BF16)
    bf_t = jax.ShapeDtypeStruct((GROUP_W, rows), BF16)
    out_shape = [bf, bf, bf, jax.ShapeDtypeStruct((rows, 512), F32), bf_t, bf, bf_t, bf, bf, bf]
    return pl.pallas_call(
        functools.partial(_inproj_kernel, rope),
        grid=(rows // tm,),
        in_specs=in_specs,
        out_specs=out_specs,
        out_shape=out_shape,
        compiler_params=_compiler_params(),
        name="in_projection",
    )(*args)


def _lane_iota():
    return lax.broadcasted_iota(jnp.int32, (1, GROUP_W), 1)


def _split_half_head(lane):
    return jnp.right_shift(jnp.bitwise_and(lane, 127), 5)


def _natural_head(lane):
    return jnp.right_shift(lane, 6)


def _stack_heads(q, lane_group, groups):
    zero = jnp.zeros_like(q)
    return jnp.concatenate([jnp.where(lane_group == g, q, zero) for g in groups], axis=0)


def _softmax_pv(scores, values, sink_col=None):
    m = functools.reduce(jnp.maximum, [jnp.max(s, axis=-1, keepdims=True) for s in scores])
    if sink_col is not None:
        m = jnp.maximum(m, sink_col)
    es = [jnp.exp2(s - m) for s in scores]
    l = functools.reduce(jnp.add, [jnp.sum(e, axis=-1, keepdims=True) for e in es])
    if sink_col is not None:
        l = l + jnp.exp2(sink_col - m)
    o = functools.reduce(jnp.add, [_dot(e.astype(BF16), v) for e, v in zip(es, values)])
    return o / l


def _select_heads(o, tq):
    head = _natural_head(_lane_iota())
    out = o[0:tq]
    for h in range(1, N_HEADS):
        out = jnp.where(head == h, o[h * tq:(h + 1) * tq], out)
    return out


def _sink_column(sink_ref, tq):
    row = lax.broadcasted_iota(jnp.int32, (N_HEADS * tq, 1), 0)
    col = jnp.full((N_HEADS * tq, 1), sink_ref[N_HEADS - 1], F32)
    for h in range(N_HEADS - 2, -1, -1):
        col = jnp.where(row < (h + 1) * tq, sink_ref[h], col)
    return col * LOG2E


def _win_attn_kernel(seq, sink_ref, q_ref, kl_ref, vl_ref, kc_ref, vc_ref, o_ref):
    lane_head = _split_half_head(_lane_iota())
    sink_col = _sink_column(sink_ref, BLK)
    shape = (BLK, 3 * BLK)
    rel = lax.broadcasted_iota(jnp.int32, shape, 1) - lax.broadcasted_iota(jnp.int32, shape, 0)
    for t in range(WIN_BLOCKS_PER_STEP):
        n = pl.program_id(1) * WIN_BLOCKS_PER_STEP + t
        start = pl.multiple_of(jnp.clip((n - 1) * BLK, 0, seq - 3 * BLK), BLK)
        qs = _stack_heads(q_ref[0, t * BLK:(t + 1) * BLK, :], lane_head, range(N_HEADS))
        kl = kl_ref[0, pl.ds(start, 3 * BLK), :]
        vl = vl_ref[0, pl.ds(start, 3 * BLK), :]
        s_loc = _dot_nt(qs, kl)
        s_ctx = _dot_nt(qs, kc_ref[0])
        dist = rel + (start - n * BLK)
        band = jnp.where(jnp.abs(dist) <= A_WINDOW, 0.0, NEG_INF)
        s_loc = (s_loc.reshape(N_HEADS, BLK, 3 * BLK) + band[None]).reshape(N_HEADS * BLK, 3 * BLK)
        o = _softmax_pv([s_loc, s_ctx], [vl, vc_ref[0]], sink_col)
        o_ref[0, t * BLK:(t + 1) * BLK, :] = _select_heads(o, BLK).astype(BF16)


def _window_attention(sink, q, k, v, kc, vc):
    b, seq, w = q.shape
    ctx_len = kc.shape[1]
    tq = WIN_BLOCKS_PER_STEP * BLK
    whole = pl.BlockSpec((1, seq, w), lambda i, n: (i, 0, 0))
    ctx_spec = pl.BlockSpec((1, ctx_len, w), lambda i, n: (i, 0, 0))
    blk = pl.BlockSpec((1, tq, w), lambda i, n: (i, n, 0))
    return pl.pallas_call(
        functools.partial(_win_attn_kernel, seq),
        grid=(b, seq // tq),
        in_specs=[pl.BlockSpec(memory_space=pltpu.SMEM), blk, whole, whole, ctx_spec, ctx_spec],
        out_specs=blk,
        out_shape=jax.ShapeDtypeStruct((b, seq, w), BF16),
        compiler_params=_compiler_params(),
        name="window_attention",
    )(sink, q, k, v, kc, vc)


def _na_bias_table(rpb):
    cq = np.arange(GRID_W)
    cs = np.clip(cq - NA_KW // 2, 0, GRID_W - NA_KW)
    col_valid = (cq[None, :] >= cs[:, None]) & (cq[None, :] < cs[:, None] + NA_KW)
    dc = np.clip(cq[None, :] - cq[:, None], -(NA_KW - 1), NA_KW - 1) + (NA_KW - 1)
    n_dr, n_dc = 2 * NA_KH - 1, 2 * NA_KW - 1
    onehot = (dc.reshape(1, -1) == np.arange(n_dc)[:, None]).astype(np.float32)
    t = jnp.dot(rpb.astype(F32).reshape(N_HEADS * n_dr, n_dc), onehot, precision=lax.Precision.HIGHEST)
    t = jnp.where(col_valid[None, None], t.reshape(N_HEADS, n_dr, GRID_W, GRID_W) * LOG2E, NEG_INF)
    tab = jnp.stack([t[:, off:off + NA_KH] for off in range(NA_KH)])
    return jnp.transpose(tab, (0, 1, 3, 2, 4)).reshape(NA_KH, N_HEADS * GRID_W, NA_KH * GRID_W)


def _na_row_start(r, rows):
    return jnp.clip(r - NA_KH // 2, 0, rows - NA_KH)


def _na_attn_kernel(rows, q_ref, kl_ref, vl_ref, kc_ref, vc_ref, bias_ref, o_ref):
    lane_head = _natural_head(_lane_iota())
    for t in range(NA_ROWS_PER_STEP):
        r = pl.program_id(1) * NA_ROWS_PER_STEP + t
        first = _na_row_start(r, rows)
        start = pl.multiple_of(first * GRID_W, GRID_W)
        qs = _stack_heads(q_ref[0, t * GRID_W:(t + 1) * GRID_W, :], lane_head, range(N_HEADS))
        kl = kl_ref[0, pl.ds(start, NA_KH * GRID_W), :]
        vl = vl_ref[0, pl.ds(start, NA_KH * GRID_W), :]
        s_loc = _dot_nt(qs, kl) + bias_ref[first - r + NA_KH - 1]
        s_ctx = _dot_nt(qs, kc_ref[0])
        o = _softmax_pv([s_loc, s_ctx], [vl, vc_ref[0]])
        o_ref[0, t * GRID_W:(t + 1) * GRID_W, :] = _select_heads(o, GRID_W).astype(BF16)


def _neighbourhood_attention(bias, q, k, v, kc, vc):
    b, seq, w = q.shape
    rows = seq // GRID_W
    ctx_len = kc.shape[1]
    tq = NA_ROWS_PER_STEP * GRID_W
    whole = pl.BlockSpec((1, seq, w), lambda i, r: (i, 0, 0))
    ctx_spec = pl.BlockSpec((1, ctx_len, w), lambda i, r: (i, 0, 0))
    blk = pl.BlockSpec((1, tq, w), lambda i, r: (i, r, 0))
    return pl.pallas_call(
        functools.partial(_na_attn_kernel, rows),
        grid=(b, seq // tq),
        in_specs=[blk, whole, whole, ctx_spec, ctx_spec, _resident(bias.shape)],
        out_specs=blk,
        out_shape=jax.ShapeDtypeStruct((b, seq, w), BF16),
        compiler_params=_compiler_params(),
        name="neighbourhood_attention",
    )(q, k, v, kc, vc, bias)


def _ctx_attn_kernel(split_layout, has_sink, *refs):
    if has_sink:
        sink_ref, q_ref, k_ref, v_ref, o_ref = refs
    else:
        q_ref, k_ref, v_ref, o_ref = refs
    tq = q_ref.shape[1]
    lane = _lane_iota()
    group = _split_half_head(lane) if split_layout else _natural_head(lane)
    qs = _stack_heads(q_ref[0], group, range(N_HEADS))
    s = _dot_nt(qs, k_ref[0])
    sink_col = _sink_column(sink_ref, tq) if has_sink else None
    o = _softmax_pv([s], [v_ref[0]], sink_col)
    o_ref[0] = _select_heads(o, tq).astype(BF16)


def _context_attention(q, k, v, sink=None, split_layout=False):
    b, n, w = q.shape
    spec = pl.BlockSpec((1, n, w), lambda i: (i, 0, 0))
    in_specs = [spec, spec, spec]
    args = [q, k, v]
    if sink is not None:
        in_specs = [pl.BlockSpec(memory_space=pltpu.SMEM)] + in_specs
        args = [sink] + args
    return pl.pallas_call(
        functools.partial(_ctx_attn_kernel, split_layout, sink is not None),
        grid=(b,),
        in_specs=in_specs,
        out_specs=spec,
        out_shape=jax.ShapeDtypeStruct((b, n, w), BF16),
        compiler_params=_compiler_params(),
        name="context_attention",
    )(*args)


def _diff_attn_kernel(has_local, lambda_init, lam_ref, subg_ref, qt_ref, *refs):
    if has_local:
        kl_ref, vtl_ref, kc_ref, vtc_ref, o_ref = refs
    else:
        kc_ref, vtc_ref, o_ref = refs
    tq = min(DIFF_TQ, qt_ref.shape[1])
    n_blocks = qt_ref.shape[1] // tq
    lv = lam_ref[...]
    lam = (jnp.exp(jnp.sum(lv[0:1] * lv[1:2], axis=-1, keepdims=True))
           - jnp.exp(jnp.sum(lv[2:3] * lv[3:4], axis=-1, keepdims=True)) + lambda_init)
    chan = lax.broadcasted_iota(jnp.int32, (GROUP_W, 1), 0)
    group = jnp.right_shift(jnp.bitwise_and(chan, 127), 4)
    zero = jnp.zeros((GROUP_W, tq), BF16)
    segs = [(kc_ref, vtc_ref, 0, kc_ref.shape[1])]
    if has_local:
        n_loc = kl_ref.shape[1]
        segs = [(kl_ref, vtl_ref, k0, DIFF_KEY_SEG) for k0 in range(0, n_loc, DIFF_KEY_SEG)] + segs
    sum_rows = 16

    def head_scores(unit):
        blk, h = unit
        qt = qt_ref[:, blk * tq:(blk + 1) * tq]
        qs = jnp.concatenate([jnp.where(group == 2 * h, qt, zero),
                              jnp.where(group == 2 * h + 1, qt, zero)], axis=1)
        return [_dot(k_ref[0, k0:k0 + nk, :], qs) for k_ref, _, k0, nk in segs]

    def column_max(s, slab=64):
        part = jnp.max(s.reshape(s.shape[0] // slab, slab, s.shape[1]), axis=0)
        return jnp.max(part, axis=0, keepdims=True)

    ys = []
    units = [(blk, h) for blk in range(n_blocks) for h in range(N_HEADS)]
    nxt = head_scores(units[0])
    for ui, (blk, h) in enumerate(units):
        scores = nxt
        if ui + 1 < len(units):
            nxt = head_scores(units[ui + 1])
        m = functools.reduce(jnp.maximum, [column_max(s) for s in scores])
        pv = None
        for s, (_, vt_ref, k0, nk) in zip(scores, segs):
            e = jnp.exp2(s - m).astype(BF16)
            vt = vt_ref[h * HEAD_DIM:(h + 1) * HEAD_DIM, k0:k0 + nk]
            lhs = jnp.concatenate([vt, jnp.ones((sum_rows, nk), BF16)], axis=0)
            p = _dot(lhs, e)
            pv = p if pv is None else pv + p
        inv = 1.0 / pv[HEAD_DIM:HEAD_DIM + 1]
        pv = pv[:HEAD_DIM]
        o = pv[:, :tq] * inv[:, :tq] - pv[:, tq:] * (lam * inv[:, tq:])
        ms = jnp.mean(o * o, axis=0, keepdims=True)
        ys.append(o * lax.rsqrt(ms + EPS))
        if h == N_HEADS - 1:
            out = jnp.concatenate(ys, axis=0).T
            o_ref[0, blk * tq:(blk + 1) * tq, :] = (out * subg_ref[...] * (1.0 - lambda_init)).astype(BF16)
            ys = []


def _diff_attention(lam_vecs, subg, lambda_init, b, qt, kc, vtc, k=None, vt=None):
    w = qt.shape[0]
    n = qt.shape[1] // b
    ctx_len = kc.shape[1]
    has_local = k is not None
    tq = min(DIFF_TQ * DIFF_BLOCKS_PER_STEP, n)
    steps = n // tq
    q_spec = pl.BlockSpec((w, tq), lambda i, j: (0, i * steps + j))
    in_specs = [pl.BlockSpec(lam_vecs.shape, lambda i, j: (0, 0)),
                pl.BlockSpec((1, w), lambda i, j: (0, 0)), q_spec]
    args = [lam_vecs, subg, qt]
    if has_local:
        in_specs += [pl.BlockSpec((1, n, w), lambda i, j: (i, 0, 0)),
                     pl.BlockSpec((w, n), lambda i, j: (0, i))]
        args += [k, vt]
    in_specs += [pl.BlockSpec((1, ctx_len, w), lambda i, j: (i, 0, 0)),
                 pl.BlockSpec((w, ctx_len), lambda i, j: (0, i))]
    args += [kc, vtc]
    return pl.pallas_call(
        functools.partial(_diff_attn_kernel, has_local, lambda_init),
        grid=(b, n // tq),
        in_specs=in_specs,
        out_specs=pl.BlockSpec((1, tq, w), lambda i, j: (i, j, 0)),
        out_shape=jax.ShapeDtypeStruct((b, n, w), BF16),
        compiler_params=_compiler_params(),
        name="diff_attention",
    )(*args)


def _conv_kernel(seq, u_ref, w_ref, b_ref, g_ref, beta_ref, o_ref, pad_ref):
    halo, tc, ch = CONV_HALO, CONV_TC, GROUP_W
    zeros = jnp.zeros((halo, ch), F32)
    pad_ref[0:halo, :] = zeros
    pad_ref[halo + seq:2 * halo + seq, :] = zeros

    def glu(i, carry):
        r0 = pl.multiple_of(i * tc, tc)
        u = u_ref[0, pl.ds(r0, tc), :]
        pad_ref[pl.ds(halo + r0, tc), :] = u[:, :ch] * _sigmoid(u[:, ch:])
        return carry

    lax.fori_loop(0, seq // tc, glu, 0)

    win_rows = tc + 2 * halo

    def chunk(i, carry):
        c0 = pl.multiple_of(i * tc, tc)
        win = pad_ref[pl.ds(c0, win_rows), :]
        acc = jnp.zeros((tc, ch), F32)
        for sub in range(8):
            shifted = win if sub == 0 else pltpu.roll(win, win_rows - sub, axis=0)
            for blk8 in range(win_rows // 8):
                tap = 8 * blk8 + sub - (halo - CONV_K // 2)
                if 0 <= tap < CONV_K:
                    acc = acc + shifted[8 * blk8:8 * blk8 + tc] * w_ref[tap:tap + 1, :]
        hcv = acc + b_ref[...]
        mu = jnp.mean(hcv, axis=-1, keepdims=True)
        cen = hcv - mu
        var = jnp.mean(cen * cen, axis=-1, keepdims=True)
        y = cen * lax.rsqrt(var + EPS) * g_ref[...] + beta_ref[...]
        o_ref[0, pl.ds(c0, tc), :] = (y * _sigmoid(y)).astype(BF16)
        return carry

    lax.fori_loop(0, seq // tc, chunk, 0, unroll=4)


def _conformer_conv(u, w, bias, ln_g, ln_b):
    b, seq, two_ch = u.shape
    ch = two_ch // 2
    vec = pl.BlockSpec((1, ch), lambda i: (0, 0))
    return pl.pallas_call(
        functools.partial(_conv_kernel, seq),
        grid=(b,),
        in_specs=[pl.BlockSpec((1, seq, two_ch), lambda i: (i, 0, 0)),
                  pl.BlockSpec((CONV_K, ch), lambda i: (0, 0)), vec, vec, vec],
        out_specs=pl.BlockSpec((1, seq, ch), lambda i: (i, 0, 0)),
        out_shape=jax.ShapeDtypeStruct((b, seq, ch), BF16),
        scratch_shapes=[pltpu.VMEM((seq + 2 * CONV_HALO, ch), F32)],
        compiler_params=_compiler_params(),
        name="conformer_conv",
    )(u, w, bias.reshape(1, ch), ln_g.reshape(1, ch), ln_b.reshape(1, ch))


def _ffn_kernel(final, x_ref, ya_ref, yb_ref, yc_ref, yd_ref, g1_ref, sh_ref, sc_ref, g2_ref, ng_ref,
                wo_ref, wg_ref, wu_ref, wd_ref, *rest):
    if final:
        fg_ref, o_ref, ycat_ref, act_ref = rest
    else:
        o_ref, ycat_ref, act_ref = rest
    for j, y_ref in enumerate((ya_ref, yb_ref, yc_ref, yd_ref)):
        ycat_ref[:, j * GROUP_W:(j + 1) * GROUP_W] = y_ref[...]
    x = x_ref[...] + g1_ref[0] * _dot(ycat_ref[...], wo_ref[...])
    r = lax.rsqrt(jnp.mean(x * x, axis=-1, keepdims=True) + EPS)
    h = (x * r) * ng_ref[...]
    hb = (h * (1.0 + sc_ref[0]) + sh_ref[0]).astype(BF16)
    for c in range(0, FFN_HIDDEN, FFN_CHUNK):
        gate = _dot(hb, wg_ref[:, c:c + FFN_CHUNK])
        up = _dot(hb, wu_ref[:, c:c + FFN_CHUNK])
        act_ref[:, c:c + FFN_CHUNK] = (gate * _sigmoid(gate) * up).astype(BF16)
    x = x + g2_ref[0] * _dot(act_ref[...], wd_ref[...])
    if final:
        r = lax.rsqrt(jnp.mean(x * x, axis=-1, keepdims=True) + EPS)
        x = (x * r) * fg_ref[...]
    o_ref[...] = x


def _out_projection_ffn(x, ys, mods, mod_rows, norm_g, wo, wg, wu, wd, final_g=None):
    rows, d = x.shape
    tm = ROW_TILE
    final = final_g is not None
    row_spec = pl.BlockSpec((tm, d), lambda i: (i, 0))
    y_spec = pl.BlockSpec((tm, GROUP_W), lambda i: (i, 0))
    vec_spec = pl.BlockSpec((1, d), lambda i: (0, 0))
    in_specs = ([row_spec, y_spec, y_spec, y_spec, y_spec]
                + [_mod_spec(d, chunk, mod_rows, tm) for chunk in (2, 3, 4, 5)]
                + [vec_spec, _resident(wo.shape), _resident(wg.shape), _resident(wu.shape), _resident(wd.shape)])
    args = [x, *ys, mods, mods, mods, mods, norm_g.reshape(1, d), wo, wg, wu, wd]
    if final:
        in_specs.append(vec_spec)
        args.append(final_g.reshape(1, d))
    return pl.pallas_call(
        functools.partial(_ffn_kernel, final),
        grid=(rows // tm,),
        in_specs=in_specs,
        out_specs=row_spec,
        out_shape=jax.ShapeDtypeStruct((rows, d), F32),
        scratch_shapes=[pltpu.VMEM((tm, d), BF16), pltpu.VMEM((tm, FFN_HIDDEN), BF16)],
        compiler_params=_compiler_params(),
        name="out_projection_ffn",
    )(*args)


def _rope_tables(n_tok, dim):
    t = np.arange(n_tok)
    row = (t // GRID_W).astype(np.float32)
    col = (t % GRID_W).astype(np.float32)
    nf = dim // 4
    inv = (np.float32(ROPE_BASE) ** (-np.arange(nf, dtype=np.float32) / np.float32(nf))).astype(np.float32)
    ang = np.concatenate([row[:, None] * inv, col[:, None] * inv], axis=-1).astype(np.float64)
    reps = 128 // (dim // 2)
    return (jnp.asarray(np.tile(np.cos(ang), (1, reps)), F32), jnp.asarray(np.tile(np.sin(ang), (1, reps)), F32))


def kernel(x, c, ctx, c_ctx, norm1_g, norm2_g, w_ada, b_ada, w_in, w_out, attn_sink, conv_w, conv_b,
           conv_ln_g, conv_ln_b, diff_lq1, diff_lk1, diff_lq2, diff_lk2, diff_subln_g, na_rpb,
           w_gate, w_up, w_down, final_g):
    batch, seq, d = x.shape
    ctx_len = ctx.shape[1]
    depth = w_ada.shape[0]

    cvec = jnp.zeros((ADA_ROWS, d), F32).at[:batch].set(c).at[batch].set(c_ctx)
    mods_all = _ada_table(cvec, w_ada, b_ada).reshape(depth, ADA_ROWS * 6, 1, d)
    rope = _rope_tables(seq, HEAD_DIM) + _rope_tables(seq, C_QK_DIM)
    lat_rows = (0, seq)
    ctx_rows = (batch, batch * ctx_len)

    xl = x.reshape(batch * seq, d)
    xc = ctx.reshape(batch * ctx_len, d)
    for l in range(depth):
        ctx_needed = l < depth - 1
        mods = mods_all[l]
        lambda_init = 0.8 - 0.6 * math.exp(-0.3 * l)
        w_in_l = _relayout_w_in(w_in[l])
        wo, wg, wu, wd = (w_out[l].astype(BF16), w_gate[l].astype(BF16),
                          w_up[l].astype(BF16), w_down[l].astype(BF16))
        lam_vecs = jnp.stack([diff_lq1[l], diff_lk1[l], diff_lq2[l], diff_lk2[l]]).astype(F32)
        subg = jnp.tile(diff_subln_g[l], N_HEADS).reshape(1, GROUP_W)
        cw = conv_w[l].reshape(CONV_K, GROUP_W)

        def per_batch(tensors, n_tok):
            return [t if i in (4, 6) else t.reshape(batch, n_tok, -1) for i, t in enumerate(tensors)]

        qa, ka, va, ub, qct, kc, vct, qd, kd, vd = per_batch(
            _in_projection(xl, norm1_g[l], mods, lat_rows, w_in_l, rope), seq)
        qa_c, ka_c, va_c, ub_c, qct_c, kc_c, vct_c, qd_c, kd_c, vd_c = per_batch(
            _in_projection(xc, norm1_g[l], mods, ctx_rows, w_in_l, None), ctx_len)

        y_a = _window_attention(attn_sink[l], qa, ka, va, ka_c, va_c)
        y_b = _conformer_conv(ub, cw, conv_b[l], conv_ln_g[l], conv_ln_b[l])
        y_c = _diff_attention(lam_vecs, subg, lambda_init, batch, qct, kc_c, vct_c, kc, vct)
        y_d = _neighbourhood_attention(_na_bias_table(na_rpb[l]), qd, kd, vd, kd_c, vd_c)
        ys = [t.reshape(batch * seq, GROUP_W) for t in (y_a, y_b, y_c, y_d)]
        xl = _out_projection_ffn(xl, ys, mods, lat_rows, norm2_g[l], wo, wg, wu, wd,
                                 final_g=None if ctx_needed else final_g)
        if ctx_needed:
            yc_a = _context_attention(qa_c, ka_c, va_c, sink=attn_sink[l], split_layout=True)
            yc_b = _conformer_conv(ub_c, cw, conv_b[l], conv_ln_g[l], conv_ln_b[l])
            yc_c = _diff_attention(lam_vecs, subg, lambda_init, batch, qct_c, kc_c, vct_c)
            yc_d = _context_attention(qd_c, kd_c, vd_c)
            ycs = [t.reshape(batch * ctx_len, GROUP_W) for t in (yc_a, yc_b, yc_c, yc_d)]
            xc = _out_projection_ffn(xc, ycs, mods, ctx_rows, norm2_g[l], wo, wg, wu, wd)
    return xl.reshape(batch, seq, d)
```

```python
import functools
import math

import numpy as np
import jax
import jax.numpy as jnp
from jax import lax
from jax.experimental import pallas as pl
from jax.experimental.pallas import tpu as pltpu

F32 = jnp.float32
BF16 = jnp.bfloat16

D_MODEL = 1024
DEPTH = 2
GRID_W = 64
HEAD_DIM = 64
GROUP_W = 256
N_HEADS = 4
A_KV_HEADS = 2
A_WINDOW = 128
BLK = 128
CONV_K = 31
C_QK_DIM = 32
NA_KH = 8
NA_KW = 16
FFN_HIDDEN = 2816
ROPE_BASE = 10000.0
EPS = 1e-6
NEG_INF = -1e30

VMEM_LIMIT = 56 * 1024 * 1024
ADA_ROWS = 16
ADA_TN = 1536
ROW_TILE = 512
FFN_CHUNK = 256
CONV_TC = 64
CONV_HALO = 16
DIFF_TQ = 512
LOG2E = math.log2(math.e)
ATTN_Q_SCALE = (HEAD_DIM ** -0.5) * LOG2E
DIFF_Q_SCALE = (C_QK_DIM ** -0.5) * LOG2E
DIFF_BLOCKS_PER_STEP = 2
DIFF_KEY_SEG = 1024
WIN_BLOCKS_PER_STEP = 4
NA_ROWS_PER_STEP = 8


def _compiler_params():
    return pltpu.CompilerParams(vmem_limit_bytes=VMEM_LIMIT)


def _resident(shape):
    return pl.BlockSpec(shape, lambda *_: (0,) * len(shape), pipeline_mode=pl.Buffered(1))


def _sigmoid(v):
    return 1.0 / (1.0 + jnp.exp(-v))


def _dot(a, b):
    return jnp.dot(a, b, preferred_element_type=F32)


def _dot_nt(a, b):
    return lax.dot_general(a, b, (((1,), (1,)), ((), ())), preferred_element_type=F32)


def _ada_kernel(c_ref, w_ref, b_ref, o_ref):
    cv = c_ref[...]
    s = cv * _sigmoid(cv)
    o_ref[0] = _dot(s.astype(BF16), w_ref[0].astype(BF16)) + b_ref[0]


def _ada_table(cvec, w_ada, b_ada):
    depth, d, n = w_ada.shape
    return pl.pallas_call(
        _ada_kernel,
        grid=(depth, n // ADA_TN),
        in_specs=[
            pl.BlockSpec((ADA_ROWS, d), lambda l, j: (0, 0)),
            pl.BlockSpec((1, d, ADA_TN), lambda l, j: (l, 0, j)),
            pl.BlockSpec((1, 1, ADA_TN), lambda l, j: (l, 0, j)),
        ],
        out_specs=pl.BlockSpec((1, ADA_ROWS, ADA_TN), lambda l, j: (l, 0, j)),
        out_shape=jax.ShapeDtypeStruct((depth, ADA_ROWS, n), F32),
        compiler_params=_compiler_params(),
        name="ada_table",
    )(cvec, w_ada, b_ada.reshape(depth, 1, n))


def _selection_matrices():
    lane = np.arange(GROUP_W)
    part, h, j = lane // 128, (lane % 128) // 32, lane % 32
    grp, jc = (lane % 128) // 16, lane % 16
    sources = (
        (GROUP_W, h * HEAD_DIM + part * 32 + j),
        (A_KV_HEADS * HEAD_DIM, (h // 2) * HEAD_DIM + part * 32 + j),
        (A_KV_HEADS * HEAD_DIM, ((lane // HEAD_DIM) // 2) * HEAD_DIM + lane % HEAD_DIM),
        (GROUP_W, grp * C_QK_DIM + part * 16 + jc),
    )
    return [np.equal(np.arange(n_src)[:, None], src[None, :]).astype(np.float32) for n_src, src in sources]


def _relayout_kernel(w_ref, s_qa, s_ka, s_va, s_c, o_ref):
    def cols(a, b):
        return w_ref[0, :, a:b].astype(BF16)

    def select(a, b, s_ref):
        return _dot(cols(a, b), s_ref[...]).astype(BF16)

    o_ref[:, OFF_QA:OFF_KA] = select(0, 256, s_qa)
    o_ref[:, OFF_KA:OFF_VA] = select(256, 384, s_ka)
    o_ref[:, OFF_VA:OFF_UB] = select(384, 512, s_va)
    o_ref[:, OFF_UB:OFF_QC] = cols(512, 1024)
    o_ref[:, OFF_QC:OFF_KC] = select(1024, 1280, s_c)
    o_ref[:, OFF_KC:OFF_VC] = select(1280, 1536, s_c)
    o_ref[:, OFF_VC:PROJ_WIDTH] = cols(1536, 2560)


def _relayout_w_in(w_in, layer):
    _, d, n_in = w_in.shape
    rows = 256
    sel = [jnp.asarray(s, BF16) for s in _selection_matrices()]
    return pl.pallas_call(
        _relayout_kernel,
        grid=(d // rows,),
        in_specs=[pl.BlockSpec((1, rows, n_in), lambda i: (layer, i, 0))]
                 + [pl.BlockSpec(s.shape, lambda i: (0, 0)) for s in sel],
        out_specs=pl.BlockSpec((rows, PROJ_WIDTH), lambda i: (i, 0)),
        out_shape=jax.ShapeDtypeStruct((d, PROJ_WIDTH), BF16),
        compiler_params=_compiler_params(),
        name="relayout_w_in",
    )(w_in, *sel)


def _cast_kernel(x_ref, o_ref):
    o_ref[...] = x_ref[0].astype(o_ref.dtype)


def _layer_weight_bf16(w, layer):
    _, r, c = w.shape
    rows = 256
    return pl.pallas_call(
        _cast_kernel,
        grid=(r // rows,),
        in_specs=[pl.BlockSpec((1, rows, c), lambda i: (layer, i, 0))],
        out_specs=pl.BlockSpec((rows, c), lambda i: (i, 0)),
        out_shape=jax.ShapeDtypeStruct((r, c), BF16),
        compiler_params=_compiler_params(),
        name="weight_bf16",
    )(w)


PROJ_WIDTH = 9 * GROUP_W + 512
OFF_QA, OFF_KA, OFF_VA, OFF_UB, OFF_QC, OFF_KC, OFF_VC, OFF_QD, OFF_KD, OFF_VD = (
    0, 256, 512, 768, 1280, 1536, 1792, 2048, 2304, 2560)


def _inproj_kernel(rope, x_ref, g_ref, sh_ref, sc_ref, w_ref, *rest):
    if rope:
        ca_ref, sa_ref, cc_ref, sc2_ref = rest[:4]
        rest = rest[4:]
    qa_o, ka_o, va_o, ub_o, qc_o, kc_o, vc_o, qd_o, kd_o, vd_o = rest
    x = x_ref[...]
    r = lax.rsqrt(jnp.mean(x * x, axis=-1, keepdims=True) + EPS)
    h = (x * r) * g_ref[...]
    h = h * (1.0 + sc_ref[0]) + sh_ref[0]
    hb = h.astype(BF16)

    def proj(off, width=GROUP_W):
        return _dot(hb, w_ref[:, off:off + width])

    def store_rot(o_ref, y, c_ref, s_ref, scale, channel_major=False):
        if rope:
            x1, x2 = y[:, :128], y[:, 128:]
            cs, sn = c_ref[...], s_ref[...]
            y1, y2 = x1 * cs - x2 * sn, x1 * sn + x2 * cs
        else:
            y1, y2 = y[:, :128], y[:, 128:]
        if channel_major:
            o_ref[:128, :] = (y1 * scale).T.astype(BF16)
            o_ref[128:, :] = (y2 * scale).T.astype(BF16)
        else:
            o_ref[:, :128] = (y1 * scale).astype(BF16)
            o_ref[:, 128:] = (y2 * scale).astype(BF16)

    ca = sa = cc = sc2 = None
    if rope:
        ca, sa, cc, sc2 = ca_ref, sa_ref, cc_ref, sc2_ref
    store_rot(qa_o, proj(OFF_QA), ca, sa, ATTN_Q_SCALE)
    store_rot(ka_o, proj(OFF_KA), ca, sa, 1.0)
    va_o[...] = proj(OFF_VA).astype(BF16)
    ub_o[...] = proj(OFF_UB, 512)
    store_rot(qc_o, proj(OFF_QC), cc, sc2, DIFF_Q_SCALE, channel_major=True)
    store_rot(kc_o, proj(OFF_KC), cc, sc2, 1.0)
    vc_o[...] = proj(OFF_VC).T.astype(BF16)
    qd_o[...] = (proj(OFF_QD) * ATTN_Q_SCALE).astype(BF16)
    kd_o[...] = proj(OFF_KD).astype(BF16)
    vd_o[...] = proj(OFF_VD).astype(BF16)


def _mod_spec(d, chunk, mod_rows, tm):
    first_row, rows_per_mod = mod_rows
    tiles_per_mod = rows_per_mod // tm
    return pl.BlockSpec((1, 1, d), lambda i: ((first_row + i // tiles_per_mod) * 6 + chunk, 0, 0))


def _in_projection(x, g, mods, mod_rows, w, rope_tables):
    rows, d = x.shape
    tm = ROW_TILE
    rope = rope_tables is not None
    in_specs = [
        pl.BlockSpec((tm, d), lambda i: (i, 0)),
        pl.BlockSpec((1, d), lambda i: (0, 0)),
        _mod_spec(d, 0, mod_rows, tm), _mod_spec(d, 1, mod_rows, tm),
        _resident((d, PROJ_WIDTH)),
    ]
    args = [x, g.reshape(1, d), mods, mods, w]
    if rope:
        tiles_per_seq = rope_tables[0].shape[0] // tm
        tab_spec = pl.BlockSpec((tm, 128), lambda i: (i % tiles_per_seq, 0))
        in_specs += [tab_spec] * 4
        args += list(rope_tables)
    narrow = pl.BlockSpec((tm, GROUP_W), lambda i: (i, 0))
    wide = pl.BlockSpec((tm, 512), lambda i: (i, 0))
    chan_major = pl.BlockSpec((GROUP_W, tm), lambda i: (0, i))
    out_specs = [narrow, narrow, narrow, wide, chan_major, narrow, chan_major, narrow, narrow, narrow]
    bf = jax.ShapeDtypeStruct((rows, GROUP_W), BF16)
    bf_t = jax.ShapeDtypeStruct((GROUP_W, rows), BF16)
    out_shape = [bf, bf, bf, jax.ShapeDtypeStruct((rows, 512), F32), bf_t, bf, bf_t, bf, bf, bf]
    return pl.pallas_call(
        functools.partial(_inproj_kernel, rope),
        grid=(rows // tm,),
        in_specs=in_specs,
        out_specs=out_specs,
        out_shape=out_shape,
        compiler_params=_compiler_params(),
        name="in_projection",
    )(*args)


def _lane_iota():
    return lax.broadcasted_iota(jnp.int32, (1, GROUP_W), 1)


def _split_half_head(lane):
    return jnp.right_shift(jnp.bitwise_and(lane, 127), 5)


def _natural_head(lane):
    return jnp.right_shift(lane, 6)


def _stack_heads(q, lane_group, groups):
    zero = jnp.zeros_like(q)
    return jnp.concatenate([jnp.where(lane_group == g, q, zero) for g in groups], axis=0)


def _softmax_pv(scores, values, sink_col=None):
    m = functools.reduce(jnp.maximum, [jnp.max(s, axis=-1, keepdims=True) for s in scores])
    if sink_col is not None:
        m = jnp.maximum(m, sink_col)
    es = [jnp.exp2(s - m) for s in scores]
    l = functools.reduce(jnp.add, [jnp.sum(e, axis=-1, keepdims=True) for e in es])
    if sink_col is not None:
        l = l + jnp.exp2(sink_col - m)
    o = functools.reduce(jnp.add, [_dot(e.astype(BF16), v) for e, v in zip(es, values)])
    return o / l


def _select_heads(o, tq):
    head = _natural_head(_lane_iota())
    out = o[0:tq]
    for h in range(1, N_HEADS):
        out = jnp.where(head == h, o[h * tq:(h + 1) * tq], out)
    return out


def _sink_column(sink_ref, tq):
    row = lax.broadcasted_iota(jnp.int32, (N_HEADS * tq, 1), 0)
    col = jnp.full((N_HEADS * tq, 1), sink_ref[N_HEADS - 1], F32)
    for h in range(N_HEADS - 2, -1, -1):
        col = jnp.where(row < (h + 1) * tq, sink_ref[h], col)
    return col * LOG2E


def _win_attn_kernel(seq, sink_ref, q_ref, kl_ref, vl_ref, kc_ref, vc_ref, o_ref):
    lane_head = _split_half_head(_lane_iota())
    sink_col = _sink_column(sink_ref, BLK)
    shape = (BLK, 3 * BLK)
    rel = lax.broadcasted_iota(jnp.int32, shape, 1) - lax.broadcasted_iota(jnp.int32, shape, 0)
    for t in range(WIN_BLOCKS_PER_STEP):
        n = pl.program_id(1) * WIN_BLOCKS_PER_STEP + t
        start = pl.multiple_of(jnp.clip((n - 1) * BLK, 0, seq - 3 * BLK), BLK)
        qs = _stack_heads(q_ref[0, t * BLK:(t + 1) * BLK, :], lane_head, range(N_HEADS))
        kl = kl_ref[0, pl.ds(start, 3 * BLK), :]
        vl = vl_ref[0, pl.ds(start, 3 * BLK), :]
        s_loc = _dot_nt(qs, kl)
        s_ctx = _dot_nt(qs, kc_ref[0])
        dist = rel + (start - n * BLK)
        band = jnp.where(jnp.abs(dist) <= A_WINDOW, 0.0, NEG_INF)
        s_loc = (s_loc.reshape(N_HEADS, BLK, 3 * BLK) + band[None]).reshape(N_HEADS * BLK, 3 * BLK)
        o = _softmax_pv([s_loc, s_ctx], [vl, vc_ref[0]], sink_col)
        o_ref[0, t * BLK:(t + 1) * BLK, :] = _select_heads(o, BLK).astype(BF16)


def _window_attention(sink, q, k, v, kc, vc):
    b, seq, w = q.shape
    ctx_len = kc.shape[1]
    tq = WIN_BLOCKS_PER_STEP * BLK
    whole = pl.BlockSpec((1, seq, w), lambda i, n: (i, 0, 0))
    ctx_spec = pl.BlockSpec((1, ctx_len, w), lambda i, n: (i, 0, 0))
    blk = pl.BlockSpec((1, tq, w), lambda i, n: (i, n, 0))
    return pl.pallas_call(
        functools.partial(_win_attn_kernel, seq),
        grid=(b, seq // tq),
        in_specs=[pl.BlockSpec(memory_space=pltpu.SMEM), blk, whole, whole, ctx_spec, ctx_spec],
        out_specs=blk,
        out_shape=jax.ShapeDtypeStruct((b, seq, w), BF16),
        compiler_params=_compiler_params(),
        name="window_attention",
    )(sink, q, k, v, kc, vc)


def _na_bias_table(rpb):
    cq = np.arange(GRID_W)
    cs = np.clip(cq - NA_KW // 2, 0, GRID_W - NA_KW)
    col_valid = (cq[None, :] >= cs[:, None]) & (cq[None, :] < cs[:, None] + NA_KW)
    dc = np.clip(cq[None, :] - cq[:, None], -(NA_KW - 1), NA_KW - 1) + (NA_KW - 1)
    n_dr, n_dc = 2 * NA_KH - 1, 2 * NA_KW - 1
    onehot = (dc.reshape(1, -1) == np.arange(n_dc)[:, None]).astype(np.float32)
    t = jnp.dot(rpb.astype(F32).reshape(N_HEADS * n_dr, n_dc), onehot, precision=lax.Precision.HIGHEST)
    t = jnp.where(col_valid[None, None], t.reshape(N_HEADS, n_dr, GRID_W, GRID_W) * LOG2E, NEG_INF)
    t = jnp.transpose(t, (1, 0, 2, 3)).reshape(n_dr, N_HEADS * GRID_W, GRID_W)
    return jnp.concatenate([t[:-1], t[1:]], axis=-1)


def _na_row_start(r, rows):
    return jnp.clip(r - NA_KH // 2, 0, rows - NA_KH)


def _na_attn_kernel(rows, q_ref, kl_ref, vl_ref, kc_ref, vc_ref, bias_ref, o_ref):
    lane_head = _natural_head(_lane_iota())
    for t in range(NA_ROWS_PER_STEP):
        r = pl.program_id(1) * NA_ROWS_PER_STEP + t
        first = _na_row_start(r, rows)
        start = pl.multiple_of(first * GRID_W, GRID_W)
        qs = _stack_heads(q_ref[0, t * GRID_W:(t + 1) * GRID_W, :], lane_head, range(N_HEADS))
        kl = kl_ref[0, pl.ds(start, NA_KH * GRID_W), :]
        vl = vl_ref[0, pl.ds(start, NA_KH * GRID_W), :]
        off = first - r + NA_KH - 1
        bias = jnp.concatenate([bias_ref[off + 2 * i] for i in range(NA_KH // 2)], axis=1)
        s_loc = _dot_nt(qs, kl) + bias
        s_ctx = _dot_nt(qs, kc_ref[0])
        o = _softmax_pv([s_loc, s_ctx], [vl, vc_ref[0]])
        o_ref[0, t * GRID_W:(t + 1) * GRID_W, :] = _select_heads(o, GRID_W).astype(BF16)


def _neighbourhood_attention(bias, q, k, v, kc, vc):
    b, seq, w = q.shape
    rows = seq // GRID_W
    ctx_len = kc.shape[1]
    tq = NA_ROWS_PER_STEP * GRID_W
    whole = pl.BlockSpec((1, seq, w), lambda i, r: (i, 0, 0))
    ctx_spec = pl.BlockSpec((1, ctx_len, w), lambda i, r: (i, 0, 0))
    blk = pl.BlockSpec((1, tq, w), lambda i, r: (i, r, 0))
    return pl.pallas_call(
        functools.partial(_na_attn_kernel, rows),
        grid=(b, seq // tq),
        in_specs=[blk, whole, whole, ctx_spec, ctx_spec, _resident(bias.shape)],
        out_specs=blk,
        out_shape=jax.ShapeDtypeStruct((b, seq, w), BF16),
        compiler_params=_compiler_params(),
        name="neighbourhood_attention",
    )(q, k, v, kc, vc, bias)


def _ctx_attn_kernel(split_layout, has_sink, *refs):
    if has_sink:
        sink_ref, q_ref, k_ref, v_ref, o_ref = refs
    else:
        q_ref, k_ref, v_ref, o_ref = refs
    tq = q_ref.shape[1]
    lane = _lane_iota()
    group = _split_half_head(lane) if split_layout else _natural_head(lane)
    qs = _stack_heads(q_ref[0], group, range(N_HEADS))
    s = _dot_nt(qs, k_ref[0])
    sink_col = _sink_column(sink_ref, tq) if has_sink else None
    o = _softmax_pv([s], [v_ref[0]], sink_col)
    o_ref[0] = _select_heads(o, tq).astype(BF16)


def _context_attention(q, k, v, sink=None, split_layout=False):
    b, n, w = q.shape
    spec = pl.BlockSpec((1, n, w), lambda i: (i, 0, 0))
    in_specs = [spec, spec, spec]
    args = [q, k, v]
    if sink is not None:
        in_specs = [pl.BlockSpec(memory_space=pltpu.SMEM)] + in_specs
        args = [sink] + args
    return pl.pallas_call(
        functools.partial(_ctx_attn_kernel, split_layout, sink is not None),
        grid=(b,),
        in_specs=in_specs,
        out_specs=spec,
        out_shape=jax.ShapeDtypeStruct((b, n, w), BF16),
        compiler_params=_compiler_params(),
        name="context_attention",
    )(*args)


def _diff_attn_kernel(has_local, lambda_init, lam_ref, subg_ref, qt_ref, *refs):
    if has_local:
        kl_ref, vtl_ref, kc_ref, vtc_ref, o_ref = refs
    else:
        kc_ref, vtc_ref, o_ref = refs
    tq = min(DIFF_TQ, qt_ref.shape[1])
    n_blocks = qt_ref.shape[1] // tq
    lv = lam_ref[...]
    lam = (jnp.exp(jnp.sum(lv[0:1] * lv[1:2], axis=-1, keepdims=True))
           - jnp.exp(jnp.sum(lv[2:3] * lv[3:4], axis=-1, keepdims=True)) + lambda_init)
    chan = lax.broadcasted_iota(jnp.int32, (GROUP_W, 1), 0)
    group = jnp.right_shift(jnp.bitwise_and(chan, 127), 4)
    zero = jnp.zeros((GROUP_W, tq), BF16)
    segs = [(kc_ref, vtc_ref, 0, kc_ref.shape[1])]
    if has_local:
        n_loc = kl_ref.shape[1]
        segs = [(kl_ref, vtl_ref, k0, DIFF_KEY_SEG) for k0 in range(0, n_loc, DIFF_KEY_SEG)] + segs
    sum_rows = 16

    def head_scores(unit):
        blk, h = unit
        qt = qt_ref[:, blk * tq:(blk + 1) * tq]
        qs = jnp.concatenate([jnp.where(group == 2 * h, qt, zero),
                              jnp.where(group == 2 * h + 1, qt, zero)], axis=1)
        return [_dot(k_ref[0, k0:k0 + nk, :], qs) for k_ref, _, k0, nk in segs]

    def column_max(s, slab=64):
        part = jnp.max(s.reshape(s.shape[0] // slab, slab, s.shape[1]), axis=0)
        return jnp.max(part, axis=0, keepdims=True)

    ys = []
    units = [(blk, h) for blk in range(n_blocks) for h in range(N_HEADS)]
    nxt = head_scores(units[0])
    for ui, (blk, h) in enumerate(units):
        scores = nxt
        if ui + 1 < len(units):
            nxt = head_scores(units[ui + 1])
        m = functools.reduce(jnp.maximum, [column_max(s) for s in scores])
        pv = None
        for s, (_, vt_ref, k0, nk) in zip(scores, segs):
            e = jnp.exp2(s - m).astype(BF16)
            vt = vt_ref[h * HEAD_DIM:(h + 1) * HEAD_DIM, k0:k0 + nk]
            lhs = jnp.concatenate([vt, jnp.ones((sum_rows, nk), BF16)], axis=0)
            p = _dot(lhs, e)
            pv = p if pv is None else pv + p
        inv = 1.0 / pv[HEAD_DIM:HEAD_DIM + 1]
        pv = pv[:HEAD_DIM]
        o = pv[:, :tq] * inv[:, :tq] - pv[:, tq:] * (lam * inv[:, tq:])
        ms = jnp.mean(o * o, axis=0, keepdims=True)
        ys.append(o * lax.rsqrt(ms + EPS))
        if h == N_HEADS - 1:
            out = jnp.concatenate(ys, axis=0).T
            o_ref[0, blk * tq:(blk + 1) * tq, :] = (out * subg_ref[...] * (1.0 - lambda_init)).astype(BF16)
            ys = []


def _diff_attention(lam_vecs, subg, lambda_init, b, qt, kc, vtc, k=None, vt=None):
    w = qt.shape[0]
    n = qt.shape[1] // b
    ctx_len = kc.shape[1]
    has_local = k is not None
    tq = min(DIFF_TQ * DIFF_BLOCKS_PER_STEP, n)
    steps = n // tq
    q_spec = pl.BlockSpec((w, tq), lambda i, j: (0, i * steps + j))
    in_specs = [pl.BlockSpec(lam_vecs.shape, lambda i, j: (0, 0)),
                pl.BlockSpec((1, w), lambda i, j: (0, 0)), q_spec]
    args = [lam_vecs, subg, qt]
    if has_local:
        in_specs += [pl.BlockSpec((1, n, w), lambda i, j: (i, 0, 0)),
                     pl.BlockSpec((w, n), lambda i, j: (0, i))]
        args += [k, vt]
    in_specs += [pl.BlockSpec((1, ctx_len, w), lambda i, j: (i, 0, 0)),
                 pl.BlockSpec((w, ctx_len), lambda i, j: (0, i))]
    args += [kc, vtc]
    return pl.pallas_call(
        functools.partial(_diff_attn_kernel, has_local, lambda_init),
        grid=(b, n // tq),
        in_specs=in_specs,
        out_specs=pl.BlockSpec((1, tq, w), lambda i, j: (i, j, 0)),
        out_shape=jax.ShapeDtypeStruct((b, n, w), BF16),
        compiler_params=_compiler_params(),
        name="diff_attention",
    )(*args)


def _conv_kernel(seq, u_ref, w_ref, b_ref, g_ref, beta_ref, o_ref, pad_ref):
    halo, tc, ch = CONV_HALO, CONV_TC, GROUP_W
    zeros = jnp.zeros((halo, ch), F32)
    pad_ref[0:halo, :] = zeros
    pad_ref[halo + seq:2 * halo + seq, :] = zeros

    def glu(i, carry):
        r0 = pl.multiple_of(i * tc, tc)
        u = u_ref[0, pl.ds(r0, tc), :]
        pad_ref[pl.ds(halo + r0, tc), :] = u[:, :ch] * _sigmoid(u[:, ch:])
        return carry

    lax.fori_loop(0, seq // tc, glu, 0)

    win_rows = tc + 2 * halo

    def chunk(i, carry):
        c0 = pl.multiple_of(i * tc, tc)
        win = pad_ref[pl.ds(c0, win_rows), :]
        acc = jnp.zeros((tc, ch), F32)
        for sub in range(8):
            shifted = win if sub == 0 else pltpu.roll(win, win_rows - sub, axis=0)
            for blk8 in range(win_rows // 8):
                tap = 8 * blk8 + sub - (halo - CONV_K // 2)
                if 0 <= tap < CONV_K:
                    acc = acc + shifted[8 * blk8:8 * blk8 + tc] * w_ref[tap:tap + 1, :]
        hcv = acc + b_ref[...]
        mu = jnp.mean(hcv, axis=-1, keepdims=True)
        cen = hcv - mu
        var = jnp.mean(cen * cen, axis=-1, keepdims=True)
        y = cen * lax.rsqrt(var + EPS) * g_ref[...] + beta_ref[...]
        o_ref[0, pl.ds(c0, tc), :] = (y * _sigmoid(y)).astype(BF16)
        return carry

    lax.fori_loop(0, seq // tc, chunk, 0, unroll=4)


def _conformer_conv(u, w, bias, ln_g, ln_b):
    b, seq, two_ch = u.shape
    ch = two_ch // 2
    vec = pl.BlockSpec((1, ch), lambda i: (0, 0))
    return pl.pallas_call(
        functools.partial(_conv_kernel, seq),
        grid=(b,),
        in_specs=[pl.BlockSpec((1, seq, two_ch), lambda i: (i, 0, 0)),
                  pl.BlockSpec((CONV_K, ch), lambda i: (0, 0)), vec, vec, vec],
        out_specs=pl.BlockSpec((1, seq, ch), lambda i: (i, 0, 0)),
        out_shape=jax.ShapeDtypeStruct((b, seq, ch), BF16),
        scratch_shapes=[pltpu.VMEM((seq + 2 * CONV_HALO, ch), F32)],
        compiler_params=_compiler_params(),
        name="conformer_conv",
    )(u, w, bias.reshape(1, ch), ln_g.reshape(1, ch), ln_b.reshape(1, ch))


def _ffn_kernel(final, x_ref, ya_ref, yb_ref, yc_ref, yd_ref, g1_ref, sh_ref, sc_ref, g2_ref, ng_ref,
                wo_ref, wg_ref, wu_ref, wd_ref, *rest):
    if final:
        fg_ref, o_ref, ycat_ref, act_ref = rest
    else:
        o_ref, ycat_ref, act_ref = rest
    for j, y_ref in enumerate((ya_ref, yb_ref, yc_ref, yd_ref)):
        ycat_ref[:, j * GROUP_W:(j + 1) * GROUP_W] = y_ref[...]
    x = x_ref[...] + g1_ref[0] * _dot(ycat_ref[...], wo_ref[...])
    r = lax.rsqrt(jnp.mean(x * x, axis=-1, keepdims=True) + EPS)
    h = (x * r) * ng_ref[...]
    hb = (h * (1.0 + sc_ref[0]) + sh_ref[0]).astype(BF16)
    for c in range(0, FFN_HIDDEN, FFN_CHUNK):
        gate = _dot(hb, wg_ref[:, c:c + FFN_CHUNK])
        up = _dot(hb, wu_ref[:, c:c + FFN_CHUNK])
        act_ref[:, c:c + FFN_CHUNK] = (gate * _sigmoid(gate) * up).astype(BF16)
    x = x + g2_ref[0] * _dot(act_ref[...], wd_ref[...])
    if final:
        r = lax.rsqrt(jnp.mean(x * x, axis=-1, keepdims=True) + EPS)
        x = (x * r) * fg_ref[...]
    o_ref[...] = x


def _out_projection_ffn(x, ys, mods, mod_rows, norm_g, wo, wg, wu, wd, final_g=None):
    rows, d = x.shape
    tm = ROW_TILE
    final = final_g is not None
    row_spec = pl.BlockSpec((tm, d), lambda i: (i, 0))
    y_spec = pl.BlockSpec((tm, GROUP_W), lambda i: (i, 0))
    vec_spec = pl.BlockSpec((1, d), lambda i: (0, 0))
    in_specs = ([row_spec, y_spec, y_spec, y_spec, y_spec]
                + [_mod_spec(d, chunk, mod_rows, tm) for chunk in (2, 3, 4, 5)]
                + [vec_spec, _resident(wo.shape), _resident(wg.shape), _resident(wu.shape), _resident(wd.shape)])
    args = [x, *ys, mods, mods, mods, mods, norm_g.reshape(1, d), wo, wg, wu, wd]
    if final:
        in_specs.append(vec_spec)
        args.append(final_g.reshape(1, d))
    return pl.pallas_call(
        functools.partial(_ffn_kernel, final),
        grid=(rows // tm,),
        in_specs=in_specs,
        out_specs=row_spec,
        out_shape=jax.ShapeDtypeStruct((rows, d), F32),
        scratch_shapes=[pltpu.VMEM((tm, d), BF16), pltpu.VMEM((tm, FFN_HIDDEN), BF16)],
        compiler_params=_compiler_params(),
        name="out_projection_ffn",
    )(*args)


def _rope_tables(n_tok, dim):
    t = np.arange(n_tok)
    row = (t // GRID_W).astype(np.float32)
    col = (t % GRID_W).astype(np.float32)
    nf = dim // 4
    inv = (np.float32(ROPE_BASE) ** (-np.arange(nf, dtype=np.float32) / np.float32(nf))).astype(np.float32)
    ang = np.concatenate([row[:, None] * inv, col[:, None] * inv], axis=-1).astype(np.float64)
    reps = 128 // (dim // 2)
    return (jnp.asarray(np.tile(np.cos(ang), (1, reps)), F32), jnp.asarray(np.tile(np.sin(ang), (1, reps)), F32))


def kernel(x, c, ctx, c_ctx, norm1_g, norm2_g, w_ada, b_ada, w_in, w_out, attn_sink, conv_w, conv_b,
           conv_ln_g, conv_ln_b, diff_lq1, diff_lk1, diff_lq2, diff_lk2, diff_subln_g, na_rpb,
           w_gate, w_up, w_down, final_g):
    batch, seq, d = x.shape
    ctx_len = ctx.shape[1]
    depth = w_ada.shape[0]

    cvec = jnp.zeros((ADA_ROWS, d), F32).at[:batch].set(c).at[batch].set(c_ctx)
    mods_all = _ada_table(cvec, w_ada, b_ada).reshape(depth, ADA_ROWS * 6, 1, d)
    rope = _rope_tables(seq, HEAD_DIM) + _rope_tables(seq, C_QK_DIM)
    lat_rows = (0, seq)
    ctx_rows = (batch, batch * ctx_len)

    xl = x.reshape(batch * seq, d)
    xc = ctx.reshape(batch * ctx_len, d)
    for l in range(depth):
        ctx_needed = l < depth - 1
        mods = mods_all[l]
        lambda_init = 0.8 - 0.6 * math.exp(-0.3 * l)
        w_in_l = _relayout_w_in(w_in, l)
        wo, wg, wu, wd = (_layer_weight_bf16(w, l) for w in (w_out, w_gate, w_up, w_down))
        lam_vecs = jnp.stack([diff_lq1[l], diff_lk1[l], diff_lq2[l], diff_lk2[l]]).astype(F32)
        subg = jnp.tile(diff_subln_g[l], N_HEADS).reshape(1, GROUP_W)
        cw = conv_w[l].reshape(CONV_K, GROUP_W)

        def per_batch(tensors, n_tok):
            return [t if i in (4, 6) else t.reshape(batch, n_tok, -1) for i, t in enumerate(tensors)]

        qa, ka, va, ub, qct, kc, vct, qd, kd, vd = per_batch(
            _in_projection(xl, norm1_g[l], mods, lat_rows, w_in_l, rope), seq)
        qa_c, ka_c, va_c, ub_c, qct_c, kc_c, vct_c, qd_c, kd_c, vd_c = per_batch(
            _in_projection(xc, norm1_g[l], mods, ctx_rows, w_in_l, None), ctx_len)

        y_a = _window_attention(attn_sink[l], qa, ka, va, ka_c, va_c)
        y_b = _conformer_conv(ub, cw, conv_b[l], conv_ln_g[l], conv_ln_b[l])
        y_c = _diff_attention(lam_vecs, subg, lambda_init, batch, qct, kc_c, vct_c, kc, vct)
        y_d = _neighbourhood_attention(_na_bias_table(na_rpb[l]), qd, kd, vd, kd_c, vd_c)
        ys = [t.reshape(batch * seq, GROUP_W) for t in (y_a, y_b, y_c, y_d)]
        xl = _out_projection_ffn(xl, ys, mods, lat_rows, norm2_g[l], wo, wg, wu, wd,
                                 final_g=None if ctx_needed else final_g)
        if ctx_needed:
            yc_a = _context_attention(qa_c, ka_c, va_c, sink=attn_sink[l], split_layout=True)
            yc_b = _conformer_conv(ub_c, cw, conv_b[l], conv_ln_g[l], conv_ln_b[l])
            yc_c = _diff_attention(lam_vecs, subg, lambda_init, batch, qct_c, kc_c, vct_c)
            yc_d = _context_attention(qd_c, kd_c, vd_c)
            ycs = [t.reshape(batch * ctx_len, GROUP_W) for t in (yc_a, yc_b, yc_c, yc_d)]
            xc = _out_projection_ffn(xc, ycs, mods, ctx_rows, norm2_g[l], wo, wg, wu, wd)
    return xl.reshape(batch, seq, d)
```

```python
import functools
import math

import numpy as np
import jax
import jax.numpy as jnp
from jax import lax
from jax.experimental import pallas as pl
from jax.experimental.pallas import tpu as pltpu

F32 = jnp.float32
BF16 = jnp.bfloat16

D_MODEL = 1024
DEPTH = 2
GRID_W = 64
HEAD_DIM = 64
GROUP_W = 256
N_HEADS = 4
A_KV_HEADS = 2
A_WINDOW = 128
BLK = 128
CONV_K = 31
C_QK_DIM = 32
NA_KH = 8
NA_KW = 16
FFN_HIDDEN = 2816
ROPE_BASE = 10000.0
EPS = 1e-6
NEG_INF = -1e30

VMEM_LIMIT = 56 * 1024 * 1024
ADA_ROWS = 16
ADA_TN = 1536
ROW_TILE = 512
FFN_CHUNK = 256
CONV_TC = 64
CONV_HALO = 16
DIFF_TQ = 512
LOG2E = math.log2(math.e)
ATTN_Q_SCALE = (HEAD_DIM ** -0.5) * LOG2E
DIFF_Q_SCALE = (C_QK_DIM ** -0.5) * LOG2E
DIFF_BLOCKS_PER_STEP = 2
DIFF_KEY_SEG = 1024


def _compiler_params():
    return pltpu.CompilerParams(vmem_limit_bytes=VMEM_LIMIT)


def _resident(shape):
    return pl.BlockSpec(shape, lambda *_: (0,) * len(shape), pipeline_mode=pl.Buffered(1))


def _sigmoid(v):
    return 1.0 / (1.0 + jnp.exp(-v))


def _dot(a, b):
    return jnp.dot(a, b, preferred_element_type=F32)


def _dot_nt(a, b):
    return lax.dot_general(a, b, (((1,), (1,)), ((), ())), preferred_element_type=F32)


def _ada_kernel(c_ref, w_ref, b_ref, o_ref):
    cv = c_ref[...]
    s = cv * _sigmoid(cv)
    o_ref[0] = _dot(s.astype(BF16), w_ref[0].astype(BF16)) + b_ref[0]


def _ada_table(cvec, w_ada, b_ada):
    depth, d, n = w_ada.shape
    return pl.pallas_call(
        _ada_kernel,
        grid=(depth, n // ADA_TN),
        in_specs=[
            pl.BlockSpec((ADA_ROWS, d), lambda l, j: (0, 0)),
            pl.BlockSpec((1, d, ADA_TN), lambda l, j: (l, 0, j)),
            pl.BlockSpec((1, 1, ADA_TN), lambda l, j: (l, 0, j)),
        ],
        out_specs=pl.BlockSpec((1, ADA_ROWS, ADA_TN), lambda l, j: (l, 0, j)),
        out_shape=jax.ShapeDtypeStruct((depth, ADA_ROWS, n), F32),
        compiler_params=_compiler_params(),
        name="ada_table",
    )(cvec, w_ada, b_ada.reshape(depth, 1, n))


def _selection_matrices():
    lane = np.arange(GROUP_W)
    part, h, j = lane // 128, (lane % 128) // 32, lane % 32
    grp, jc = (lane % 128) // 16, lane % 16
    sources = (
        (GROUP_W, h * HEAD_DIM + part * 32 + j),
        (A_KV_HEADS * HEAD_DIM, (h // 2) * HEAD_DIM + part * 32 + j),
        (A_KV_HEADS * HEAD_DIM, ((lane // HEAD_DIM) // 2) * HEAD_DIM + lane % HEAD_DIM),
        (GROUP_W, grp * C_QK_DIM + part * 16 + jc),
    )
    return [np.equal(np.arange(n_src)[:, None], src[None, :]).astype(np.float32) for n_src, src in sources]


def _relayout_kernel(w_ref, s_qa, s_ka, s_va, s_c, o_ref):
    def cols(a, b):
        return w_ref[0, :, a:b].astype(BF16)

    def select(a, b, s_ref):
        return _dot(cols(a, b), s_ref[...]).astype(BF16)

    o_ref[:, OFF_QA:OFF_KA] = select(0, 256, s_qa)
    o_ref[:, OFF_KA:OFF_VA] = select(256, 384, s_ka)
    o_ref[:, OFF_VA:OFF_UB] = select(384, 512, s_va)
    o_ref[:, OFF_UB:OFF_QC] = cols(512, 1024)
    o_ref[:, OFF_QC:OFF_KC] = select(1024, 1280, s_c)
    o_ref[:, OFF_KC:OFF_VC] = select(1280, 1536, s_c)
    o_ref[:, OFF_VC:PROJ_WIDTH] = cols(1536, 2560)


def _relayout_w_in(w_in, layer):
    _, d, n_in = w_in.shape
    rows = 256
    sel = [jnp.asarray(s, BF16) for s in _selection_matrices()]
    return pl.pallas_call(
        _relayout_kernel,
        grid=(d // rows,),
        in_specs=[pl.BlockSpec((1, rows, n_in), lambda i: (layer, i, 0))]
                 + [pl.BlockSpec(s.shape, lambda i: (0, 0)) for s in sel],
        out_specs=pl.BlockSpec((rows, PROJ_WIDTH), lambda i: (i, 0)),
        out_shape=jax.ShapeDtypeStruct((d, PROJ_WIDTH), BF16),
        compiler_params=_compiler_params(),
        name="relayout_w_in",
    )(w_in, *sel)


def _cast_kernel(x_ref, o_ref):
    o_ref[...] = x_ref[0].astype(o_ref.dtype)


def _layer_weight_bf16(w, layer):
    _, r, c = w.shape
    rows = 256
    return pl.pallas_call(
        _cast_kernel,
        grid=(r // rows,),
        in_specs=[pl.BlockSpec((1, rows, c), lambda i: (layer, i, 0))],
        out_specs=pl.BlockSpec((rows, c), lambda i: (i, 0)),
        out_shape=jax.ShapeDtypeStruct((r, c), BF16),
        compiler_params=_compiler_params(),
        name="weight_bf16",
    )(w)


PROJ_WIDTH = 9 * GROUP_W + 512
OFF_QA, OFF_KA, OFF_VA, OFF_UB, OFF_QC, OFF_KC, OFF_VC, OFF_QD, OFF_KD, OFF_VD = (
    0, 256, 512, 768, 1280, 1536, 1792, 2048, 2304, 2560)


def _inproj_kernel(rope, x_ref, g_ref, sh_ref, sc_ref, w_ref, *rest):
    if rope:
        ca_ref, sa_ref, cc_ref, sc2_ref = rest[:4]
        rest = rest[4:]
    qa_o, ka_o, va_o, ub_o, qc_o, kc_o, vc_o, qd_o, kd_o, vd_o = rest
    x = x_ref[...]
    r = lax.rsqrt(jnp.mean(x * x, axis=-1, keepdims=True) + EPS)
    h = (x * r) * g_ref[...]
    h = h * (1.0 + sc_ref[0]) + sh_ref[0]
    hb = h.astype(BF16)

    def proj(off, width=GROUP_W):
        return _dot(hb, w_ref[:, off:off + width])

    def store_rot(o_ref, y, c_ref, s_ref, scale, channel_major=False):
        if rope:
            x1, x2 = y[:, :128], y[:, 128:]
            cs, sn = c_ref[...], s_ref[...]
            y1, y2 = x1 * cs - x2 * sn, x1 * sn + x2 * cs
        else:
            y1, y2 = y[:, :128], y[:, 128:]
        if channel_major:
            o_ref[:128, :] = (y1 * scale).T.astype(BF16)
            o_ref[128:, :] = (y2 * scale).T.astype(BF16)
        else:
            o_ref[:, :128] = (y1 * scale).astype(BF16)
            o_ref[:, 128:] = (y2 * scale).astype(BF16)

    ca = sa = cc = sc2 = None
    if rope:
        ca, sa, cc, sc2 = ca_ref, sa_ref, cc_ref, sc2_ref
    store_rot(qa_o, proj(OFF_QA), ca, sa, ATTN_Q_SCALE)
    store_rot(ka_o, proj(OFF_KA), ca, sa, 1.0)
    va_o[...] = proj(OFF_VA).astype(BF16)
    ub_o[...] = proj(OFF_UB, 512)
    store_rot(qc_o, proj(OFF_QC), cc, sc2, DIFF_Q_SCALE, channel_major=True)
    store_rot(kc_o, proj(OFF_KC), cc, sc2, 1.0)
    vc_o[...] = proj(OFF_VC).T.astype(BF16)
    qd_o[...] = (proj(OFF_QD) * ATTN_Q_SCALE).astype(BF16)
    kd_o[...] = proj(OFF_KD).astype(BF16)
    vd_o[...] = proj(OFF_VD).astype(BF16)


def _mod_spec(d, chunk, mod_rows, tm):
    first_row, rows_per_mod = mod_rows
    tiles_per_mod = rows_per_mod // tm
    return pl.BlockSpec((1, 1, d), lambda i: ((first_row + i // tiles_per_mod) * 6 + chunk, 0, 0))


def _in_projection(x, g, mods, mod_rows, w, rope_tables):
    rows, d = x.shape
    tm = ROW_TILE
    rope = rope_tables is not None
    in_specs = [
        pl.BlockSpec((tm, d), lambda i: (i, 0)),
        pl.BlockSpec((1, d), lambda i: (0, 0)),
        _mod_spec(d, 0, mod_rows, tm), _mod_spec(d, 1, mod_rows, tm),
        _resident((d, PROJ_WIDTH)),
    ]
    args = [x, g.reshape(1, d), mods, mods, w]
    if rope:
        tiles_per_seq = rope_tables[0].shape[0] // tm
        tab_spec = pl.BlockSpec((tm, 128), lambda i: (i % tiles_per_seq, 0))
        in_specs += [tab_spec] * 4
        args += list(rope_tables)
    narrow = pl.BlockSpec((tm, GROUP_W), lambda i: (i, 0))
    wide = pl.BlockSpec((tm, 512), lambda i: (i, 0))
    chan_major = pl.BlockSpec((GROUP_W, tm), lambda i: (0, i))
    out_specs = [narrow, narrow, narrow, wide, chan_major, narrow, chan_major, narrow, narrow, narrow]
    bf = jax.ShapeDtypeStruct((rows, GROUP_W), BF16)
    bf_t = jax.ShapeDtypeStruct((GROUP_W, rows), BF16)
    out_shape = [bf, bf, bf, jax.ShapeDtypeStruct((rows, 512), F32), bf_t, bf, bf_t, bf, bf, bf]
    return pl.pallas_call(
        functools.partial(_inproj_kernel, rope),
        grid=(rows // tm,),
        in_specs=in_specs,
        out_specs=out_specs,
        out_shape=out_shape,
        compiler_params=_compiler_params(),
        name="in_projection",
    )(*args)


def _lane_iota():
    return lax.broadcasted_iota(jnp.int32, (1, GROUP_W), 1)


def _split_half_head(lane):
    return jnp.right_shift(jnp.bitwise_and(lane, 127), 5)


def _natural_head(lane):
    return jnp.right_shift(lane, 6)


def _stack_heads(q, lane_group, groups):
    zero = jnp.zeros_like(q)
    return jnp.concatenate([jnp.where(lane_group == g, q, zero) for g in groups], axis=0)


def _softmax_pv(scores, values, sink_col=None):
    m = functools.reduce(jnp.maximum, [jnp.max(s, axis=-1, keepdims=True) for s in scores])
    if sink_col is not None:
        m = jnp.maximum(m, sink_col)
    es = [jnp.exp2(s - m) for s in scores]
    l = functools.reduce(jnp.add, [jnp.sum(e, axis=-1, keepdims=True) for e in es])
    if sink_col is not None:
        l = l + jnp.exp2(sink_col - m)
    o = functools.reduce(jnp.add, [_dot(e.astype(BF16), v) for e, v in zip(es, values)])
    return o / l


def _select_heads(o, tq):
    head = _natural_head(_lane_iota())
    out = o[0:tq]
    for h in range(1, N_HEADS):
        out = jnp.where(head == h, o[h * tq:(h + 1) * tq], out)
    return out


def _sink_column(sink_ref, tq):
    row = lax.broadcasted_iota(jnp.int32, (N_HEADS * tq, 1), 0)
    col = jnp.full((N_HEADS * tq, 1), sink_ref[N_HEADS - 1], F32)
    for h in range(N_HEADS - 2, -1, -1):
        col = jnp.where(row < (h + 1) * tq, sink_ref[h], col)
    return col * LOG2E


def _win_attn_kernel(sink_ref, q_ref, kl_ref, vl_ref, kc_ref, vc_ref, o_ref):
    seq = q_ref.shape[1]
    lane_head = _split_half_head(_lane_iota())
    sink_col = _sink_column(sink_ref, BLK)
    shape = (BLK, 3 * BLK)
    rel = lax.broadcasted_iota(jnp.int32, shape, 1) - lax.broadcasted_iota(jnp.int32, shape, 0)
    bands = {}
    for n in range(seq // BLK):
        start = min(max((n - 1) * BLK, 0), seq - 3 * BLK)
        shift = start - n * BLK
        if shift not in bands:
            bands[shift] = jnp.where(jnp.abs(rel + shift) <= A_WINDOW, 0.0, NEG_INF)
        qs = _stack_heads(q_ref[0, n * BLK:(n + 1) * BLK, :], lane_head, range(N_HEADS))
        kl = kl_ref[0, start:start + 3 * BLK, :]
        vl = vl_ref[0, start:start + 3 * BLK, :]
        s_loc = _dot_nt(qs, kl)
        s_ctx = _dot_nt(qs, kc_ref[0])
        s_loc = (s_loc.reshape(N_HEADS, BLK, 3 * BLK) + bands[shift][None]).reshape(N_HEADS * BLK, 3 * BLK)
        o = _softmax_pv([s_loc, s_ctx], [vl, vc_ref[0]], sink_col)
        o_ref[0, n * BLK:(n + 1) * BLK, :] = _select_heads(o, BLK).astype(BF16)


def _window_attention(sink, q, k, v, kc, vc):
    b, seq, w = q.shape
    ctx_len = kc.shape[1]
    whole = pl.BlockSpec((1, seq, w), lambda i: (i, 0, 0))
    ctx_spec = pl.BlockSpec((1, ctx_len, w), lambda i: (i, 0, 0))
    return pl.pallas_call(
        _win_attn_kernel,
        grid=(b,),
        in_specs=[pl.BlockSpec(memory_space=pltpu.SMEM), whole, whole, whole, ctx_spec, ctx_spec],
        out_specs=whole,
        out_shape=jax.ShapeDtypeStruct((b, seq, w), BF16),
        compiler_params=_compiler_params(),
        name="window_attention",
    )(sink, q, k, v, kc, vc)


def _na_bias_table(rpb):
    cq = np.arange(GRID_W)
    cs = np.clip(cq - NA_KW // 2, 0, GRID_W - NA_KW)
    col_valid = (cq[None, :] >= cs[:, None]) & (cq[None, :] < cs[:, None] + NA_KW)
    dc = np.clip(cq[None, :] - cq[:, None], -(NA_KW - 1), NA_KW - 1) + (NA_KW - 1)
    n_dr, n_dc = 2 * NA_KH - 1, 2 * NA_KW - 1
    onehot = (dc.reshape(1, -1) == np.arange(n_dc)[:, None]).astype(np.float32)
    t = jnp.dot(rpb.astype(F32).reshape(N_HEADS * n_dr, n_dc), onehot, precision=lax.Precision.HIGHEST)
    t = jnp.where(col_valid[None, None], t.reshape(N_HEADS, n_dr, GRID_W, GRID_W) * LOG2E, NEG_INF)
    t = jnp.transpose(t, (1, 0, 2, 3)).reshape(n_dr, N_HEADS * GRID_W, GRID_W)
    return jnp.concatenate([t[:-1], t[1:]], axis=-1)


def _na_attn_kernel(q_ref, kl_ref, vl_ref, kc_ref, vc_ref, bias_ref, o_ref):
    rows = q_ref.shape[1] // GRID_W
    lane_head = _natural_head(_lane_iota())
    for r in range(rows):
        first = min(max(r - NA_KH // 2, 0), rows - NA_KH)
        start = first * GRID_W
        off = first - r + NA_KH - 1
        qs = _stack_heads(q_ref[0, r * GRID_W:(r + 1) * GRID_W, :], lane_head, range(N_HEADS))
        kl = kl_ref[0, start:start + NA_KH * GRID_W, :]
        vl = vl_ref[0, start:start + NA_KH * GRID_W, :]
        bias = jnp.concatenate([bias_ref[off + 2 * i] for i in range(NA_KH // 2)], axis=1)
        s_loc = _dot_nt(qs, kl) + bias
        s_ctx = _dot_nt(qs, kc_ref[0])
        o = _softmax_pv([s_loc, s_ctx], [vl, vc_ref[0]])
        o_ref[0, r * GRID_W:(r + 1) * GRID_W, :] = _select_heads(o, GRID_W).astype(BF16)


def _neighbourhood_attention(bias, q, k, v, kc, vc):
    b, seq, w = q.shape
    ctx_len = kc.shape[1]
    whole = pl.BlockSpec((1, seq, w), lambda i: (i, 0, 0))
    ctx_spec = pl.BlockSpec((1, ctx_len, w), lambda i: (i, 0, 0))
    return pl.pallas_call(
        _na_attn_kernel,
        grid=(b,),
        in_specs=[whole, whole, whole, ctx_spec, ctx_spec, _resident(bias.shape)],
        out_specs=whole,
        out_shape=jax.ShapeDtypeStruct((b, seq, w), BF16),
        compiler_params=_compiler_params(),
        name="neighbourhood_attention",
    )(q, k, v, kc, vc, bias)


def _ctx_attn_kernel(split_layout, has_sink, *refs):
    if has_sink:
        sink_ref, q_ref, k_ref, v_ref, o_ref = refs
    else:
        q_ref, k_ref, v_ref, o_ref = refs
    tq = q_ref.shape[1]
    lane = _lane_iota()
    group = _split_half_head(lane) if split_layout else _natural_head(lane)
    qs = _stack_heads(q_ref[0], group, range(N_HEADS))
    s = _dot_nt(qs, k_ref[0])
    sink_col = _sink_column(sink_ref, tq) if has_sink else None
    o = _softmax_pv([s], [v_ref[0]], sink_col)
    o_ref[0] = _select_heads(o, tq).astype(BF16)


def _context_attention(q, k, v, sink=None, split_layout=False):
    b, n, w = q.shape
    spec = pl.BlockSpec((1, n, w), lambda i: (i, 0, 0))
    in_specs = [spec, spec, spec]
    args = [q, k, v]
    if sink is not None:
        in_specs = [pl.BlockSpec(memory_space=pltpu.SMEM)] + in_specs
        args = [sink] + args
    return pl.pallas_call(
        functools.partial(_ctx_attn_kernel, split_layout, sink is not None),
        grid=(b,),
        in_specs=in_specs,
        out_specs=spec,
        out_shape=jax.ShapeDtypeStruct((b, n, w), BF16),
        compiler_params=_compiler_params(),
        name="context_attention",
    )(*args)


def _diff_attn_kernel(has_local, lambda_init, lam_ref, subg_ref, qt_ref, *refs):
    if has_local:
        kl_ref, vtl_ref, kc_ref, vtc_ref, o_ref = refs
    else:
        kc_ref, vtc_ref, o_ref = refs
    tq = min(DIFF_TQ, qt_ref.shape[1])
    n_blocks = qt_ref.shape[1] // tq
    lv = lam_ref[...]
    lam = (jnp.exp(jnp.sum(lv[0:1] * lv[1:2], axis=-1, keepdims=True))
           - jnp.exp(jnp.sum(lv[2:3] * lv[3:4], axis=-1, keepdims=True)) + lambda_init)
    chan = lax.broadcasted_iota(jnp.int32, (GROUP_W, 1), 0)
    group = jnp.right_shift(jnp.bitwise_and(chan, 127), 4)
    zero = jnp.zeros((GROUP_W, tq), BF16)
    segs = [(kc_ref, vtc_ref, 0, kc_ref.shape[1])]
    if has_local:
        n_loc = kl_ref.shape[1]
        segs = [(kl_ref, vtl_ref, k0, DIFF_KEY_SEG) for k0 in range(0, n_loc, DIFF_KEY_SEG)] + segs
    sum_rows = 16

    def head_scores(unit):
        blk, h = unit
        qt = qt_ref[:, blk * tq:(blk + 1) * tq]
        qs = jnp.concatenate([jnp.where(group == 2 * h, qt, zero),
                              jnp.where(group == 2 * h + 1, qt, zero)], axis=1)
        return [_dot(k_ref[0, k0:k0 + nk, :], qs) for k_ref, _, k0, nk in segs]

    def column_max(s, slab=64):
        part = jnp.max(s.reshape(s.shape[0] // slab, slab, s.shape[1]), axis=0)
        return jnp.max(part, axis=0, keepdims=True)

    ys = []
    units = [(blk, h) for blk in range(n_blocks) for h in range(N_HEADS)]
    nxt = head_scores(units[0])
    for ui, (blk, h) in enumerate(units):
        scores = nxt
        if ui + 1 < len(units):
            nxt = head_scores(units[ui + 1])
        m = functools.reduce(jnp.maximum, [column_max(s) for s in scores])
        pv = None
        for s, (_, vt_ref, k0, nk) in zip(scores, segs):
            e = jnp.exp2(s - m).astype(BF16)
            vt = vt_ref[h * HEAD_DIM:(h + 1) * HEAD_DIM, k0:k0 + nk]
            lhs = jnp.concatenate([vt, jnp.ones((sum_rows, nk), BF16)], axis=0)
            p = _dot(lhs, e)
            pv = p if pv is None else pv + p
        inv = 1.0 / pv[HEAD_DIM:HEAD_DIM + 1]
        pv = pv[:HEAD_DIM]
        o = pv[:, :tq] * inv[:, :tq] - pv[:, tq:] * (lam * inv[:, tq:])
        ms = jnp.mean(o * o, axis=0, keepdims=True)
        ys.append(o * lax.rsqrt(ms + EPS))
        if h == N_HEADS - 1:
            out = jnp.concatenate(ys, axis=0).T
            o_ref[0, blk * tq:(blk + 1) * tq, :] = (out * subg_ref[...] * (1.0 - lambda_init)).astype(BF16)
            ys = []


def _diff_attention(lam_vecs, subg, lambda_init, b, qt, kc, vtc, k=None, vt=None):
    w = qt.shape[0]
    n = qt.shape[1] // b
    ctx_len = kc.shape[1]
    has_local = k is not None
    tq = min(DIFF_TQ * DIFF_BLOCKS_PER_STEP, n)
    steps = n // tq
    q_spec = pl.BlockSpec((w, tq), lambda i, j: (0, i * steps + j))
    in_specs = [pl.BlockSpec(lam_vecs.shape, lambda i, j: (0, 0)),
                pl.BlockSpec((1, w), lambda i, j: (0, 0)), q_spec]
    args = [lam_vecs, subg, qt]
    if has_local:
        in_specs += [pl.BlockSpec((1, n, w), lambda i, j: (i, 0, 0)),
                     pl.BlockSpec((w, n), lambda i, j: (0, i))]
        args += [k, vt]
    in_specs += [pl.BlockSpec((1, ctx_len, w), lambda i, j: (i, 0, 0)),
                 pl.BlockSpec((w, ctx_len), lambda i, j: (0, i))]
    args += [kc, vtc]
    return pl.pallas_call(
        functools.partial(_diff_attn_kernel, has_local, lambda_init),
        grid=(b, n // tq),
        in_specs=in_specs,
        out_specs=pl.BlockSpec((1, tq, w), lambda i, j: (i, j, 0)),
        out_shape=jax.ShapeDtypeStruct((b, n, w), BF16),
        compiler_params=_compiler_params(),
        name="diff_attention",
    )(*args)


def _conv_kernel(seq, u_ref, w_ref, b_ref, g_ref, beta_ref, o_ref, pad_ref):
    halo, tc, ch = CONV_HALO, CONV_TC, GROUP_W
    zeros = jnp.zeros((halo, ch), F32)
    pad_ref[0:halo, :] = zeros
    pad_ref[halo + seq:2 * halo + seq, :] = zeros

    def glu(i, carry):
        r0 = pl.multiple_of(i * tc, tc)
        u = u_ref[0, pl.ds(r0, tc), :]
        pad_ref[pl.ds(halo + r0, tc), :] = u[:, :ch] * _sigmoid(u[:, ch:])
        return carry

    lax.fori_loop(0, seq // tc, glu, 0)

    win_rows = tc + 2 * halo

    def chunk(i, carry):
        c0 = pl.multiple_of(i * tc, tc)
        win = pad_ref[pl.ds(c0, win_rows), :]
        acc = jnp.zeros((tc, ch), F32)
        for sub in range(8):
            shifted = win if sub == 0 else pltpu.roll(win, win_rows - sub, axis=0)
            for blk8 in range(win_rows // 8):
                tap = 8 * blk8 + sub - (halo - CONV_K // 2)
                if 0 <= tap < CONV_K:
                    acc = acc + shifted[8 * blk8:8 * blk8 + tc] * w_ref[tap:tap + 1, :]
        hcv = acc + b_ref[...]
        mu = jnp.mean(hcv, axis=-1, keepdims=True)
        cen = hcv - mu
        var = jnp.mean(cen * cen, axis=-1, keepdims=True)
        y = cen * lax.rsqrt(var + EPS) * g_ref[...] + beta_ref[...]
        o_ref[0, pl.ds(c0, tc), :] = (y * _sigmoid(y)).astype(BF16)
        return carry

    lax.fori_loop(0, seq // tc, chunk, 0, unroll=4)


def _conformer_conv(u, w, bias, ln_g, ln_b):
    b, seq, two_ch = u.shape
    ch = two_ch // 2
    vec = pl.BlockSpec((1, ch), lambda i: (0, 0))
    return pl.pallas_call(
        functools.partial(_conv_kernel, seq),
        grid=(b,),
        in_specs=[pl.BlockSpec((1, seq, two_ch), lambda i: (i, 0, 0)),
                  pl.BlockSpec((CONV_K, ch), lambda i: (0, 0)), vec, vec, vec],
        out_specs=pl.BlockSpec((1, seq, ch), lambda i: (i, 0, 0)),
        out_shape=jax.ShapeDtypeStruct((b, seq, ch), BF16),
        scratch_shapes=[pltpu.VMEM((seq + 2 * CONV_HALO, ch), F32)],
        compiler_params=_compiler_params(),
        name="conformer_conv",
    )(u, w, bias.reshape(1, ch), ln_g.reshape(1, ch), ln_b.reshape(1, ch))


def _ffn_kernel(final, x_ref, ya_ref, yb_ref, yc_ref, yd_ref, g1_ref, sh_ref, sc_ref, g2_ref, ng_ref,
                wo_ref, wg_ref, wu_ref, wd_ref, *rest):
    if final:
        fg_ref, o_ref, ycat_ref, act_ref = rest
    else:
        o_ref, ycat_ref, act_ref = rest
    for j, y_ref in enumerate((ya_ref, yb_ref, yc_ref, yd_ref)):
        ycat_ref[:, j * GROUP_W:(j + 1) * GROUP_W] = y_ref[...]
    x = x_ref[...] + g1_ref[0] * _dot(ycat_ref[...], wo_ref[...])
    r = lax.rsqrt(jnp.mean(x * x, axis=-1, keepdims=True) + EPS)
    h = (x * r) * ng_ref[...]
    hb = (h * (1.0 + sc_ref[0]) + sh_ref[0]).astype(BF16)
    for c in range(0, FFN_HIDDEN, FFN_CHUNK):
        gate = _dot(hb, wg_ref[:, c:c + FFN_CHUNK])
        up = _dot(hb, wu_ref[:, c:c + FFN_CHUNK])
        act_ref[:, c:c + FFN_CHUNK] = (gate * _sigmoid(gate) * up).astype(BF16)
    x = x + g2_ref[0] * _dot(act_ref[...], wd_ref[...])
    if final:
        r = lax.rsqrt(jnp.mean(x * x, axis=-1, keepdims=True) + EPS)
        x = (x * r) * fg_ref[...]
    o_ref[...] = x


def _out_projection_ffn(x, ys, mods, mod_rows, norm_g, wo, wg, wu, wd, final_g=None):
    rows, d = x.shape
    tm = ROW_TILE
    final = final_g is not None
    row_spec = pl.BlockSpec((tm, d), lambda i: (i, 0))
    y_spec = pl.BlockSpec((tm, GROUP_W), lambda i: (i, 0))
    vec_spec = pl.BlockSpec((1, d), lambda i: (0, 0))
    in_specs = ([row_spec, y_spec, y_spec, y_spec, y_spec]
                + [_mod_spec(d, chunk, mod_rows, tm) for chunk in (2, 3, 4, 5)]
                + [vec_spec, _resident(wo.shape), _resident(wg.shape), _resident(wu.shape), _resident(wd.shape)])
    args = [x, *ys, mods, mods, mods, mods, norm_g.reshape(1, d), wo, wg, wu, wd]
    if final:
        in_specs.append(vec_spec)
        args.append(final_g.reshape(1, d))
    return pl.pallas_call(
        functools.partial(_ffn_kernel, final),
        grid=(rows // tm,),
        in_specs=in_specs,
        out_specs=row_spec,
        out_shape=jax.ShapeDtypeStruct((rows, d), F32),
        scratch_shapes=[pltpu.VMEM((tm, d), BF16), pltpu.VMEM((tm, FFN_HIDDEN), BF16)],
        compiler_params=_compiler_params(),
        name="out_projection_ffn",
    )(*args)


def _rope_tables(n_tok, dim):
    t = np.arange(n_tok)
    row = (t // GRID_W).astype(np.float32)
    col = (t % GRID_W).astype(np.float32)
    nf = dim // 4
    inv = (np.float32(ROPE_BASE) ** (-np.arange(nf, dtype=np.float32) / np.float32(nf))).astype(np.float32)
    ang = np.concatenate([row[:, None] * inv, col[:, None] * inv], axis=-1).astype(np.float64)
    reps = 128 // (dim // 2)
    return (jnp.asarray(np.tile(np.cos(ang), (1, reps)), F32), jnp.asarray(np.tile(np.sin(ang), (1, reps)), F32))


def kernel(x, c, ctx, c_ctx, norm1_g, norm2_g, w_ada, b_ada, w_in, w_out, attn_sink, conv_w, conv_b,
           conv_ln_g, conv_ln_b, diff_lq1, diff_lk1, diff_lq2, diff_lk2, diff_subln_g, na_rpb,
           w_gate, w_up, w_down, final_g):
    batch, seq, d = x.shape
    ctx_len = ctx.shape[1]
    depth = w_ada.shape[0]

    cvec = jnp.zeros((ADA_ROWS, d), F32).at[:batch].set(c).at[batch].set(c_ctx)
    mods_all = _ada_table(cvec, w_ada, b_ada).reshape(depth, ADA_ROWS * 6, 1, d)
    rope = _rope_tables(seq, HEAD_DIM) + _rope_tables(seq, C_QK_DIM)
    lat_rows = (0, seq)
    ctx_rows = (batch, batch * ctx_len)

    xl = x.reshape(batch * seq, d)
    xc = ctx.reshape(batch * ctx_len, d)
    for l in range(depth):
        ctx_needed = l < depth - 1
        mods = mods_all[l]
        lambda_init = 0.8 - 0.6 * math.exp(-0.3 * l)
        w_in_l = _relayout_w_in(w_in, l)
        wo, wg, wu, wd = (_layer_weight_bf16(w, l) for w in (w_out, w_gate, w_up, w_down))
        lam_vecs = jnp.stack([diff_lq1[l], diff_lk1[l], diff_lq2[l], diff_lk2[l]]).astype(F32)
        subg = jnp.tile(diff_subln_g[l], N_HEADS).reshape(1, GROUP_W)
        cw = conv_w[l].reshape(CONV_K, GROUP_W)

        def per_batch(tensors, n_tok):
            return [t if i in (4, 6) else t.reshape(batch, n_tok, -1) for i, t in enumerate(tensors)]

        qa, ka, va, ub, qct, kc, vct, qd, kd, vd = per_batch(
            _in_projection(xl, norm1_g[l], mods, lat_rows, w_in_l, rope), seq)
        qa_c, ka_c, va_c, ub_c, qct_c, kc_c, vct_c, qd_c, kd_c, vd_c = per_batch(
            _in_projection(xc, norm1_g[l], mods, ctx_rows, w_in_l, None), ctx_len)

        y_a = _window_attention(attn_sink[l], qa, ka, va, ka_c, va_c)
        y_b = _conformer_conv(ub, cw, conv_b[l], conv_ln_g[l], conv_ln_b[l])
        y_c = _diff_attention(lam_vecs, subg, lambda_init, batch, qct, kc_c, vct_c, kc, vct)
        y_d = _neighbourhood_attention(_na_bias_table(na_rpb[l]), qd, kd, vd, kd_c, vd_c)
        ys = [t.reshape(batch * seq, GROUP_W) for t in (y_a, y_b, y_c, y_d)]
        xl = _out_projection_ffn(xl, ys, mods, lat_rows, norm2_g[l], wo, wg, wu, wd,
                                 final_g=None if ctx_needed else final_g)
        if ctx_needed:
            yc_a = _context_attention(qa_c, ka_c, va_c, sink=attn_sink[l], split_layout=True)
            yc_b = _conformer_conv(ub_c, cw, conv_b[l], conv_ln_g[l], conv_ln_b[l])
            yc_c = _diff_attention(lam_vecs, subg, lambda_init, batch, qct_c, kc_c, vct_c)
            yc_d = _context_attention(qd_c, kd_c, vd_c)
            ycs = [t.reshape(batch * ctx_len, GROUP_W) for t in (yc_a, yc_b, yc_c, yc_d)]
            xc = _out_projection_ffn(xc, ycs, mods, ctx_rows, norm2_g[l], wo, wg, wu, wd)
    return xl.reshape(batch, seq, d)
```

```python
import functools
import math

import numpy as np
import jax
import jax.numpy as jnp
from jax import lax
from jax.experimental import pallas as pl
from jax.experimental.pallas import tpu as pltpu

F32 = jnp.float32
BF16 = jnp.bfloat16

D_MODEL = 1024
DEPTH = 2
GRID_W = 64
HEAD_DIM = 64
GROUP_W = 256
N_HEADS = 4
A_KV_HEADS = 2
A_WINDOW = 128
BLK = 128
CONV_K = 31
C_QK_DIM = 32
NA_KH = 8
NA_KW = 16
FFN_HIDDEN = 2816
ROPE_BASE = 10000.0
EPS = 1e-6
NEG_INF = -1e30

VMEM_LIMIT = 56 * 1024 * 1024
BF16_SUBLANES = 16
ADA_ROWS = 16
ADA_TN = 1536
ROW_TILE = 512
FFN_CHUNK = 256
CONV_TC = 64
CONV_HALO = 16
DIFF_TQ = 512
LOG2E = math.log2(math.e)
ATTN_Q_SCALE = (HEAD_DIM ** -0.5) * LOG2E
DIFF_Q_SCALE = (C_QK_DIM ** -0.5) * LOG2E
DIFF_BLOCKS_PER_STEP = 2
DIFF_KEY_SEG = 1024


def _compiler_params():
    return pltpu.CompilerParams(vmem_limit_bytes=VMEM_LIMIT)


def _resident(shape):
    return pl.BlockSpec(shape, lambda *_: (0,) * len(shape), pipeline_mode=pl.Buffered(1))


def _sigmoid(v):
    return 1.0 / (1.0 + jnp.exp(-v))


def _dot(a, b):
    return jnp.dot(a, b, preferred_element_type=F32)


def _dot_nt(a, b):
    return lax.dot_general(a, b, (((1,), (1,)), ((), ())), preferred_element_type=F32)


def _ada_kernel(c_ref, w_ref, b_ref, o_ref):
    cv = c_ref[...]
    s = cv * _sigmoid(cv)
    o_ref[0] = _dot(s.astype(BF16), w_ref[0].astype(BF16)) + b_ref[0]


def _ada_table(cvec, w_ada, b_ada):
    depth, d, n = w_ada.shape
    return pl.pallas_call(
        _ada_kernel,
        grid=(depth, n // ADA_TN),
        in_specs=[
            pl.BlockSpec((ADA_ROWS, d), lambda l, j: (0, 0)),
            pl.BlockSpec((1, d, ADA_TN), lambda l, j: (l, 0, j)),
            pl.BlockSpec((1, 1, ADA_TN), lambda l, j: (l, 0, j)),
        ],
        out_specs=pl.BlockSpec((1, ADA_ROWS, ADA_TN), lambda l, j: (l, 0, j)),
        out_shape=jax.ShapeDtypeStruct((depth, ADA_ROWS, n), F32),
        compiler_params=_compiler_params(),
        name="ada_table",
    )(cvec, w_ada, b_ada.reshape(depth, 1, n))


def _selection_matrices():
    lane = np.arange(GROUP_W)
    part, h, j = lane // 128, (lane % 128) // 32, lane % 32
    grp, jc = (lane % 128) // 16, lane % 16
    sources = (
        (GROUP_W, h * HEAD_DIM + part * 32 + j),
        (A_KV_HEADS * HEAD_DIM, (h // 2) * HEAD_DIM + part * 32 + j),
        (A_KV_HEADS * HEAD_DIM, ((lane // HEAD_DIM) // 2) * HEAD_DIM + lane % HEAD_DIM),
        (GROUP_W, grp * C_QK_DIM + part * 16 + jc),
    )
    return [np.equal(np.arange(n_src)[:, None], src[None, :]).astype(np.float32) for n_src, src in sources]


def _relayout_kernel(w_ref, s_qa, s_ka, s_va, s_c, o_ref):
    def cols(a, b):
        return w_ref[0, :, a:b].astype(BF16)

    def select(a, b, s_ref):
        return _dot(cols(a, b), s_ref[...]).astype(BF16)

    o_ref[:, OFF_QA:OFF_KA] = select(0, 256, s_qa)
    o_ref[:, OFF_KA:OFF_VA] = select(256, 384, s_ka)
    o_ref[:, OFF_VA:OFF_UB] = select(384, 512, s_va)
    o_ref[:, OFF_UB:OFF_QC] = cols(512, 1024)
    o_ref[:, OFF_QC:OFF_KC] = select(1024, 1280, s_c)
    o_ref[:, OFF_KC:OFF_VC] = select(1280, 1536, s_c)
    o_ref[:, OFF_VC:PROJ_WIDTH] = cols(1536, 2560)


def _relayout_w_in(w_in, layer):
    _, d, n_in = w_in.shape
    rows = 256
    sel = [jnp.asarray(s, BF16) for s in _selection_matrices()]
    return pl.pallas_call(
        _relayout_kernel,
        grid=(d // rows,),
        in_specs=[pl.BlockSpec((1, rows, n_in), lambda i: (layer, i, 0))]
                 + [pl.BlockSpec(s.shape, lambda i: (0, 0)) for s in sel],
        out_specs=pl.BlockSpec((rows, PROJ_WIDTH), lambda i: (i, 0)),
        out_shape=jax.ShapeDtypeStruct((d, PROJ_WIDTH), BF16),
        compiler_params=_compiler_params(),
        name="relayout_w_in",
    )(w_in, *sel)


PROJ_WIDTH = 9 * GROUP_W + 512
OFF_QA, OFF_KA, OFF_VA, OFF_UB, OFF_QC, OFF_KC, OFF_VC, OFF_QD, OFF_KD, OFF_VD = (
    0, 256, 512, 768, 1280, 1536, 1792, 2048, 2304, 2560)


def _inproj_kernel(rope, n_cast, x_ref, g_ref, sh_ref, sc_ref, w_ref, *rest):
    if rope:
        ca_ref, sa_ref, cc_ref, sc2_ref = rest[:4]
        rest = rest[4:]
    cast_in, rest = rest[:n_cast], rest[n_cast:]
    qa_o, ka_o, va_o, ub_o, qc_o, kc_o, vc_o, qd_o, kd_o, vd_o = rest[:10]
    for src_ref, dst_ref in zip(cast_in, rest[10:]):
        dst_ref[...] = src_ref[0].astype(BF16)
    x = x_ref[...]
    r = lax.rsqrt(jnp.mean(x * x, axis=-1, keepdims=True) + EPS)
    h = (x * r) * g_ref[...]
    h = h * (1.0 + sc_ref[0]) + sh_ref[0]
    hb = h.astype(BF16)

    def proj(off, width=GROUP_W):
        return _dot(hb, w_ref[:, off:off + width])

    def store_rot(o_ref, y, c_ref, s_ref, scale, channel_major=False):
        if rope:
            x1, x2 = y[:, :128], y[:, 128:]
            cs, sn = c_ref[...], s_ref[...]
            y1, y2 = x1 * cs - x2 * sn, x1 * sn + x2 * cs
        else:
            y1, y2 = y[:, :128], y[:, 128:]
        if channel_major:
            o_ref[:128, :] = (y1 * scale).T.astype(BF16)
            o_ref[128:, :] = (y2 * scale).T.astype(BF16)
        else:
            o_ref[:, :128] = (y1 * scale).astype(BF16)
            o_ref[:, 128:] = (y2 * scale).astype(BF16)

    ca = sa = cc = sc2 = None
    if rope:
        ca, sa, cc, sc2 = ca_ref, sa_ref, cc_ref, sc2_ref
    store_rot(qa_o, proj(OFF_QA), ca, sa, ATTN_Q_SCALE)
    store_rot(ka_o, proj(OFF_KA), ca, sa, 1.0)
    va_o[...] = proj(OFF_VA).astype(BF16)
    ub_o[...] = proj(OFF_UB, 512)
    store_rot(qc_o, proj(OFF_QC), cc, sc2, DIFF_Q_SCALE, channel_major=True)
    store_rot(kc_o, proj(OFF_KC), cc, sc2, 1.0)
    vc_o[...] = proj(OFF_VC).T.astype(BF16)
    qd_o[...] = (proj(OFF_QD) * ATTN_Q_SCALE).astype(BF16)
    kd_o[...] = proj(OFF_KD).astype(BF16)
    vd_o[...] = proj(OFF_VD).astype(BF16)


def _mod_spec(d, chunk, mod_rows, tm):
    first_row, rows_per_mod = mod_rows
    tiles_per_mod = rows_per_mod // tm
    return pl.BlockSpec((1, 1, d), lambda i: ((first_row + i // tiles_per_mod) * 6 + chunk, 0, 0))


def _in_projection(x, g, mods, mod_rows, w, rope_tables, cast_weights=(), layer=0):
    rows, d = x.shape
    tm = ROW_TILE
    steps = rows // tm
    rope = rope_tables is not None
    in_specs = [
        pl.BlockSpec((tm, d), lambda i: (i, 0)),
        pl.BlockSpec((1, d), lambda i: (0, 0)),
        _mod_spec(d, 0, mod_rows, tm), _mod_spec(d, 1, mod_rows, tm),
        _resident((d, PROJ_WIDTH)),
    ]
    args = [x, g.reshape(1, d), mods, mods, w]
    if rope:
        tiles_per_seq = rope_tables[0].shape[0] // tm
        tab_spec = pl.BlockSpec((tm, 128), lambda i: (i % tiles_per_seq, 0))
        in_specs += [tab_spec] * 4
        args += list(rope_tables)
    narrow = pl.BlockSpec((tm, GROUP_W), lambda i: (i, 0))
    wide = pl.BlockSpec((tm, 512), lambda i: (i, 0))
    chan_major = pl.BlockSpec((GROUP_W, tm), lambda i: (0, i))
    out_specs = [narrow, narrow, narrow, wide, chan_major, narrow, chan_major, narrow, narrow, narrow]
    bf = jax.ShapeDtypeStruct((rows, GROUP_W), BF16)
    bf_t = jax.ShapeDtypeStruct((GROUP_W, rows), BF16)
    out_shape = [bf, bf, bf, jax.ShapeDtypeStruct((rows, 512), F32), bf_t, bf, bf_t, bf, bf, bf]
    for cw in cast_weights:
        _, r_w, c_w = cw.shape
        blk_rows = next(n for n in range(BF16_SUBLANES, r_w + 1, BF16_SUBLANES)
                        if r_w % n == 0 and steps % (r_w // n) == 0 and r_w // n <= steps)
        steps_per_blk = steps // (r_w // blk_rows)
        in_specs.append(pl.BlockSpec((1, blk_rows, c_w), lambda i, s=steps_per_blk: (layer, i // s, 0)))
        args.append(cw)
        out_specs.append(pl.BlockSpec((blk_rows, c_w), lambda i, s=steps_per_blk: (i // s, 0)))
        out_shape.append(jax.ShapeDtypeStruct((r_w, c_w), BF16))
    return pl.pallas_call(
        functools.partial(_inproj_kernel, rope, len(cast_weights)),
        grid=(steps,),
        in_specs=in_specs,
        out_specs=out_specs,
        out_shape=out_shape,
        compiler_params=_compiler_params(),
        name="in_projection",
    )(*args)


def _lane_iota():
    return lax.broadcasted_iota(jnp.int32, (1, GROUP_W), 1)


def _split_half_head(lane):
    return jnp.right_shift(jnp.bitwise_and(lane, 127), 5)


def _natural_head(lane):
    return jnp.right_shift(lane, 6)


def _stack_heads(q, lane_group, groups):
    zero = jnp.zeros_like(q)
    return jnp.concatenate([jnp.where(lane_group == g, q, zero) for g in groups], axis=0)


def _softmax_pv(scores, values, sink_col=None):
    m = functools.reduce(jnp.maximum, [jnp.max(s, axis=-1, keepdims=True) for s in scores])
    if sink_col is not None:
        m = jnp.maximum(m, sink_col)
    es = [jnp.exp2(s - m) for s in scores]
    l = functools.reduce(jnp.add, [jnp.sum(e, axis=-1, keepdims=True) for e in es])
    if sink_col is not None:
        l = l + jnp.exp2(sink_col - m)
    o = functools.reduce(jnp.add, [_dot(e.astype(BF16), v) for e, v in zip(es, values)])
    return o / l


def _select_heads(o, tq):
    head = _natural_head(_lane_iota())
    out = o[0:tq]
    for h in range(1, N_HEADS):
        out = jnp.where(head == h, o[h * tq:(h + 1) * tq], out)
    return out


def _sink_column(sink_ref, tq):
    row = lax.broadcasted_iota(jnp.int32, (N_HEADS * tq, 1), 0)
    col = jnp.full((N_HEADS * tq, 1), sink_ref[N_HEADS - 1], F32)
    for h in range(N_HEADS - 2, -1, -1):
        col = jnp.where(row < (h + 1) * tq, sink_ref[h], col)
    return col * LOG2E


def _win_attn_kernel(sink_ref, q_ref, kl_ref, vl_ref, kc_ref, vc_ref, o_ref):
    seq = q_ref.shape[1]
    lane_head = _split_half_head(_lane_iota())
    sink_col = _sink_column(sink_ref, BLK)
    shape = (BLK, 3 * BLK)
    rel = lax.broadcasted_iota(jnp.int32, shape, 1) - lax.broadcasted_iota(jnp.int32, shape, 0)
    bands = {}
    for n in range(seq // BLK):
        start = min(max((n - 1) * BLK, 0), seq - 3 * BLK)
        shift = start - n * BLK
        if shift not in bands:
            bands[shift] = jnp.where(jnp.abs(rel + shift) <= A_WINDOW, 0.0, NEG_INF)
        qs = _stack_heads(q_ref[0, n * BLK:(n + 1) * BLK, :], lane_head, range(N_HEADS))
        kl = kl_ref[0, start:start + 3 * BLK, :]
        vl = vl_ref[0, start:start + 3 * BLK, :]
        s_loc = _dot_nt(qs, kl)
        s_ctx = _dot_nt(qs, kc_ref[0])
        s_loc = (s_loc.reshape(N_HEADS, BLK, 3 * BLK) + bands[shift][None]).reshape(N_HEADS * BLK, 3 * BLK)
        o = _softmax_pv([s_loc, s_ctx], [vl, vc_ref[0]], sink_col)
        o_ref[0, n * BLK:(n + 1) * BLK, :] = _select_heads(o, BLK).astype(BF16)


def _window_attention(sink, q, k, v, kc, vc):
    b, seq, w = q.shape
    ctx_len = kc.shape[1]
    whole = pl.BlockSpec((1, seq, w), lambda i: (i, 0, 0))
    ctx_spec = pl.BlockSpec((1, ctx_len, w), lambda i: (i, 0, 0))
    return pl.pallas_call(
        _win_attn_kernel,
        grid=(b,),
        in_specs=[pl.BlockSpec(memory_space=pltpu.SMEM), whole, whole, whole, ctx_spec, ctx_spec],
        out_specs=whole,
        out_shape=jax.ShapeDtypeStruct((b, seq, w), BF16),
        compiler_params=_compiler_params(),
        name="window_attention",
    )(sink, q, k, v, kc, vc)


def _na_bias_table(rpb):
    cq = np.arange(GRID_W)
    cs = np.clip(cq - NA_KW // 2, 0, GRID_W - NA_KW)
    col_valid = (cq[None, :] >= cs[:, None]) & (cq[None, :] < cs[:, None] + NA_KW)
    dc = np.clip(cq[None, :] - cq[:, None], -(NA_KW - 1), NA_KW - 1) + (NA_KW - 1)
    n_dr, n_dc = 2 * NA_KH - 1, 2 * NA_KW - 1
    onehot = (dc.reshape(1, -1) == np.arange(n_dc)[:, None]).astype(np.float32)
    t = jnp.dot(rpb.astype(F32).reshape(N_HEADS * n_dr, n_dc), onehot, precision=lax.Precision.HIGHEST)
    t = jnp.where(col_valid[None, None], t.reshape(N_HEADS, n_dr, GRID_W, GRID_W) * LOG2E, NEG_INF)
    t = jnp.transpose(t, (1, 0, 2, 3)).reshape(n_dr, N_HEADS * GRID_W, GRID_W)
    return jnp.concatenate([t[:-1], t[1:]], axis=-1)


def _na_attn_kernel(q_ref, kl_ref, vl_ref, kc_ref, vc_ref, bias_ref, o_ref):
    rows = q_ref.shape[1] // GRID_W
    lane_head = _natural_head(_lane_iota())
    for r in range(rows):
        first = min(max(r - NA_KH // 2, 0), rows - NA_KH)
        start = first * GRID_W
        off = first - r + NA_KH - 1
        qs = _stack_heads(q_ref[0, r * GRID_W:(r + 1) * GRID_W, :], lane_head, range(N_HEADS))
        kl = kl_ref[0, start:start + NA_KH * GRID_W, :]
        vl = vl_ref[0, start:start + NA_KH * GRID_W, :]
        bias = jnp.concatenate([bias_ref[off + 2 * i] for i in range(NA_KH // 2)], axis=1)
        s_loc = _dot_nt(qs, kl) + bias
        s_ctx = _dot_nt(qs, kc_ref[0])
        o = _softmax_pv([s_loc, s_ctx], [vl, vc_ref[0]])
        o_ref[0, r * GRID_W:(r + 1) * GRID_W, :] = _select_heads(o, GRID_W).astype(BF16)


def _neighbourhood_attention(bias, q, k, v, kc, vc):
    b, seq, w = q.shape
    ctx_len = kc.shape[1]
    whole = pl.BlockSpec((1, seq, w), lambda i: (i, 0, 0))
    ctx_spec = pl.BlockSpec((1, ctx_len, w), lambda i: (i, 0, 0))
    return pl.pallas_call(
        _na_attn_kernel,
        grid=(b,),
        in_specs=[whole, whole, whole, ctx_spec, ctx_spec, _resident(bias.shape)],
        out_specs=whole,
        out_shape=jax.ShapeDtypeStruct((b, seq, w), BF16),
        compiler_params=_compiler_params(),
        name="neighbourhood_attention",
    )(q, k, v, kc, vc, bias)


def _ctx_attn_kernel(split_layout, has_sink, *refs):
    if has_sink:
        sink_ref, q_ref, k_ref, v_ref, o_ref = refs
    else:
        q_ref, k_ref, v_ref, o_ref = refs
    tq = q_ref.shape[1]
    lane = _lane_iota()
    group = _split_half_head(lane) if split_layout else _natural_head(lane)
    qs = _stack_heads(q_ref[0], group, range(N_HEADS))
    s = _dot_nt(qs, k_ref[0])
    sink_col = _sink_column(sink_ref, tq) if has_sink else None
    o = _softmax_pv([s], [v_ref[0]], sink_col)
    o_ref[0] = _select_heads(o, tq).astype(BF16)


def _context_attention(q, k, v, sink=None, split_layout=False):
    b, n, w = q.shape
    spec = pl.BlockSpec((1, n, w), lambda i: (i, 0, 0))
    in_specs = [spec, spec, spec]
    args = [q, k, v]
    if sink is not None:
        in_specs = [pl.BlockSpec(memory_space=pltpu.SMEM)] + in_specs
        args = [sink] + args
    return pl.pallas_call(
        functools.partial(_ctx_attn_kernel, split_layout, sink is not None),
        grid=(b,),
        in_specs=in_specs,
        out_specs=spec,
        out_shape=jax.ShapeDtypeStruct((b, n, w), BF16),
        compiler_params=_compiler_params(),
        name="context_attention",
    )(*args)


def _diff_attn_kernel(has_local, lambda_init, lam_ref, subg_ref, qt_ref, *refs):
    if has_local:
        kl_ref, vtl_ref, kc_ref, vtc_ref, o_ref = refs
    else:
        kc_ref, vtc_ref, o_ref = refs
    tq = min(DIFF_TQ, qt_ref.shape[1])
    n_blocks = qt_ref.shape[1] // tq
    lv = lam_ref[...]
    lam = (jnp.exp(jnp.sum(lv[0:1] * lv[1:2], axis=-1, keepdims=True))
           - jnp.exp(jnp.sum(lv[2:3] * lv[3:4], axis=-1, keepdims=True)) + lambda_init)
    chan = lax.broadcasted_iota(jnp.int32, (GROUP_W, 1), 0)
    group = jnp.right_shift(jnp.bitwise_and(chan, 127), 4)
    zero = jnp.zeros((GROUP_W, tq), BF16)
    segs = [(kc_ref, vtc_ref, 0, kc_ref.shape[1])]
    if has_local:
        n_loc = kl_ref.shape[1]
        segs = [(kl_ref, vtl_ref, k0, DIFF_KEY_SEG) for k0 in range(0, n_loc, DIFF_KEY_SEG)] + segs
    sum_rows = 16

    def head_scores(unit):
        blk, h = unit
        qt = qt_ref[:, blk * tq:(blk + 1) * tq]
        qs = jnp.concatenate([jnp.where(group == 2 * h, qt, zero),
                              jnp.where(group == 2 * h + 1, qt, zero)], axis=1)
        return [_dot(k_ref[0, k0:k0 + nk, :], qs) for k_ref, _, k0, nk in segs]

    def column_max(s, slab=64):
        part = jnp.max(s.reshape(s.shape[0] // slab, slab, s.shape[1]), axis=0)
        return jnp.max(part, axis=0, keepdims=True)

    ys = []
    units = [(blk, h) for blk in range(n_blocks) for h in range(N_HEADS)]
    nxt = head_scores(units[0])
    for ui, (blk, h) in enumerate(units):
        scores = nxt
        if ui + 1 < len(units):
            nxt = head_scores(units[ui + 1])
        m = functools.reduce(jnp.maximum, [column_max(s) for s in scores])
        pv = None
        for s, (_, vt_ref, k0, nk) in zip(scores, segs):
            e = jnp.exp2(s - m).astype(BF16)
            vt = vt_ref[h * HEAD_DIM:(h + 1) * HEAD_DIM, k0:k0 + nk]
            lhs = jnp.concatenate([vt, jnp.ones((sum_rows, nk), BF16)], axis=0)
            p = _dot(lhs, e)
            pv = p if pv is None else pv + p
        inv = 1.0 / pv[HEAD_DIM:HEAD_DIM + 1]
        pv = pv[:HEAD_DIM]
        o = pv[:, :tq] * inv[:, :tq] - pv[:, tq:] * (lam * inv[:, tq:])
        ms = jnp.mean(o * o, axis=0, keepdims=True)
        ys.append(o * lax.rsqrt(ms + EPS))
        if h == N_HEADS - 1:
            out = jnp.concatenate(ys, axis=0).T
            o_ref[0, blk * tq:(blk + 1) * tq, :] = (out * subg_ref[...] * (1.0 - lambda_init)).astype(BF16)
            ys = []


def _diff_attention(lam_vecs, subg, lambda_init, b, qt, kc, vtc, k=None, vt=None):
    w = qt.shape[0]
    n = qt.shape[1] // b
    ctx_len = kc.shape[1]
    has_local = k is not None
    tq = min(DIFF_TQ * DIFF_BLOCKS_PER_STEP, n)
    steps = n // tq
    q_spec = pl.BlockSpec((w, tq), lambda i, j: (0, i * steps + j))
    in_specs = [pl.BlockSpec(lam_vecs.shape, lambda i, j: (0, 0)),
                pl.BlockSpec((1, w), lambda i, j: (0, 0)), q_spec]
    args = [lam_vecs, subg, qt]
    if has_local:
        in_specs += [pl.BlockSpec((1, n, w), lambda i, j: (i, 0, 0)),
                     pl.BlockSpec((w, n), lambda i, j: (0, i))]
        args += [k, vt]
    in_specs += [pl.BlockSpec((1, ctx_len, w), lambda i, j: (i, 0, 0)),
                 pl.BlockSpec((w, ctx_len), lambda i, j: (0, i))]
    args += [kc, vtc]
    return pl.pallas_call(
        functools.partial(_diff_attn_kernel, has_local, lambda_init),
        grid=(b, n // tq),
        in_specs=in_specs,
        out_specs=pl.BlockSpec((1, tq, w), lambda i, j: (i, j, 0)),
        out_shape=jax.ShapeDtypeStruct((b, n, w), BF16),
        compiler_params=_compiler_params(),
        name="diff_attention",
    )(*args)


def _conv_kernel(seq, u_ref, w_ref, b_ref, g_ref, beta_ref, o_ref, pad_ref):
    halo, tc, ch = CONV_HALO, CONV_TC, GROUP_W
    zeros = jnp.zeros((halo, ch), F32)
    pad_ref[0:halo, :] = zeros
    pad_ref[halo + seq:2 * halo + seq, :] = zeros

    def glu(i, carry):
        r0 = pl.multiple_of(i * tc, tc)
        u = u_ref[0, pl.ds(r0, tc), :]
        pad_ref[pl.ds(halo + r0, tc), :] = u[:, :ch] * _sigmoid(u[:, ch:])
        return carry

    lax.fori_loop(0, seq // tc, glu, 0)

    win_rows = tc + 2 * halo

    def chunk(i, carry):
        c0 = pl.multiple_of(i * tc, tc)
        win = pad_ref[pl.ds(c0, win_rows), :]
        acc = jnp.zeros((tc, ch), F32)
        for sub in range(8):
            shifted = win if sub == 0 else pltpu.roll(win, win_rows - sub, axis=0)
            for blk8 in range(win_rows // 8):
                tap = 8 * blk8 + sub - (halo - CONV_K // 2)
                if 0 <= tap < CONV_K:
                    acc = acc + shifted[8 * blk8:8 * blk8 + tc] * w_ref[tap:tap + 1, :]
        hcv = acc + b_ref[...]
        mu = jnp.mean(hcv, axis=-1, keepdims=True)
        cen = hcv - mu
        var = jnp.mean(cen * cen, axis=-1, keepdims=True)
        y = cen * lax.rsqrt(var + EPS) * g_ref[...] + beta_ref[...]
        o_ref[0, pl.ds(c0, tc), :] = (y * _sigmoid(y)).astype(BF16)
        return carry

    lax.fori_loop(0, seq // tc, chunk, 0, unroll=4)


def _conformer_conv(u, w, bias, ln_g, ln_b):
    b, seq, two_ch = u.shape
    ch = two_ch // 2
    vec = pl.BlockSpec((1, ch), lambda i: (0, 0))
    return pl.pallas_call(
        functools.partial(_conv_kernel, seq),
        grid=(b,),
        in_specs=[pl.BlockSpec((1, seq, two_ch), lambda i: (i, 0, 0)),
                  pl.BlockSpec((CONV_K, ch), lambda i: (0, 0)), vec, vec, vec],
        out_specs=pl.BlockSpec((1, seq, ch), lambda i: (i, 0, 0)),
        out_shape=jax.ShapeDtypeStruct((b, seq, ch), BF16),
        scratch_shapes=[pltpu.VMEM((seq + 2 * CONV_HALO, ch), F32)],
        compiler_params=_compiler_params(),
        name="conformer_conv",
    )(u, w, bias.reshape(1, ch), ln_g.reshape(1, ch), ln_b.reshape(1, ch))


def _ffn_kernel(final, x_ref, ya_ref, yb_ref, yc_ref, yd_ref, g1_ref, sh_ref, sc_ref, g2_ref, ng_ref,
                wo_ref, wg_ref, wu_ref, wd_ref, *rest):
    if final:
        fg_ref, o_ref, ycat_ref, act_ref = rest
    else:
        o_ref, ycat_ref, act_ref = rest
    for j, y_ref in enumerate((ya_ref, yb_ref, yc_ref, yd_ref)):
        ycat_ref[:, j * GROUP_W:(j + 1) * GROUP_W] = y_ref[...]
    x = x_ref[...] + g1_ref[0] * _dot(ycat_ref[...], wo_ref[...])
    r = lax.rsqrt(jnp.mean(x * x, axis=-1, keepdims=True) + EPS)
    h = (x * r) * ng_ref[...]
    hb = (h * (1.0 + sc_ref[0]) + sh_ref[0]).astype(BF16)
    for c in range(0, FFN_HIDDEN, FFN_CHUNK):
        gate = _dot(hb, wg_ref[:, c:c + FFN_CHUNK])
        up = _dot(hb, wu_ref[:, c:c + FFN_CHUNK])
        act_ref[:, c:c + FFN_CHUNK] = (gate * _sigmoid(gate) * up).astype(BF16)
    x = x + g2_ref[0] * _dot(act_ref[...], wd_ref[...])
    if final:
        r = lax.rsqrt(jnp.mean(x * x, axis=-1, keepdims=True) + EPS)
        x = (x * r) * fg_ref[...]
    o_ref[...] = x


def _out_projection_ffn(x, ys, mods, mod_rows, norm_g, wo, wg, wu, wd, final_g=None):
    rows, d = x.shape
    tm = ROW_TILE
    final = final_g is not None
    row_spec = pl.BlockSpec((tm, d), lambda i: (i, 0))
    y_spec = pl.BlockSpec((tm, GROUP_W), lambda i: (i, 0))
    vec_spec = pl.BlockSpec((1, d), lambda i: (0, 0))
    in_specs = ([row_spec, y_spec, y_spec, y_spec, y_spec]
                + [_mod_spec(d, chunk, mod_rows, tm) for chunk in (2, 3, 4, 5)]
                + [vec_spec, _resident(wo.shape), _resident(wg.shape), _resident(wu.shape), _resident(wd.shape)])
    args = [x, *ys, mods, mods, mods, mods, norm_g.reshape(1, d), wo, wg, wu, wd]
    if final:
        in_specs.append(vec_spec)
        args.append(final_g.reshape(1, d))
    return pl.pallas_call(
        functools.partial(_ffn_kernel, final),
        grid=(rows // tm,),
        in_specs=in_specs,
        out_specs=row_spec,
        out_shape=jax.ShapeDtypeStruct((rows, d), F32),
        scratch_shapes=[pltpu.VMEM((tm, d), BF16), pltpu.VMEM((tm, FFN_HIDDEN), BF16)],
        compiler_params=_compiler_params(),
        name="out_projection_ffn",
    )(*args)


def _rope_tables(n_tok, dim):
    t = np.arange(n_tok)
    row = (t // GRID_W).astype(np.float32)
    col = (t % GRID_W).astype(np.float32)
    nf = dim // 4
    inv = (np.float32(ROPE_BASE) ** (-np.arange(nf, dtype=np.float32) / np.float32(nf))).astype(np.float32)
    ang = np.concatenate([row[:, None] * inv, col[:, None] * inv], axis=-1).astype(np.float64)
    reps = 128 // (dim // 2)
    return (jnp.asarray(np.tile(np.cos(ang), (1, reps)), F32), jnp.asarray(np.tile(np.sin(ang), (1, reps)), F32))


def kernel(x, c, ctx, c_ctx, norm1_g, norm2_g, w_ada, b_ada, w_in, w_out, attn_sink, conv_w, conv_b,
           conv_ln_g, conv_ln_b, diff_lq1, diff_lk1, diff_lq2, diff_lk2, diff_subln_g, na_rpb,
           w_gate, w_up, w_down, final_g):
    batch, seq, d = x.shape
    ctx_len = ctx.shape[1]
    depth = w_ada.shape[0]

    cvec = jnp.zeros((ADA_ROWS, d), F32).at[:batch].set(c).at[batch].set(c_ctx)
    mods_all = _ada_table(cvec, w_ada, b_ada).reshape(depth, ADA_ROWS * 6, 1, d)
    rope = _rope_tables(seq, HEAD_DIM) + _rope_tables(seq, C_QK_DIM)
    lat_rows = (0, seq)
    ctx_rows = (batch, batch * ctx_len)

    xl = x.reshape(batch * seq, d)
    xc = ctx.reshape(batch * ctx_len, d)
    for l in range(depth):
        ctx_needed = l < depth - 1
        mods = mods_all[l]
        lambda_init = 0.8 - 0.6 * math.exp(-0.3 * l)
        w_in_l = _relayout_w_in(w_in, l)
        lam_vecs = jnp.stack([diff_lq1[l], diff_lk1[l], diff_lq2[l], diff_lk2[l]]).astype(F32)
        subg = jnp.tile(diff_subln_g[l], N_HEADS).reshape(1, GROUP_W)
        cw = conv_w[l].reshape(CONV_K, GROUP_W)

        def per_batch(tensors, n_tok):
            return [t if i in (4, 6) else t.reshape(batch, n_tok, -1) for i, t in enumerate(tensors)]

        lat_out = _in_projection(xl, norm1_g[l], mods, lat_rows, w_in_l, rope,
                                 cast_weights=(w_out, w_gate, w_up, w_down), layer=l)
        qa, ka, va, ub, qct, kc, vct, qd, kd, vd = per_batch(lat_out[:10], seq)
        wo, wg, wu, wd = lat_out[10:]
        qa_c, ka_c, va_c, ub_c, qct_c, kc_c, vct_c, qd_c, kd_c, vd_c = per_batch(
            _in_projection(xc, norm1_g[l], mods, ctx_rows, w_in_l, None), ctx_len)

        y_a = _window_attention(attn_sink[l], qa, ka, va, ka_c, va_c)
        y_b = _conformer_conv(ub, cw, conv_b[l], conv_ln_g[l], conv_ln_b[l])
        y_c = _diff_attention(lam_vecs, subg, lambda_init, batch, qct, kc_c, vct_c, kc, vct)
        y_d = _neighbourhood_attention(_na_bias_table(na_rpb[l]), qd, kd, vd, kd_c, vd_c)
        ys = [t.reshape(batch * seq, GROUP_W) for t in (y_a, y_b, y_c, y_d)]
        xl = _out_projection_ffn(xl, ys, mods, lat_rows, norm2_g[l], wo, wg, wu, wd,
                                 final_g=None if ctx_needed else final_g)
        if ctx_needed:
            yc_a = _context_attention(qa_c, ka_c, va_c, sink=attn_sink[l], split_layout=True)
            yc_b = _conformer_conv(ub_c, cw, conv_b[l], conv_ln_g[l], conv_ln_b[l])
            yc_c = _diff_attention(lam_vecs, subg, lambda_init, batch, qct_c, kc_c, vct_c)
            yc_d = _context_attention(qd_c, kd_c, vd_c)
            ycs = [t.reshape(batch * ctx_len, GROUP_W) for t in (yc_a, yc_b, yc_c, yc_d)]
            xc = _out_projection_ffn(xc, ycs, mods, ctx_rows, norm2_g[l], wo, wg, wu, wd)
    return xl.reshape(batch, seq, d)
```

```python
import functools
import math

import numpy as np
import jax
import jax.numpy as jnp
from jax import lax
from jax.experimental import pallas as pl
from jax.experimental.pallas import tpu as pltpu

F32 = jnp.float32
BF16 = jnp.bfloat16

D_MODEL = 1024
DEPTH = 2
GRID_W = 64
HEAD_DIM = 64
GROUP_W = 256
N_HEADS = 4
A_KV_HEADS = 2
A_WINDOW = 128
BLK = 128
CONV_K = 31
C_QK_DIM = 32
NA_KH = 8
NA_KW = 16
FFN_HIDDEN = 2816
ROPE_BASE = 10000.0
EPS = 1e-6
NEG_INF = -1e30

VMEM_LIMIT = 56 * 1024 * 1024
BF16_SUBLANES = 16
ADA_ROWS = 16
ADA_TN = 1536
ROW_TILE = 512
FFN_CHUNK = 256
CONV_TC = 64
CONV_HALO = 16
DIFF_TQ = 512
LOG2E = math.log2(math.e)
ATTN_Q_SCALE = (HEAD_DIM ** -0.5) * LOG2E
DIFF_Q_SCALE = (C_QK_DIM ** -0.5) * LOG2E
DIFF_BLOCKS_PER_STEP = 2
DIFF_KEY_SEG = 1024


def _compiler_params():
    return pltpu.CompilerParams(vmem_limit_bytes=VMEM_LIMIT)


def _resident(shape):
    return pl.BlockSpec(shape, lambda *_: (0,) * len(shape), pipeline_mode=pl.Buffered(1))


def _sigmoid(v):
    return 1.0 / (1.0 + jnp.exp(-v))


def _dot(a, b):
    return jnp.dot(a, b, preferred_element_type=F32)


def _dot_nt(a, b):
    return lax.dot_general(a, b, (((1,), (1,)), ((), ())), preferred_element_type=F32)


def _ada_kernel(c_ref, w_ref, b_ref, o_ref):
    cv = c_ref[...]
    s = cv * _sigmoid(cv)
    o_ref[0] = _dot(s.astype(BF16), w_ref[0].astype(BF16)) + b_ref[0]


def _ada_table(cvec, w_ada, b_ada):
    depth, d, n = w_ada.shape
    return pl.pallas_call(
        _ada_kernel,
        grid=(depth, n // ADA_TN),
        in_specs=[
            pl.BlockSpec((ADA_ROWS, d), lambda l, j: (0, 0)),
            pl.BlockSpec((1, d, ADA_TN), lambda l, j: (l, 0, j)),
            pl.BlockSpec((1, 1, ADA_TN), lambda l, j: (l, 0, j)),
        ],
        out_specs=pl.BlockSpec((1, ADA_ROWS, ADA_TN), lambda l, j: (l, 0, j)),
        out_shape=jax.ShapeDtypeStruct((depth, ADA_ROWS, n), F32),
        compiler_params=_compiler_params(),
        name="ada_table",
    )(cvec, w_ada, b_ada.reshape(depth, 1, n))


def _selection_matrices():
    lane = np.arange(GROUP_W)
    part, h, j = lane // 128, (lane % 128) // 32, lane % 32
    grp, jc = (lane % 128) // 16, lane % 16
    sources = (
        (GROUP_W, h * HEAD_DIM + part * 32 + j),
        (A_KV_HEADS * HEAD_DIM, (h // 2) * HEAD_DIM + part * 32 + j),
        (A_KV_HEADS * HEAD_DIM, ((lane // HEAD_DIM) // 2) * HEAD_DIM + lane % HEAD_DIM),
        (GROUP_W, grp * C_QK_DIM + part * 16 + jc),
    )
    return [np.equal(np.arange(n_src)[:, None], src[None, :]).astype(np.float32) for n_src, src in sources]


def _relayout_kernel(w_ref, s_qa, s_ka, s_va, s_c, o_ref):
    def cols(a, b):
        return w_ref[0, :, a:b].astype(BF16)

    def select(a, b, s_ref):
        return _dot(cols(a, b), s_ref[...]).astype(BF16)

    o_ref[:, OFF_QA:OFF_KA] = select(0, 256, s_qa)
    o_ref[:, OFF_KA:OFF_VA] = select(256, 384, s_ka)
    o_ref[:, OFF_VA:OFF_UB] = select(384, 512, s_va)
    o_ref[:, OFF_UB:OFF_QC] = cols(512, 1024)
    o_ref[:, OFF_QC:OFF_KC] = select(1024, 1280, s_c)
    o_ref[:, OFF_KC:OFF_VC] = select(1280, 1536, s_c)
    o_ref[:, OFF_VC:PROJ_WIDTH] = cols(1536, 2560)


def _relayout_w_in(w_in, layer):
    _, d, n_in = w_in.shape
    rows = 256
    sel = [jnp.asarray(s, BF16) for s in _selection_matrices()]
    return pl.pallas_call(
        _relayout_kernel,
        grid=(d // rows,),
        in_specs=[pl.BlockSpec((1, rows, n_in), lambda i: (layer, i, 0))]
                 + [pl.BlockSpec(s.shape, lambda i: (0, 0)) for s in sel],
        out_specs=pl.BlockSpec((rows, PROJ_WIDTH), lambda i: (i, 0)),
        out_shape=jax.ShapeDtypeStruct((d, PROJ_WIDTH), BF16),
        compiler_params=_compiler_params(),
        name="relayout_w_in",
    )(w_in, *sel)


PROJ_WIDTH = 9 * GROUP_W + 512
OFF_QA, OFF_KA, OFF_VA, OFF_UB, OFF_QC, OFF_KC, OFF_VC, OFF_QD, OFF_KD, OFF_VD = (
    0, 256, 512, 768, 1280, 1536, 1792, 2048, 2304, 2560)


def _inproj_kernel(rope, n_cast, x_ref, g_ref, sh_ref, sc_ref, w_ref, *rest):
    if rope:
        ca_ref, sa_ref, cc_ref, sc2_ref = rest[:4]
        rest = rest[4:]
    cast_in, rest = rest[:n_cast], rest[n_cast:]
    qa_o, ka_o, va_o, ub_o, qc_o, kc_o, vc_o, qd_o, kd_o, vd_o = rest[:10]
    for src_ref, dst_ref in zip(cast_in, rest[10:]):
        dst_ref[...] = src_ref[0].astype(BF16)
    x = x_ref[...]
    r = lax.rsqrt(jnp.mean(x * x, axis=-1, keepdims=True) + EPS)
    h = (x * r) * g_ref[...]
    h = h * (1.0 + sc_ref[0]) + sh_ref[0]
    hb = h.astype(BF16)

    def proj(off, width=GROUP_W):
        return _dot(hb, w_ref[:, off:off + width])

    def store_rot(o_ref, y, c_ref, s_ref, scale, channel_major=False):
        if rope:
            x1, x2 = y[:, :128], y[:, 128:]
            cs, sn = c_ref[...], s_ref[...]
            y1, y2 = x1 * cs - x2 * sn, x1 * sn + x2 * cs
        else:
            y1, y2 = y[:, :128], y[:, 128:]
        if channel_major:
            o_ref[:128, :] = (y1 * scale).T.astype(BF16)
            o_ref[128:, :] = (y2 * scale).T.astype(BF16)
        else:
            o_ref[:, :128] = (y1 * scale).astype(BF16)
            o_ref[:, 128:] = (y2 * scale).astype(BF16)

    ca = sa = cc = sc2 = None
    if rope:
        ca, sa, cc, sc2 = ca_ref, sa_ref, cc_ref, sc2_ref
    store_rot(qa_o, proj(OFF_QA), ca, sa, ATTN_Q_SCALE)
    store_rot(ka_o, proj(OFF_KA), ca, sa, 1.0)
    va_o[...] = proj(OFF_VA).astype(BF16)
    ub_o[...] = proj(OFF_UB, 512)
    store_rot(qc_o, proj(OFF_QC), cc, sc2, DIFF_Q_SCALE, channel_major=True)
    store_rot(kc_o, proj(OFF_KC), cc, sc2, 1.0)
    vc_o[...] = proj(OFF_VC).T.astype(BF16)
    qd_o[...] = (proj(OFF_QD) * ATTN_Q_SCALE).astype(BF16)
    kd_o[...] = proj(OFF_KD).astype(BF16)
    vd_o[...] = proj(OFF_VD).astype(BF16)


def _mod_spec(d, chunk, mod_rows, tm, lag=0):
    first_row, rows_per_mod = mod_rows
    tiles_per_mod = rows_per_mod // tm
    return pl.BlockSpec(
        (1, 1, d), lambda i: ((first_row + jnp.maximum(i - lag, 0) // tiles_per_mod) * 6 + chunk, 0, 0))


def _in_projection(x, g, mods, mod_rows, w, rope_tables, cast_weights=(), layer=0):
    rows, d = x.shape
    tm = ROW_TILE
    steps = rows // tm
    rope = rope_tables is not None
    in_specs = [
        pl.BlockSpec((tm, d), lambda i: (i, 0)),
        pl.BlockSpec((1, d), lambda i: (0, 0)),
        _mod_spec(d, 0, mod_rows, tm), _mod_spec(d, 1, mod_rows, tm),
        _resident((d, PROJ_WIDTH)),
    ]
    args = [x, g.reshape(1, d), mods, mods, w]
    if rope:
        tiles_per_seq = rope_tables[0].shape[0] // tm
        tab_spec = pl.BlockSpec((tm, 128), lambda i: (i % tiles_per_seq, 0))
        in_specs += [tab_spec] * 4
        args += list(rope_tables)
    narrow = pl.BlockSpec((tm, GROUP_W), lambda i: (i, 0))
    wide = pl.BlockSpec((tm, 512), lambda i: (i, 0))
    chan_major = pl.BlockSpec((GROUP_W, tm), lambda i: (0, i))
    out_specs = [narrow, narrow, narrow, wide, chan_major, narrow, chan_major, narrow, narrow, narrow]
    bf = jax.ShapeDtypeStruct((rows, GROUP_W), BF16)
    bf_t = jax.ShapeDtypeStruct((GROUP_W, rows), BF16)
    out_shape = [bf, bf, bf, jax.ShapeDtypeStruct((rows, 512), F32), bf_t, bf, bf_t, bf, bf, bf]
    for cw in cast_weights:
        _, r_w, c_w = cw.shape
        blk_rows = next(n for n in range(BF16_SUBLANES, r_w + 1, BF16_SUBLANES)
                        if r_w % n == 0 and steps % (r_w // n) == 0 and r_w // n <= steps)
        steps_per_blk = steps // (r_w // blk_rows)
        in_specs.append(pl.BlockSpec((1, blk_rows, c_w), lambda i, s=steps_per_blk: (layer, i // s, 0)))
        args.append(cw)
        out_specs.append(pl.BlockSpec((blk_rows, c_w), lambda i, s=steps_per_blk: (i // s, 0)))
        out_shape.append(jax.ShapeDtypeStruct((r_w, c_w), BF16))
    return pl.pallas_call(
        functools.partial(_inproj_kernel, rope, len(cast_weights)),
        grid=(steps,),
        in_specs=in_specs,
        out_specs=out_specs,
        out_shape=out_shape,
        compiler_params=_compiler_params(),
        name="in_projection",
    )(*args)


def _lane_iota():
    return lax.broadcasted_iota(jnp.int32, (1, GROUP_W), 1)


def _split_half_head(lane):
    return jnp.right_shift(jnp.bitwise_and(lane, 127), 5)


def _natural_head(lane):
    return jnp.right_shift(lane, 6)


def _stack_heads(q, lane_group, groups):
    zero = jnp.zeros_like(q)
    return jnp.concatenate([jnp.where(lane_group == g, q, zero) for g in groups], axis=0)


def _softmax_pv(scores, values, sink_col=None):
    m = functools.reduce(jnp.maximum, [jnp.max(s, axis=-1, keepdims=True) for s in scores])
    if sink_col is not None:
        m = jnp.maximum(m, sink_col)
    es = [jnp.exp2(s - m) for s in scores]
    l = functools.reduce(jnp.add, [jnp.sum(e, axis=-1, keepdims=True) for e in es])
    if sink_col is not None:
        l = l + jnp.exp2(sink_col - m)
    o = functools.reduce(jnp.add, [_dot(e.astype(BF16), v) for e, v in zip(es, values)])
    return o / l


def _select_heads(o, tq):
    head = _natural_head(_lane_iota())
    out = o[0:tq]
    for h in range(1, N_HEADS):
        out = jnp.where(head == h, o[h * tq:(h + 1) * tq], out)
    return out


def _sink_column(sink_ref, tq):
    row = lax.broadcasted_iota(jnp.int32, (N_HEADS * tq, 1), 0)
    col = jnp.full((N_HEADS * tq, 1), sink_ref[N_HEADS - 1], F32)
    for h in range(N_HEADS - 2, -1, -1):
        col = jnp.where(row < (h + 1) * tq, sink_ref[h], col)
    return col * LOG2E


def _win_attn_kernel(sink_ref, q_ref, kl_ref, vl_ref, kc_ref, vc_ref, o_ref):
    seq = q_ref.shape[1]
    lane_head = _split_half_head(_lane_iota())
    sink_col = _sink_column(sink_ref, BLK)
    shape = (BLK, 3 * BLK)
    rel = lax.broadcasted_iota(jnp.int32, shape, 1) - lax.broadcasted_iota(jnp.int32, shape, 0)
    bands = {}
    for n in range(seq // BLK):
        start = min(max((n - 1) * BLK, 0), seq - 3 * BLK)
        shift = start - n * BLK
        if shift not in bands:
            bands[shift] = jnp.where(jnp.abs(rel + shift) <= A_WINDOW, 0.0, NEG_INF)
        qs = _stack_heads(q_ref[0, n * BLK:(n + 1) * BLK, :], lane_head, range(N_HEADS))
        kl = kl_ref[0, start:start + 3 * BLK, :]
        vl = vl_ref[0, start:start + 3 * BLK, :]
        s_loc = _dot_nt(qs, kl)
        s_ctx = _dot_nt(qs, kc_ref[0])
        s_loc = (s_loc.reshape(N_HEADS, BLK, 3 * BLK) + bands[shift][None]).reshape(N_HEADS * BLK, 3 * BLK)
        o = _softmax_pv([s_loc, s_ctx], [vl, vc_ref[0]], sink_col)
        o_ref[0, n * BLK:(n + 1) * BLK, :] = _select_heads(o, BLK).astype(BF16)


def _window_attention(sink, q, k, v, kc, vc):
    b, seq, w = q.shape
    ctx_len = kc.shape[1]
    whole = pl.BlockSpec((1, seq, w), lambda i: (i, 0, 0))
    ctx_spec = pl.BlockSpec((1, ctx_len, w), lambda i: (i, 0, 0))
    return pl.pallas_call(
        _win_attn_kernel,
        grid=(b,),
        in_specs=[pl.BlockSpec(memory_space=pltpu.SMEM), whole, whole, whole, ctx_spec, ctx_spec],
        out_specs=whole,
        out_shape=jax.ShapeDtypeStruct((b, seq, w), BF16),
        compiler_params=_compiler_params(),
        name="window_attention",
    )(sink, q, k, v, kc, vc)


def _na_bias_table(rpb):
    cq = np.arange(GRID_W)
    cs = np.clip(cq - NA_KW // 2, 0, GRID_W - NA_KW)
    col_valid = (cq[None, :] >= cs[:, None]) & (cq[None, :] < cs[:, None] + NA_KW)
    dc = np.clip(cq[None, :] - cq[:, None], -(NA_KW - 1), NA_KW - 1) + (NA_KW - 1)
    n_dr, n_dc = 2 * NA_KH - 1, 2 * NA_KW - 1
    onehot = (dc.reshape(1, -1) == np.arange(n_dc)[:, None]).astype(np.float32)
    t = jnp.dot(rpb.astype(F32).reshape(N_HEADS * n_dr, n_dc), onehot, precision=lax.Precision.HIGHEST)
    t = jnp.where(col_valid[None, None], t.reshape(N_HEADS, n_dr, GRID_W, GRID_W) * LOG2E, NEG_INF)
    t = jnp.transpose(t, (1, 0, 2, 3)).reshape(n_dr, N_HEADS * GRID_W, GRID_W)
    return jnp.concatenate([t[:-1], t[1:]], axis=-1)


def _na_attn_kernel(q_ref, kl_ref, vl_ref, kc_ref, vc_ref, bias_ref, o_ref):
    rows = q_ref.shape[1] // GRID_W
    lane_head = _natural_head(_lane_iota())
    for r in range(rows):
        first = min(max(r - NA_KH // 2, 0), rows - NA_KH)
        start = first * GRID_W
        off = first - r + NA_KH - 1
        qs = _stack_heads(q_ref[0, r * GRID_W:(r + 1) * GRID_W, :], lane_head, range(N_HEADS))
        kl = kl_ref[0, start:start + NA_KH * GRID_W, :]
        vl = vl_ref[0, start:start + NA_KH * GRID_W, :]
        bias = jnp.concatenate([bias_ref[off + 2 * i] for i in range(NA_KH // 2)], axis=1)
        s_loc = _dot_nt(qs, kl) + bias
        s_ctx = _dot_nt(qs, kc_ref[0])
        o = _softmax_pv([s_loc, s_ctx], [vl, vc_ref[0]])
        o_ref[0, r * GRID_W:(r + 1) * GRID_W, :] = _select_heads(o, GRID_W).astype(BF16)


def _neighbourhood_attention(bias, q, k, v, kc, vc):
    b, seq, w = q.shape
    ctx_len = kc.shape[1]
    whole = pl.BlockSpec((1, seq, w), lambda i: (i, 0, 0))
    ctx_spec = pl.BlockSpec((1, ctx_len, w), lambda i: (i, 0, 0))
    return pl.pallas_call(
        _na_attn_kernel,
        grid=(b,),
        in_specs=[whole, whole, whole, ctx_spec, ctx_spec, _resident(bias.shape)],
        out_specs=whole,
        out_shape=jax.ShapeDtypeStruct((b, seq, w), BF16),
        compiler_params=_compiler_params(),
        name="neighbourhood_attention",
    )(q, k, v, kc, vc, bias)


def _ctx_attn_kernel(split_layout, has_sink, *refs):
    if has_sink:
        sink_ref, q_ref, k_ref, v_ref, o_ref = refs
    else:
        q_ref, k_ref, v_ref, o_ref = refs
    tq = q_ref.shape[1]
    lane = _lane_iota()
    group = _split_half_head(lane) if split_layout else _natural_head(lane)
    qs = _stack_heads(q_ref[0], group, range(N_HEADS))
    s = _dot_nt(qs, k_ref[0])
    sink_col = _sink_column(sink_ref, tq) if has_sink else None
    o = _softmax_pv([s], [v_ref[0]], sink_col)
    o_ref[0] = _select_heads(o, tq).astype(BF16)


def _context_attention(q, k, v, sink=None, split_layout=False):
    b, n, w = q.shape
    spec = pl.BlockSpec((1, n, w), lambda i: (i, 0, 0))
    in_specs = [spec, spec, spec]
    args = [q, k, v]
    if sink is not None:
        in_specs = [pl.BlockSpec(memory_space=pltpu.SMEM)] + in_specs
        args = [sink] + args
    return pl.pallas_call(
        functools.partial(_ctx_attn_kernel, split_layout, sink is not None),
        grid=(b,),
        in_specs=in_specs,
        out_specs=spec,
        out_shape=jax.ShapeDtypeStruct((b, n, w), BF16),
        compiler_params=_compiler_params(),
        name="context_attention",
    )(*args)


def _diff_attn_kernel(has_local, lambda_init, lam_ref, subg_ref, qt_ref, *refs):
    if has_local:
        kl_ref, vtl_ref, kc_ref, vtc_ref, o_ref = refs
    else:
        kc_ref, vtc_ref, o_ref = refs
    tq = min(DIFF_TQ, qt_ref.shape[1])
    n_blocks = qt_ref.shape[1] // tq
    lv = lam_ref[...]
    lam = (jnp.exp(jnp.sum(lv[0:1] * lv[1:2], axis=-1, keepdims=True))
           - jnp.exp(jnp.sum(lv[2:3] * lv[3:4], axis=-1, keepdims=True)) + lambda_init)
    chan = lax.broadcasted_iota(jnp.int32, (GROUP_W, 1), 0)
    group = jnp.right_shift(jnp.bitwise_and(chan, 127), 4)
    zero = jnp.zeros((GROUP_W, tq), BF16)
    segs = [(kc_ref, vtc_ref, 0, kc_ref.shape[1])]
    if has_local:
        n_loc = kl_ref.shape[1]
        segs = [(kl_ref, vtl_ref, k0, DIFF_KEY_SEG) for k0 in range(0, n_loc, DIFF_KEY_SEG)] + segs
    sum_rows = 16

    def head_scores(unit):
        blk, h = unit
        qt = qt_ref[:, blk * tq:(blk + 1) * tq]
        qs = jnp.concatenate([jnp.where(group == 2 * h, qt, zero),
                              jnp.where(group == 2 * h + 1, qt, zero)], axis=1)
        return [_dot(k_ref[0, k0:k0 + nk, :], qs) for k_ref, _, k0, nk in segs]

    def column_max(s, slab=64):
        part = jnp.max(s.reshape(s.shape[0] // slab, slab, s.shape[1]), axis=0)
        return jnp.max(part, axis=0, keepdims=True)

    ys = []
    units = [(blk, h) for blk in range(n_blocks) for h in range(N_HEADS)]
    nxt = head_scores(units[0])
    for ui, (blk, h) in enumerate(units):
        scores = nxt
        if ui + 1 < len(units):
            nxt = head_scores(units[ui + 1])
        m = functools.reduce(jnp.maximum, [column_max(s) for s in scores])
        pv = None
        for s, (_, vt_ref, k0, nk) in zip(scores, segs):
            e = jnp.exp2(s - m).astype(BF16)
            vt = vt_ref[h * HEAD_DIM:(h + 1) * HEAD_DIM, k0:k0 + nk]
            lhs = jnp.concatenate([vt, jnp.ones((sum_rows, nk), BF16)], axis=0)
            p = _dot(lhs, e)
            pv = p if pv is None else pv + p
        inv = 1.0 / pv[HEAD_DIM:HEAD_DIM + 1]
        pv = pv[:HEAD_DIM]
        o = pv[:, :tq] * inv[:, :tq] - pv[:, tq:] * (lam * inv[:, tq:])
        ms = jnp.mean(o * o, axis=0, keepdims=True)
        ys.append(o * lax.rsqrt(ms + EPS))
        if h == N_HEADS - 1:
            out = jnp.concatenate(ys, axis=0).T
            o_ref[0, blk * tq:(blk + 1) * tq, :] = (out * subg_ref[...] * (1.0 - lambda_init)).astype(BF16)
            ys = []


def _diff_attention(lam_vecs, subg, lambda_init, b, qt, kc, vtc, k=None, vt=None):
    w = qt.shape[0]
    n = qt.shape[1] // b
    ctx_len = kc.shape[1]
    has_local = k is not None
    tq = min(DIFF_TQ * DIFF_BLOCKS_PER_STEP, n)
    steps = n // tq
    q_spec = pl.BlockSpec((w, tq), lambda i, j: (0, i * steps + j))
    in_specs = [pl.BlockSpec(lam_vecs.shape, lambda i, j: (0, 0)),
                pl.BlockSpec((1, w), lambda i, j: (0, 0)), q_spec]
    args = [lam_vecs, subg, qt]
    if has_local:
        in_specs += [pl.BlockSpec((1, n, w), lambda i, j: (i, 0, 0)),
                     pl.BlockSpec((w, n), lambda i, j: (0, i))]
        args += [k, vt]
    in_specs += [pl.BlockSpec((1, ctx_len, w), lambda i, j: (i, 0, 0)),
                 pl.BlockSpec((w, ctx_len), lambda i, j: (0, i))]
    args += [kc, vtc]
    return pl.pallas_call(
        functools.partial(_diff_attn_kernel, has_local, lambda_init),
        grid=(b, n // tq),
        in_specs=in_specs,
        out_specs=pl.BlockSpec((1, tq, w), lambda i, j: (i, j, 0)),
        out_shape=jax.ShapeDtypeStruct((b, n, w), BF16),
        compiler_params=_compiler_params(),
        name="diff_attention",
    )(*args)


CONV_WIN = CONV_TC + 2 * CONV_HALO


def _glu(u):
    return u[:, :GROUP_W] * _sigmoid(u[:, GROUP_W:])


def _conv_chunk(win, w_ref, b_ref, g_ref, beta_ref):
    halo, tc = CONV_HALO, CONV_TC
    acc = jnp.zeros((tc, win.shape[1]), F32)
    for sub in range(8):
        shifted = win if sub == 0 else pltpu.roll(win, CONV_WIN - sub, axis=0)
        for blk8 in range(CONV_WIN // 8):
            tap = 8 * blk8 + sub - (halo - CONV_K // 2)
            if 0 <= tap < CONV_K:
                acc = acc + shifted[8 * blk8:8 * blk8 + tc] * w_ref[tap:tap + 1, :]
    hcv = acc + b_ref[...]
    mu = jnp.mean(hcv, axis=-1, keepdims=True)
    cen = hcv - mu
    var = jnp.mean(cen * cen, axis=-1, keepdims=True)
    y = cen * lax.rsqrt(var + EPS) * g_ref[...] + beta_ref[...]
    return y * _sigmoid(y)


def _conv_kernel(seq, u_ref, w_ref, b_ref, g_ref, beta_ref, o_ref, pad_ref):
    halo, tc, ch = CONV_HALO, CONV_TC, GROUP_W
    zeros = jnp.zeros((halo, ch), F32)
    pad_ref[0:halo, :] = zeros
    pad_ref[halo + seq:2 * halo + seq, :] = zeros

    def glu(i, carry):
        r0 = pl.multiple_of(i * tc, tc)
        pad_ref[pl.ds(halo + r0, tc), :] = _glu(u_ref[0, pl.ds(r0, tc), :])
        return carry

    lax.fori_loop(0, seq // tc, glu, 0)

    def chunk(i, carry):
        c0 = pl.multiple_of(i * tc, tc)
        win = pad_ref[pl.ds(c0, CONV_WIN), :]
        o_ref[0, pl.ds(c0, tc), :] = _conv_chunk(win, w_ref, b_ref, g_ref, beta_ref).astype(BF16)
        return carry

    lax.fori_loop(0, seq // tc, chunk, 0, unroll=4)


def _conformer_conv(u, w, bias, ln_g, ln_b):
    b, seq, two_ch = u.shape
    ch = two_ch // 2
    vec = pl.BlockSpec((1, ch), lambda i: (0, 0))
    return pl.pallas_call(
        functools.partial(_conv_kernel, seq),
        grid=(b,),
        in_specs=[pl.BlockSpec((1, seq, two_ch), lambda i: (i, 0, 0)),
                  pl.BlockSpec((CONV_K, ch), lambda i: (0, 0)), vec, vec, vec],
        out_specs=pl.BlockSpec((1, seq, ch), lambda i: (i, 0, 0)),
        out_shape=jax.ShapeDtypeStruct((b, seq, ch), BF16),
        scratch_shapes=[pltpu.VMEM((seq + 2 * CONV_HALO, ch), F32)],
        compiler_params=_compiler_params(),
        name="conformer_conv",
    )(u, w, bias.reshape(1, ch), ln_g.reshape(1, ch), ln_b.reshape(1, ch))


def _fused_conv_glu(tiles_per_seq, u_ref, up_ref, un_ref, pad_ref):
    halo, tm = CONV_HALO, u_ref.shape[0]
    n_tiles = pl.num_programs(0) - 1
    pos = lax.rem(jnp.minimum(pl.program_id(0), n_tiles - 1), tiles_per_seq)
    pad_ref[0:halo, :] = jnp.where(pos > 0, _glu(up_ref[...]), 0.0)
    pad_ref[halo:halo + tm, :] = _glu(u_ref[...])
    pad_ref[halo + tm:2 * halo + tm, :] = jnp.where(pos < tiles_per_seq - 1, _glu(un_ref[...]), 0.0)


def _ffn_kernel(final, conv_tiles_per_seq, x_ref, ya_ref, *rest):
    if conv_tiles_per_seq is None:
        yb_ref, rest = rest[0], rest[1:]
    else:
        conv_refs, rest = rest[:7], rest[7:]
    (yc_ref, yd_ref, g1_ref, sh_ref, sc_ref, g2_ref, ng_ref, wo_ref, wg_ref, wu_ref, wd_ref), rest = rest[:11], rest[11:]
    if final:
        fg_ref, rest = rest[0], rest[1:]
    o_ref, ycat_ref, act_ref = rest[:3]
    if conv_tiles_per_seq is not None:
        yb_ref, pad_ref = rest[3:]

        @pl.when(pl.program_id(0) == 0)
        def _():
            yb_ref[...] = jnp.zeros(yb_ref.shape, BF16)

    for j, y_ref in enumerate((ya_ref, yb_ref, yc_ref, yd_ref)):
        ycat_ref[:, j * GROUP_W:(j + 1) * GROUP_W] = y_ref[...]
    conv_chunks = []
    if conv_tiles_per_seq is not None:
        u_ref, up_ref, un_ref, cw_ref, cb_ref, cg_ref, cbeta_ref = conv_refs
        _fused_conv_glu(conv_tiles_per_seq, u_ref, up_ref, un_ref, pad_ref)
        conv_chunks = list(range(0, u_ref.shape[0], CONV_TC))
    x = x_ref[...] + g1_ref[0] * _dot(ycat_ref[...], wo_ref[...])
    r = lax.rsqrt(jnp.mean(x * x, axis=-1, keepdims=True) + EPS)
    h = (x * r) * ng_ref[...]
    hb = (h * (1.0 + sc_ref[0]) + sh_ref[0]).astype(BF16)
    for c in range(0, FFN_HIDDEN, FFN_CHUNK):
        gate = _dot(hb, wg_ref[:, c:c + FFN_CHUNK])
        up = _dot(hb, wu_ref[:, c:c + FFN_CHUNK])
        act_ref[:, c:c + FFN_CHUNK] = (gate * _sigmoid(gate) * up).astype(BF16)
        if conv_chunks:
            c0 = conv_chunks.pop(0)
            win = pad_ref[c0:c0 + CONV_WIN, :]
            yconv = _conv_chunk(win, cw_ref, cb_ref, cg_ref, cbeta_ref)
            yb_ref[c0:c0 + CONV_TC, :] = yconv.astype(BF16)
            anchor = (slice(0, BF16_SUBLANES), slice(c, c + 128))
            act_ref[anchor] = (act_ref[anchor].astype(F32) + 0.0 * yconv[:BF16_SUBLANES, :128]).astype(BF16)
    assert not conv_chunks
    x = x + g2_ref[0] * _dot(act_ref[...], wd_ref[...])
    if final:
        r = lax.rsqrt(jnp.mean(x * x, axis=-1, keepdims=True) + EPS)
        x = (x * r) * fg_ref[...]
    o_ref[...] = x


def _out_projection_ffn(x, ys, mods, mod_rows, norm_g, wo, wg, wu, wd, final_g=None, conv=None):
    rows, d = x.shape
    tm = ROW_TILE
    n_tiles = rows // tm
    final = final_g is not None
    lag = 0 if conv is None else 1

    def tile(i):
        return jnp.maximum(i - lag, 0)

    row_spec = pl.BlockSpec((tm, d), lambda i: (tile(i), 0))
    y_spec = pl.BlockSpec((tm, GROUP_W), lambda i: (tile(i), 0))
    vec_spec = pl.BlockSpec((1, d), lambda i: (0, 0))
    scratch = [pltpu.VMEM((tm, d), BF16), pltpu.VMEM((tm, FFN_HIDDEN), BF16)]
    if conv is None:
        yb_specs, yb_args, tiles_per_seq = [y_spec], [ys[1]], None
    else:
        u, cw, cb, cg, cbeta, seq = conv
        ch = cw.shape[1]
        halo_blocks = tm // CONV_HALO

        def conv_tile(i):
            return jnp.minimum(i, n_tiles - 1)

        cvec = pl.BlockSpec((1, ch), lambda i: (0, 0))
        yb_specs = [
            pl.BlockSpec((tm, 2 * ch), lambda i: (conv_tile(i), 0)),
            pl.BlockSpec((CONV_HALO, 2 * ch), lambda i: (jnp.maximum(conv_tile(i) * halo_blocks - 1, 0), 0)),
            pl.BlockSpec((CONV_HALO, 2 * ch),
                         lambda i: (jnp.minimum((conv_tile(i) + 1) * halo_blocks, rows // CONV_HALO - 1), 0)),
            pl.BlockSpec(cw.shape, lambda i: (0, 0)), cvec, cvec, cvec]
        yb_args = [u, u, u, cw, cb.reshape(1, ch), cg.reshape(1, ch), cbeta.reshape(1, ch)]
        tiles_per_seq = seq // tm
        scratch += [pltpu.VMEM((tm, ch), BF16), pltpu.VMEM((tm + 2 * CONV_HALO, ch), F32)]
    in_specs = ([row_spec, y_spec] + yb_specs + [y_spec, y_spec]
                + [_mod_spec(d, chunk, mod_rows, tm, lag) for chunk in (2, 3, 4, 5)]
                + [vec_spec, _resident(wo.shape), _resident(wg.shape), _resident(wu.shape), _resident(wd.shape)])
    args = [x, ys[0], *yb_args, ys[2], ys[3], mods, mods, mods, mods, norm_g.reshape(1, d), wo, wg, wu, wd]
    if final:
        in_specs.append(vec_spec)
        args.append(final_g.reshape(1, d))
    return pl.pallas_call(
        functools.partial(_ffn_kernel, final, tiles_per_seq),
        grid=(n_tiles + lag,),
        in_specs=in_specs,
        out_specs=row_spec,
        out_shape=jax.ShapeDtypeStruct((rows, d), F32),
        scratch_shapes=scratch,
        compiler_params=_compiler_params(),
        name="out_projection_ffn",
    )(*args)


def _rope_tables(n_tok, dim):
    t = np.arange(n_tok)
    row = (t // GRID_W).astype(np.float32)
    col = (t % GRID_W).astype(np.float32)
    nf = dim // 4
    inv = (np.float32(ROPE_BASE) ** (-np.arange(nf, dtype=np.float32) / np.float32(nf))).astype(np.float32)
    ang = np.concatenate([row[:, None] * inv, col[:, None] * inv], axis=-1).astype(np.float64)
    reps = 128 // (dim // 2)
    return (jnp.asarray(np.tile(np.cos(ang), (1, reps)), F32), jnp.asarray(np.tile(np.sin(ang), (1, reps)), F32))


def kernel(x, c, ctx, c_ctx, norm1_g, norm2_g, w_ada, b_ada, w_in, w_out, attn_sink, conv_w, conv_b,
           conv_ln_g, conv_ln_b, diff_lq1, diff_lk1, diff_lq2, diff_lk2, diff_subln_g, na_rpb,
           w_gate, w_up, w_down, final_g):
    batch, seq, d = x.shape
    ctx_len = ctx.shape[1]
    depth = w_ada.shape[0]

    cvec = jnp.zeros((ADA_ROWS, d), F32).at[:batch].set(c).at[batch].set(c_ctx)
    mods_all = _ada_table(cvec, w_ada, b_ada).reshape(depth, ADA_ROWS * 6, 1, d)
    rope = _rope_tables(seq, HEAD_DIM) + _rope_tables(seq, C_QK_DIM)
    lat_rows = (0, seq)
    ctx_rows = (batch, batch * ctx_len)

    xl = x.reshape(batch * seq, d)
    xc = ctx.reshape(batch * ctx_len, d)
    for l in range(depth):
        ctx_needed = l < depth - 1
        mods = mods_all[l]
        lambda_init = 0.8 - 0.6 * math.exp(-0.3 * l)
        w_in_l = _relayout_w_in(w_in, l)
        lam_vecs = jnp.stack([diff_lq1[l], diff_lk1[l], diff_lq2[l], diff_lk2[l]]).astype(F32)
        subg = jnp.tile(diff_subln_g[l], N_HEADS).reshape(1, GROUP_W)
        cw = conv_w[l].reshape(CONV_K, GROUP_W)

        def per_batch(tensors, n_tok):
            return [t if i in (4, 6) else t.reshape(batch, n_tok, -1) for i, t in enumerate(tensors)]

        lat_out = _in_projection(xl, norm1_g[l], mods, lat_rows, w_in_l, rope,
                                 cast_weights=(w_out, w_gate, w_up, w_down), layer=l)
        qa, ka, va, ub, qct, kc, vct, qd, kd, vd = per_batch(lat_out[:10], seq)
        wo, wg, wu, wd = lat_out[10:]
        qa_c, ka_c, va_c, ub_c, qct_c, kc_c, vct_c, qd_c, kd_c, vd_c = per_batch(
            _in_projection(xc, norm1_g[l], mods, ctx_rows, w_in_l, None), ctx_len)

        y_a = _window_attention(attn_sink[l], qa, ka, va, ka_c, va_c)
        y_c = _diff_attention(lam_vecs, subg, lambda_init, batch, qct, kc_c, vct_c, kc, vct)
        y_d = _neighbourhood_attention(_na_bias_table(na_rpb[l]), qd, kd, vd, kd_c, vd_c)
        ys = [None if t is None else t.reshape(batch * seq, GROUP_W) for t in (y_a, None, y_c, y_d)]
        conv = (ub.reshape(batch * seq, 2 * GROUP_W), cw, conv_b[l], conv_ln_g[l], conv_ln_b[l], seq)
        xl = _out_projection_ffn(xl, ys, mods, lat_rows, norm2_g[l], wo, wg, wu, wd,
                                 final_g=None if ctx_needed else final_g, conv=conv)
        if ctx_needed:
            yc_a = _context_attention(qa_c, ka_c, va_c, sink=attn_sink[l], split_layout=True)
            yc_b = _conformer_conv(ub_c, cw, conv_b[l], conv_ln_g[l], conv_ln_b[l])
            yc_c = _diff_attention(lam_vecs, subg, lambda_init, batch, qct_c, kc_c, vct_c)
            yc_d = _context_attention(qd_c, kd_c, vd_c)
            ycs = [t.reshape(batch * ctx_len, GROUP_W) for t in (yc_a, yc_b, yc_c, yc_d)]
            xc = _out_projection_ffn(xc, ycs, mods, ctx_rows, norm2_g[l], wo, wg, wu, wd)
    return xl.reshape(batch, seq, d)
```

```python
import functools
import math

import numpy as np
import jax
import jax.numpy as jnp
from jax import lax
from jax.experimental import pallas as pl
from jax.experimental.pallas import tpu as pltpu

F32 = jnp.float32
BF16 = jnp.bfloat16

D_MODEL = 1024
DEPTH = 2
GRID_W = 64
HEAD_DIM = 64
GROUP_W = 256
N_HEADS = 4
A_KV_HEADS = 2
A_WINDOW = 128
BLK = 128
CONV_K = 31
C_QK_DIM = 32
NA_KH = 8
NA_KW = 16
FFN_HIDDEN = 2816
ROPE_BASE = 10000.0
EPS = 1e-6
NEG_INF = -1e30

VMEM_LIMIT = 56 * 1024 * 1024
BF16_SUBLANES = 16
ADA_ROWS = 16
ADA_TN = 1536
ROW_TILE = 512
PROJ_ROW_TILE = 1024
FFN_CHUNK = 256
CONV_TC = 64
CONV_HALO = 16
DIFF_TQ = 512
LOG2E = math.log2(math.e)
ATTN_Q_SCALE = (HEAD_DIM ** -0.5) * LOG2E
DIFF_Q_SCALE = (C_QK_DIM ** -0.5) * LOG2E
DIFF_BLOCKS_PER_STEP = 2
DIFF_KEY_SEG = 1024


def _compiler_params():
    return pltpu.CompilerParams(vmem_limit_bytes=VMEM_LIMIT)


def _resident(shape):
    return pl.BlockSpec(shape, lambda *_: (0,) * len(shape), pipeline_mode=pl.Buffered(1))


def _sigmoid(v):
    return 1.0 / (1.0 + jnp.exp(-v))


def _dot(a, b):
    return jnp.dot(a, b, preferred_element_type=F32)


def _dot_nt(a, b):
    return lax.dot_general(a, b, (((1,), (1,)), ((), ())), preferred_element_type=F32)


def _ada_kernel(c_ref, w_ref, b_ref, o_ref):
    cv = c_ref[...]
    s = cv * _sigmoid(cv)
    o_ref[0] = _dot(s.astype(BF16), w_ref[0].astype(BF16)) + b_ref[0]


def _ada_table(cvec, w_ada, b_ada):
    depth, d, n = w_ada.shape
    return pl.pallas_call(
        _ada_kernel,
        grid=(depth, n // ADA_TN),
        in_specs=[
            pl.BlockSpec((ADA_ROWS, d), lambda l, j: (0, 0)),
            pl.BlockSpec((1, d, ADA_TN), lambda l, j: (l, 0, j)),
            pl.BlockSpec((1, 1, ADA_TN), lambda l, j: (l, 0, j)),
        ],
        out_specs=pl.BlockSpec((1, ADA_ROWS, ADA_TN), lambda l, j: (l, 0, j)),
        out_shape=jax.ShapeDtypeStruct((depth, ADA_ROWS, n), F32),
        compiler_params=_compiler_params(),
        name="ada_table",
    )(cvec, w_ada, b_ada.reshape(depth, 1, n))


def _selection_matrices():
    lane = np.arange(GROUP_W)
    part, h, j = lane // 128, (lane % 128) // 32, lane % 32
    grp, jc = (lane % 128) // 16, lane % 16
    sources = (
        (GROUP_W, h * HEAD_DIM + part * 32 + j),
        (A_KV_HEADS * HEAD_DIM, (h // 2) * HEAD_DIM + part * 32 + j),
        (A_KV_HEADS * HEAD_DIM, ((lane // HEAD_DIM) // 2) * HEAD_DIM + lane % HEAD_DIM),
        (GROUP_W, grp * C_QK_DIM + part * 16 + jc),
    )
    return [np.equal(np.arange(n_src)[:, None], src[None, :]).astype(np.float32) for n_src, src in sources]


def _relayout_kernel(w_ref, s_qa, s_ka, s_va, s_c, o_ref):
    def cols(a, b):
        return w_ref[0, :, a:b].astype(BF16)

    def select(a, b, s_ref):
        return _dot(cols(a, b), s_ref[...]).astype(BF16)

    o_ref[:, OFF_QA:OFF_KA] = select(0, 256, s_qa)
    o_ref[:, OFF_KA:OFF_VA] = select(256, 384, s_ka)
    o_ref[:, OFF_VA:OFF_UB] = select(384, 512, s_va)
    o_ref[:, OFF_UB:OFF_QC] = cols(512, 1024)
    o_ref[:, OFF_QC:OFF_KC] = select(1024, 1280, s_c)
    o_ref[:, OFF_KC:OFF_VC] = select(1280, 1536, s_c)
    o_ref[:, OFF_VC:PROJ_WIDTH] = cols(1536, 2560)


def _relayout_w_in(w_in, layer):
    _, d, n_in = w_in.shape
    rows = 256
    sel = [jnp.asarray(s, BF16) for s in _selection_matrices()]
    return pl.pallas_call(
        _relayout_kernel,
        grid=(d // rows,),
        in_specs=[pl.BlockSpec((1, rows, n_in), lambda i: (layer, i, 0))]
                 + [pl.BlockSpec(s.shape, lambda i: (0, 0)) for s in sel],
        out_specs=pl.BlockSpec((rows, PROJ_WIDTH), lambda i: (i, 0)),
        out_shape=jax.ShapeDtypeStruct((d, PROJ_WIDTH), BF16),
        compiler_params=_compiler_params(),
        name="relayout_w_in",
    )(w_in, *sel)


PROJ_WIDTH = 9 * GROUP_W + 512
OFF_QA, OFF_KA, OFF_VA, OFF_UB, OFF_QC, OFF_KC, OFF_VC, OFF_QD, OFF_KD, OFF_VD = (
    0, 256, 512, 768, 1280, 1536, 1792, 2048, 2304, 2560)


def _inproj_kernel(rope, n_cast, x_ref, g_ref, sh_ref, sc_ref, w_ref, *rest):
    if rope:
        ca_ref, sa_ref, cc_ref, sc2_ref = rest[:4]
        rest = rest[4:]
    cast_in, rest = rest[:n_cast], rest[n_cast:]
    qa_o, ka_o, va_o, ub_o, qc_o, kc_o, vc_o, qd_o, kd_o, vd_o = rest[:10]
    for src_ref, dst_ref in zip(cast_in, rest[10:]):
        dst_ref[...] = src_ref[0].astype(BF16)
    x = x_ref[...]
    r = lax.rsqrt(jnp.mean(x * x, axis=-1, keepdims=True) + EPS)
    h = (x * r) * g_ref[...]
    h = h * (1.0 + sc_ref[0]) + sh_ref[0]
    hb = h.astype(BF16)

    def proj(off, width=GROUP_W):
        return _dot(hb, w_ref[:, off:off + width])

    def store_rot(o_ref, y, c_ref, s_ref, scale, channel_major=False):
        if rope:
            x1, x2 = y[:, :128], y[:, 128:]
            cs, sn = c_ref[...], s_ref[...]
            y1, y2 = x1 * cs - x2 * sn, x1 * sn + x2 * cs
        else:
            y1, y2 = y[:, :128], y[:, 128:]
        if channel_major:
            o_ref[:128, :] = (y1 * scale).T.astype(BF16)
            o_ref[128:, :] = (y2 * scale).T.astype(BF16)
        else:
            o_ref[:, :128] = (y1 * scale).astype(BF16)
            o_ref[:, 128:] = (y2 * scale).astype(BF16)

    ca = sa = cc = sc2 = None
    if rope:
        ca, sa, cc, sc2 = ca_ref, sa_ref, cc_ref, sc2_ref
    store_rot(qa_o, proj(OFF_QA), ca, sa, ATTN_Q_SCALE)
    store_rot(ka_o, proj(OFF_KA), ca, sa, 1.0)
    va_o[...] = proj(OFF_VA).astype(BF16)
    ub_o[...] = _glu(proj(OFF_UB, 2 * GROUP_W))
    store_rot(qc_o, proj(OFF_QC), cc, sc2, DIFF_Q_SCALE, channel_major=True)
    store_rot(kc_o, proj(OFF_KC), cc, sc2, 1.0)
    vc_o[...] = proj(OFF_VC).T.astype(BF16)
    qd_o[...] = (proj(OFF_QD) * ATTN_Q_SCALE).astype(BF16)
    kd_o[...] = proj(OFF_KD).astype(BF16)
    vd_o[...] = proj(OFF_VD).astype(BF16)


def _mod_spec(d, chunk, mod_rows, tm, lag=0):
    first_row, rows_per_mod = mod_rows
    tiles_per_mod = rows_per_mod // tm
    return pl.BlockSpec(
        (1, 1, d), lambda i: ((first_row + jnp.maximum(i - lag, 0) // tiles_per_mod) * 6 + chunk, 0, 0))


def _in_projection(x, g, mods, mod_rows, w, rope_tables, cast_weights=(), layer=0):
    rows, d = x.shape
    tm = PROJ_ROW_TILE
    steps = rows // tm
    rope = rope_tables is not None
    in_specs = [
        pl.BlockSpec((tm, d), lambda i: (i, 0)),
        pl.BlockSpec((1, d), lambda i: (0, 0)),
        _mod_spec(d, 0, mod_rows, tm), _mod_spec(d, 1, mod_rows, tm),
        _resident((d, PROJ_WIDTH)),
    ]
    args = [x, g.reshape(1, d), mods, mods, w]
    if rope:
        tiles_per_seq = rope_tables[0].shape[0] // tm
        tab_spec = pl.BlockSpec((tm, 128), lambda i: (i % tiles_per_seq, 0))
        in_specs += [tab_spec] * 4
        args += list(rope_tables)
    narrow = pl.BlockSpec((tm, GROUP_W), lambda i: (i, 0))
    chan_major = pl.BlockSpec((GROUP_W, tm), lambda i: (0, i))
    out_specs = [narrow, narrow, narrow, narrow, chan_major, narrow, chan_major, narrow, narrow, narrow]
    bf = jax.ShapeDtypeStruct((rows, GROUP_W), BF16)
    bf_t = jax.ShapeDtypeStruct((GROUP_W, rows), BF16)
    out_shape = [bf, bf, bf, jax.ShapeDtypeStruct((rows, GROUP_W), F32), bf_t, bf, bf_t, bf, bf, bf]
    for cw in cast_weights:
        _, r_w, c_w = cw.shape
        blk_rows = next(n for n in range(BF16_SUBLANES, r_w + 1, BF16_SUBLANES)
                        if r_w % n == 0 and steps % (r_w // n) == 0 and r_w // n <= steps)
        steps_per_blk = steps // (r_w // blk_rows)
        in_specs.append(pl.BlockSpec((1, blk_rows, c_w), lambda i, s=steps_per_blk: (layer, i // s, 0)))
        args.append(cw)
        out_specs.append(pl.BlockSpec((blk_rows, c_w), lambda i, s=steps_per_blk: (i // s, 0)))
        out_shape.append(jax.ShapeDtypeStruct((r_w, c_w), BF16))
    return pl.pallas_call(
        functools.partial(_inproj_kernel, rope, len(cast_weights)),
        grid=(steps,),
        in_specs=in_specs,
        out_specs=out_specs,
        out_shape=out_shape,
        compiler_params=_compiler_params(),
        name="in_projection",
    )(*args)


def _lane_iota():
    return lax.broadcasted_iota(jnp.int32, (1, GROUP_W), 1)


def _split_half_head(lane):
    return jnp.right_shift(jnp.bitwise_and(lane, 127), 5)


def _natural_head(lane):
    return jnp.right_shift(lane, 6)


def _stack_heads(q, lane_group, groups):
    zero = jnp.zeros_like(q)
    return jnp.concatenate([jnp.where(lane_group == g, q, zero) for g in groups], axis=0)


def _softmax_pv(scores, values, sink_col=None):
    m = functools.reduce(jnp.maximum, [jnp.max(s, axis=-1, keepdims=True) for s in scores])
    if sink_col is not None:
        m = jnp.maximum(m, sink_col)
    es = [jnp.exp2(s - m) for s in scores]
    l = functools.reduce(jnp.add, [jnp.sum(e, axis=-1, keepdims=True) for e in es])
    if sink_col is not None:
        l = l + jnp.exp2(sink_col - m)
    o = functools.reduce(jnp.add, [_dot(e.astype(BF16), v) for e, v in zip(es, values)])
    return o / l


def _select_heads(o, tq):
    head = _natural_head(_lane_iota())
    out = o[0:tq]
    for h in range(1, N_HEADS):
        out = jnp.where(head == h, o[h * tq:(h + 1) * tq], out)
    return out


def _sink_column(sink_ref, tq):
    row = lax.broadcasted_iota(jnp.int32, (N_HEADS * tq, 1), 0)
    col = jnp.full((N_HEADS * tq, 1), sink_ref[N_HEADS - 1], F32)
    for h in range(N_HEADS - 2, -1, -1):
        col = jnp.where(row < (h + 1) * tq, sink_ref[h], col)
    return col * LOG2E


def _win_attn_kernel(sink_ref, q_ref, kl_ref, vl_ref, kc_ref, vc_ref, o_ref):
    seq = q_ref.shape[1]
    lane_head = _split_half_head(_lane_iota())
    sink_col = _sink_column(sink_ref, BLK)
    shape = (BLK, 3 * BLK)
    rel = lax.broadcasted_iota(jnp.int32, shape, 1) - lax.broadcasted_iota(jnp.int32, shape, 0)
    bands = {}
    for n in range(seq // BLK):
        start = min(max((n - 1) * BLK, 0), seq - 3 * BLK)
        shift = start - n * BLK
        if shift not in bands:
            bands[shift] = jnp.where(jnp.abs(rel + shift) <= A_WINDOW, 0.0, NEG_INF)
        qs = _stack_heads(q_ref[0, n * BLK:(n + 1) * BLK, :], lane_head, range(N_HEADS))
        kl = kl_ref[0, start:start + 3 * BLK, :]
        vl = vl_ref[0, start:start + 3 * BLK, :]
        s_loc = _dot_nt(qs, kl)
        s_ctx = _dot_nt(qs, kc_ref[0])
        s_loc = (s_loc.reshape(N_HEADS, BLK, 3 * BLK) + bands[shift][None]).reshape(N_HEADS * BLK, 3 * BLK)
        o = _softmax_pv([s_loc, s_ctx], [vl, vc_ref[0]], sink_col)
        o_ref[0, n * BLK:(n + 1) * BLK, :] = _select_heads(o, BLK).astype(BF16)


def _window_attention(sink, q, k, v, kc, vc):
    b, seq, w = q.shape
    ctx_len = kc.shape[1]
    whole = pl.BlockSpec((1, seq, w), lambda i: (i, 0, 0))
    ctx_spec = pl.BlockSpec((1, ctx_len, w), lambda i: (i, 0, 0))
    return pl.pallas_call(
        _win_attn_kernel,
        grid=(b,),
        in_specs=[pl.BlockSpec(memory_space=pltpu.SMEM), whole, whole, whole, ctx_spec, ctx_spec],
        out_specs=whole,
        out_shape=jax.ShapeDtypeStruct((b, seq, w), BF16),
        compiler_params=_compiler_params(),
        name="window_attention",
    )(sink, q, k, v, kc, vc)


def _na_bias_table(rpb):
    cq = np.arange(GRID_W)
    cs = np.clip(cq - NA_KW // 2, 0, GRID_W - NA_KW)
    col_valid = (cq[None, :] >= cs[:, None]) & (cq[None, :] < cs[:, None] + NA_KW)
    dc = np.clip(cq[None, :] - cq[:, None], -(NA_KW - 1), NA_KW - 1) + (NA_KW - 1)
    n_dr, n_dc = 2 * NA_KH - 1, 2 * NA_KW - 1
    onehot = (dc.reshape(1, -1) == np.arange(n_dc)[:, None]).astype(np.float32)
    t = jnp.dot(rpb.astype(F32).reshape(N_HEADS * n_dr, n_dc), onehot, precision=lax.Precision.HIGHEST)
    t = jnp.where(col_valid[None, None], t.reshape(N_HEADS, n_dr, GRID_W, GRID_W) * LOG2E, NEG_INF)
    t = jnp.transpose(t, (1, 0, 2, 3)).reshape(n_dr, N_HEADS * GRID_W, GRID_W)
    return jnp.concatenate([t[:-1], t[1:]], axis=-1)


def _na_attn_kernel(q_ref, kl_ref, vl_ref, kc_ref, vc_ref, bias_ref, o_ref):
    rows = q_ref.shape[1] // GRID_W
    lane_head = _natural_head(_lane_iota())
    for r in range(rows):
        first = min(max(r - NA_KH // 2, 0), rows - NA_KH)
        start = first * GRID_W
        off = first - r + NA_KH - 1
        qs = _stack_heads(q_ref[0, r * GRID_W:(r + 1) * GRID_W, :], lane_head, range(N_HEADS))
        kl = kl_ref[0, start:start + NA_KH * GRID_W, :]
        vl = vl_ref[0, start:start + NA_KH * GRID_W, :]
        bias = jnp.concatenate([bias_ref[off + 2 * i] for i in range(NA_KH // 2)], axis=1)
        s_loc = _dot_nt(qs, kl) + bias
        s_ctx = _dot_nt(qs, kc_ref[0])
        o = _softmax_pv([s_loc, s_ctx], [vl, vc_ref[0]])
        o_ref[0, r * GRID_W:(r + 1) * GRID_W, :] = _select_heads(o, GRID_W).astype(BF16)


def _neighbourhood_attention(bias, q, k, v, kc, vc):
    b, seq, w = q.shape
    ctx_len = kc.shape[1]
    whole = pl.BlockSpec((1, seq, w), lambda i: (i, 0, 0))
    ctx_spec = pl.BlockSpec((1, ctx_len, w), lambda i: (i, 0, 0))
    return pl.pallas_call(
        _na_attn_kernel,
        grid=(b,),
        in_specs=[whole, whole, whole, ctx_spec, ctx_spec, _resident(bias.shape)],
        out_specs=whole,
        out_shape=jax.ShapeDtypeStruct((b, seq, w), BF16),
        compiler_params=_compiler_params(),
        name="neighbourhood_attention",
    )(q, k, v, kc, vc, bias)


def _ctx_attn_kernel(split_layout, has_sink, *refs):
    if has_sink:
        sink_ref, q_ref, k_ref, v_ref, o_ref = refs
    else:
        q_ref, k_ref, v_ref, o_ref = refs
    tq = q_ref.shape[1]
    lane = _lane_iota()
    group = _split_half_head(lane) if split_layout else _natural_head(lane)
    qs = _stack_heads(q_ref[0], group, range(N_HEADS))
    s = _dot_nt(qs, k_ref[0])
    sink_col = _sink_column(sink_ref, tq) if has_sink else None
    o = _softmax_pv([s], [v_ref[0]], sink_col)
    o_ref[0] = _select_heads(o, tq).astype(BF16)


def _context_attention(q, k, v, sink=None, split_layout=False):
    b, n, w = q.shape
    spec = pl.BlockSpec((1, n, w), lambda i: (i, 0, 0))
    in_specs = [spec, spec, spec]
    args = [q, k, v]
    if sink is not None:
        in_specs = [pl.BlockSpec(memory_space=pltpu.SMEM)] + in_specs
        args = [sink] + args
    return pl.pallas_call(
        functools.partial(_ctx_attn_kernel, split_layout, sink is not None),
        grid=(b,),
        in_specs=in_specs,
        out_specs=spec,
        out_shape=jax.ShapeDtypeStruct((b, n, w), BF16),
        compiler_params=_compiler_params(),
        name="context_attention",
    )(*args)


def _diff_attn_kernel(has_local, lambda_init, lam_ref, subg_ref, qt_ref, *refs):
    if has_local:
        kl_ref, vtl_ref, kc_ref, vtc_ref, o_ref = refs
    else:
        kc_ref, vtc_ref, o_ref = refs
    tq = min(DIFF_TQ, qt_ref.shape[1])
    n_blocks = qt_ref.shape[1] // tq
    lv = lam_ref[...]
    lam = (jnp.exp(jnp.sum(lv[0:1] * lv[1:2], axis=-1, keepdims=True))
           - jnp.exp(jnp.sum(lv[2:3] * lv[3:4], axis=-1, keepdims=True)) + lambda_init)
    chan = lax.broadcasted_iota(jnp.int32, (GROUP_W, 1), 0)
    group = jnp.right_shift(jnp.bitwise_and(chan, 127), 4)
    zero = jnp.zeros((GROUP_W, tq), BF16)
    segs = [(kc_ref, vtc_ref, 0, kc_ref.shape[1])]
    if has_local:
        n_loc = kl_ref.shape[1]
        segs = [(kl_ref, vtl_ref, k0, DIFF_KEY_SEG) for k0 in range(0, n_loc, DIFF_KEY_SEG)] + segs
    sum_rows = 16

    def head_scores(unit):
        blk, h = unit
        qt = qt_ref[:, blk * tq:(blk + 1) * tq]
        qs = jnp.concatenate([jnp.where(group == 2 * h, qt, zero),
                              jnp.where(group == 2 * h + 1, qt, zero)], axis=1)
        return [_dot(k_ref[0, k0:k0 + nk, :], qs) for k_ref, _, k0, nk in segs]

    def column_max(s, slab=64):
        part = jnp.max(s.reshape(s.shape[0] // slab, slab, s.shape[1]), axis=0)
        return jnp.max(part, axis=0, keepdims=True)

    ys = []
    units = [(blk, h) for blk in range(n_blocks) for h in range(N_HEADS)]
    nxt = head_scores(units[0])
    for ui, (blk, h) in enumerate(units):
        scores = nxt
        if ui + 1 < len(units):
            nxt = head_scores(units[ui + 1])
        m = functools.reduce(jnp.maximum, [column_max(s) for s in scores])
        pv = None
        for s, (_, vt_ref, k0, nk) in zip(scores, segs):
            e = jnp.exp2(s - m).astype(BF16)
            vt = vt_ref[h * HEAD_DIM:(h + 1) * HEAD_DIM, k0:k0 + nk]
            lhs = jnp.concatenate([vt, jnp.ones((sum_rows, nk), BF16)], axis=0)
            p = _dot(lhs, e)
            pv = p if pv is None else pv + p
        inv = 1.0 / pv[HEAD_DIM:HEAD_DIM + 1]
        pv = pv[:HEAD_DIM]
        o = pv[:, :tq] * inv[:, :tq] - pv[:, tq:] * (lam * inv[:, tq:])
        ms = jnp.mean(o * o, axis=0, keepdims=True)
        ys.append(o * lax.rsqrt(ms + EPS))
        if h == N_HEADS - 1:
            out = jnp.concatenate(ys, axis=0).T
            o_ref[0, blk * tq:(blk + 1) * tq, :] = (out * subg_ref[...] * (1.0 - lambda_init)).astype(BF16)
            ys = []


def _diff_attention(lam_vecs, subg, lambda_init, b, qt, kc, vtc, k=None, vt=None):
    w = qt.shape[0]
    n = qt.shape[1] // b
    ctx_len = kc.shape[1]
    has_local = k is not None
    tq = min(DIFF_TQ * DIFF_BLOCKS_PER_STEP, n)
    steps = n // tq
    q_spec = pl.BlockSpec((w, tq), lambda i, j: (0, i * steps + j))
    in_specs = [pl.BlockSpec(lam_vecs.shape, lambda i, j: (0, 0)),
                pl.BlockSpec((1, w), lambda i, j: (0, 0)), q_spec]
    args = [lam_vecs, subg, qt]
    if has_local:
        in_specs += [pl.BlockSpec((1, n, w), lambda i, j: (i, 0, 0)),
                     pl.BlockSpec((w, n), lambda i, j: (0, i))]
        args += [k, vt]
    in_specs += [pl.BlockSpec((1, ctx_len, w), lambda i, j: (i, 0, 0)),
                 pl.BlockSpec((w, ctx_len), lambda i, j: (0, i))]
    args += [kc, vtc]
    return pl.pallas_call(
        functools.partial(_diff_attn_kernel, has_local, lambda_init),
        grid=(b, n // tq),
        in_specs=in_specs,
        out_specs=pl.BlockSpec((1, tq, w), lambda i, j: (i, j, 0)),
        out_shape=jax.ShapeDtypeStruct((b, n, w), BF16),
        compiler_params=_compiler_params(),
        name="diff_attention",
    )(*args)


CONV_WIN = CONV_TC + 2 * CONV_HALO


def _glu(u):
    return u[:, :GROUP_W] * _sigmoid(u[:, GROUP_W:])


def _conv_chunk(win, w_ref, b_ref, g_ref, beta_ref):
    halo, tc = CONV_HALO, CONV_TC
    acc = jnp.zeros((tc, win.shape[1]), F32)
    for sub in range(8):
        shifted = win if sub == 0 else pltpu.roll(win, CONV_WIN - sub, axis=0)
        for blk8 in range(CONV_WIN // 8):
            tap = 8 * blk8 + sub - (halo - CONV_K // 2)
            if 0 <= tap < CONV_K:
                acc = acc + shifted[8 * blk8:8 * blk8 + tc] * w_ref[tap:tap + 1, :]
    hcv = acc + b_ref[...]
    mu = jnp.mean(hcv, axis=-1, keepdims=True)
    cen = hcv - mu
    var = jnp.mean(cen * cen, axis=-1, keepdims=True)
    y = cen * lax.rsqrt(var + EPS) * g_ref[...] + beta_ref[...]
    return y * _sigmoid(y)


def _conv_kernel(seq, hg_ref, w_ref, b_ref, g_ref, beta_ref, o_ref, pad_ref):
    halo, tc, ch = CONV_HALO, CONV_TC, GROUP_W
    zeros = jnp.zeros((halo, ch), F32)
    pad_ref[0:halo, :] = zeros
    pad_ref[halo:halo + seq, :] = hg_ref[0]
    pad_ref[halo + seq:2 * halo + seq, :] = zeros

    def chunk(i, carry):
        c0 = pl.multiple_of(i * tc, tc)
        win = pad_ref[pl.ds(c0, CONV_WIN), :]
        o_ref[0, pl.ds(c0, tc), :] = _conv_chunk(win, w_ref, b_ref, g_ref, beta_ref).astype(BF16)
        return carry

    lax.fori_loop(0, seq // tc, chunk, 0, unroll=4)


def _conformer_conv(hg, w, bias, ln_g, ln_b):
    b, seq, ch = hg.shape
    vec = pl.BlockSpec((1, ch), lambda i: (0, 0))
    return pl.pallas_call(
        functools.partial(_conv_kernel, seq),
        grid=(b,),
        in_specs=[pl.BlockSpec((1, seq, ch), lambda i: (i, 0, 0)),
                  pl.BlockSpec((CONV_K, ch), lambda i: (0, 0)), vec, vec, vec],
        out_specs=pl.BlockSpec((1, seq, ch), lambda i: (i, 0, 0)),
        out_shape=jax.ShapeDtypeStruct((b, seq, ch), BF16),
        scratch_shapes=[pltpu.VMEM((seq + 2 * CONV_HALO, ch), F32)],
        compiler_params=_compiler_params(),
        name="conformer_conv",
    )(hg, w, bias.reshape(1, ch), ln_g.reshape(1, ch), ln_b.reshape(1, ch))


def _fused_conv_pad(tiles_per_seq, u_ref, up_ref, un_ref, pad_ref):
    halo, tm = CONV_HALO, u_ref.shape[0]
    n_tiles = pl.num_programs(0) - 1
    pos = lax.rem(jnp.minimum(pl.program_id(0), n_tiles - 1), tiles_per_seq)
    pad_ref[0:halo, :] = jnp.where(pos > 0, up_ref[...], 0.0)
    pad_ref[halo:halo + tm, :] = u_ref[...]
    pad_ref[halo + tm:2 * halo + tm, :] = jnp.where(pos < tiles_per_seq - 1, un_ref[...], 0.0)


def _ffn_kernel(final, conv_tiles_per_seq, x_ref, ya_ref, *rest):
    if conv_tiles_per_seq is None:
        yb_ref, rest = rest[0], rest[1:]
    else:
        conv_refs, rest = rest[:7], rest[7:]
    (yc_ref, yd_ref, g1_ref, sh_ref, sc_ref, g2_ref, ng_ref, wo_ref, wg_ref, wu_ref, wd_ref), rest = rest[:11], rest[11:]
    if final:
        fg_ref, rest = rest[0], rest[1:]
    o_ref, ycat_ref, act_ref = rest[:3]
    if conv_tiles_per_seq is not None:
        yb_ref, pad_ref = rest[3:]

        @pl.when(pl.program_id(0) == 0)
        def _():
            yb_ref[...] = jnp.zeros(yb_ref.shape, BF16)

    for j, y_ref in enumerate((ya_ref, yb_ref, yc_ref, yd_ref)):
        ycat_ref[:, j * GROUP_W:(j + 1) * GROUP_W] = y_ref[...]
    conv_chunks = []
    if conv_tiles_per_seq is not None:
        u_ref, up_ref, un_ref, cw_ref, cb_ref, cg_ref, cbeta_ref = conv_refs
        _fused_conv_pad(conv_tiles_per_seq, u_ref, up_ref, un_ref, pad_ref)
        conv_chunks = list(range(0, u_ref.shape[0], CONV_TC))
    x = x_ref[...] + g1_ref[0] * _dot(ycat_ref[...], wo_ref[...])
    r = lax.rsqrt(jnp.mean(x * x, axis=-1, keepdims=True) + EPS)
    h = (x * r) * ng_ref[...]
    hb = (h * (1.0 + sc_ref[0]) + sh_ref[0]).astype(BF16)
    for c in range(0, FFN_HIDDEN, FFN_CHUNK):
        gate = _dot(hb, wg_ref[:, c:c + FFN_CHUNK])
        up = _dot(hb, wu_ref[:, c:c + FFN_CHUNK])
        act_ref[:, c:c + FFN_CHUNK] = (gate * _sigmoid(gate) * up).astype(BF16)
        if conv_chunks:
            c0 = conv_chunks.pop(0)
            win = pad_ref[c0:c0 + CONV_WIN, :]
            yconv = _conv_chunk(win, cw_ref, cb_ref, cg_ref, cbeta_ref)
            yb_ref[c0:c0 + CONV_TC, :] = yconv.astype(BF16)
            anchor = (slice(0, BF16_SUBLANES), slice(c, c + 128))
            act_ref[anchor] = (act_ref[anchor].astype(F32) + 0.0 * yconv[:BF16_SUBLANES, :128]).astype(BF16)
    assert not conv_chunks
    x = x + g2_ref[0] * _dot(act_ref[...], wd_ref[...])
    if final:
        r = lax.rsqrt(jnp.mean(x * x, axis=-1, keepdims=True) + EPS)
        x = (x * r) * fg_ref[...]
    o_ref[...] = x


def _out_projection_ffn(x, ys, mods, mod_rows, norm_g, wo, wg, wu, wd, final_g=None, conv=None):
    rows, d = x.shape
    tm = ROW_TILE
    n_tiles = rows // tm
    final = final_g is not None
    lag = 0 if conv is None else 1

    def tile(i):
        return jnp.maximum(i - lag, 0)

    row_spec = pl.BlockSpec((tm, d), lambda i: (tile(i), 0))
    y_spec = pl.BlockSpec((tm, GROUP_W), lambda i: (tile(i), 0))
    vec_spec = pl.BlockSpec((1, d), lambda i: (0, 0))
    scratch = [pltpu.VMEM((tm, d), BF16), pltpu.VMEM((tm, FFN_HIDDEN), BF16)]
    if conv is None:
        yb_specs, yb_args, tiles_per_seq = [y_spec], [ys[1]], None
    else:
        u, cw, cb, cg, cbeta, seq = conv
        ch = cw.shape[1]
        halo_blocks = tm // CONV_HALO

        def conv_tile(i):
            return jnp.minimum(i, n_tiles - 1)

        cvec = pl.BlockSpec((1, ch), lambda i: (0, 0))
        yb_specs = [
            pl.BlockSpec((tm, ch), lambda i: (conv_tile(i), 0)),
            pl.BlockSpec((CONV_HALO, ch), lambda i: (jnp.maximum(conv_tile(i) * halo_blocks - 1, 0), 0)),
            pl.BlockSpec((CONV_HALO, ch),
                         lambda i: (jnp.minimum((conv_tile(i) + 1) * halo_blocks, rows // CONV_HALO - 1), 0)),
            pl.BlockSpec(cw.shape, lambda i: (0, 0)), cvec, cvec, cvec]
        yb_args = [u, u, u, cw, cb.reshape(1, ch), cg.reshape(1, ch), cbeta.reshape(1, ch)]
        tiles_per_seq = seq // tm
        scratch += [pltpu.VMEM((tm, ch), BF16), pltpu.VMEM((tm + 2 * CONV_HALO, ch), F32)]
    in_specs = ([row_spec, y_spec] + yb_specs + [y_spec, y_spec]
                + [_mod_spec(d, chunk, mod_rows, tm, lag) for chunk in (2, 3, 4, 5)]
                + [vec_spec, _resident(wo.shape), _resident(wg.shape), _resident(wu.shape), _resident(wd.shape)])
    args = [x, ys[0], *yb_args, ys[2], ys[3], mods, mods, mods, mods, norm_g.reshape(1, d), wo, wg, wu, wd]
    if final:
        in_specs.append(vec_spec)
        args.append(final_g.reshape(1, d))
    return pl.pallas_call(
        functools.partial(_ffn_kernel, final, tiles_per_seq),
        grid=(n_tiles + lag,),
        in_specs=in_specs,
        out_specs=row_spec,
        out_shape=jax.ShapeDtypeStruct((rows, d), F32),
        scratch_shapes=scratch,
        compiler_params=_compiler_params(),
        name="out_projection_ffn",
    )(*args)


def _rope_tables(n_tok, dim):
    t = np.arange(n_tok)
    row = (t // GRID_W).astype(np.float32)
    col = (t % GRID_W).astype(np.float32)
    nf = dim // 4
    inv = (np.float32(ROPE_BASE) ** (-np.arange(nf, dtype=np.float32) / np.float32(nf))).astype(np.float32)
    ang = np.concatenate([row[:, None] * inv, col[:, None] * inv], axis=-1).astype(np.float64)
    reps = 128 // (dim // 2)
    return (jnp.asarray(np.tile(np.cos(ang), (1, reps)), F32), jnp.asarray(np.tile(np.sin(ang), (1, reps)), F32))


def kernel(x, c, ctx, c_ctx, norm1_g, norm2_g, w_ada, b_ada, w_in, w_out, attn_sink, conv_w, conv_b,
           conv_ln_g, conv_ln_b, diff_lq1, diff_lk1, diff_lq2, diff_lk2, diff_subln_g, na_rpb,
           w_gate, w_up, w_down, final_g):
    batch, seq, d = x.shape
    ctx_len = ctx.shape[1]
    depth = w_ada.shape[0]

    cvec = jnp.zeros((ADA_ROWS, d), F32).at[:batch].set(c).at[batch].set(c_ctx)
    mods_all = _ada_table(cvec, w_ada, b_ada).reshape(depth, ADA_ROWS * 6, 1, d)
    rope = _rope_tables(seq, HEAD_DIM) + _rope_tables(seq, C_QK_DIM)
    lat_rows = (0, seq)
    ctx_rows = (batch, batch * ctx_len)

    xl = x.reshape(batch * seq, d)
    xc = ctx.reshape(batch * ctx_len, d)
    for l in range(depth):
        ctx_needed = l < depth - 1
        mods = mods_all[l]
        lambda_init = 0.8 - 0.6 * math.exp(-0.3 * l)
        w_in_l = _relayout_w_in(w_in, l)
        lam_vecs = jnp.stack([diff_lq1[l], diff_lk1[l], diff_lq2[l], diff_lk2[l]]).astype(F32)
        subg = jnp.tile(diff_subln_g[l], N_HEADS).reshape(1, GROUP_W)
        cw = conv_w[l].reshape(CONV_K, GROUP_W)

        def per_batch(tensors, n_tok):
            return [t if i in (4, 6) else t.reshape(batch, n_tok, -1) for i, t in enumerate(tensors)]

        lat_out = _in_projection(xl, norm1_g[l], mods, lat_rows, w_in_l, rope,
                                 cast_weights=(w_out, w_gate, w_up, w_down), layer=l)
        qa, ka, va, ub, qct, kc, vct, qd, kd, vd = per_batch(lat_out[:10], seq)
        wo, wg, wu, wd = lat_out[10:]
        qa_c, ka_c, va_c, ub_c, qct_c, kc_c, vct_c, qd_c, kd_c, vd_c = per_batch(
            _in_projection(xc, norm1_g[l], mods, ctx_rows, w_in_l, None), ctx_len)

        y_a = _window_attention(attn_sink[l], qa, ka, va, ka_c, va_c)
        y_c = _diff_attention(lam_vecs, subg, lambda_init, batch, qct, kc_c, vct_c, kc, vct)
        y_d = _neighbourhood_attention(_na_bias_table(na_rpb[l]), qd, kd, vd, kd_c, vd_c)
        ys = [None if t is None else t.reshape(batch * seq, GROUP_W) for t in (y_a, None, y_c, y_d)]
        conv = (ub.reshape(batch * seq, GROUP_W), cw, conv_b[l], conv_ln_g[l], conv_ln_b[l], seq)
        xl = _out_projection_ffn(xl, ys, mods, lat_rows, norm2_g[l], wo, wg, wu, wd,
                                 final_g=None if ctx_needed else final_g, conv=conv)
        if ctx_needed:
            yc_a = _context_attention(qa_c, ka_c, va_c, sink=attn_sink[l], split_layout=True)
            yc_b = _conformer_conv(ub_c, cw, conv_b[l], conv_ln_g[l], conv_ln_b[l])
            yc_c = _diff_attention(lam_vecs, subg, lambda_init, batch, qct_c, kc_c, vct_c)
            yc_d = _context_attention(qd_c, kd_c, vd_c)
            ycs = [t.reshape(batch * ctx_len, GROUP_W) for t in (yc_a, yc_b, yc_c, yc_d)]
            xc = _out_projection_ffn(xc, ycs, mods, ctx_rows, norm2_g[l], wo, wg, wu, wd)
    return xl.reshape(batch, seq, d)
```

```python
import functools
import math

import numpy as np
import jax
import jax.numpy as jnp
from jax import lax
from jax.experimental import pallas as pl
from jax.experimental.pallas import tpu as pltpu

F32 = jnp.float32
BF16 = jnp.bfloat16

D_MODEL = 1024
DEPTH = 2
GRID_W = 64
HEAD_DIM = 64
GROUP_W = 256
N_HEADS = 4
A_KV_HEADS = 2
A_WINDOW = 128
BLK = 128
CONV_K = 31
C_QK_DIM = 32
NA_KH = 8
NA_KW = 16
FFN_HIDDEN = 2816
ROPE_BASE = 10000.0
EPS = 1e-6
NEG_INF = -1e30

VMEM_LIMIT = 56 * 1024 * 1024
BF16_SUBLANES = 16
ADA_ROWS = 16
ADA_TN = 1536
ROW_TILE = 512
PROJ_ROW_TILE = 1024
FFN_CHUNK = 256
CONV_TC = 32
CONV_HALO = 16
DIFF_TQ = 512
LOG2E = math.log2(math.e)
ATTN_Q_SCALE = (HEAD_DIM ** -0.5) * LOG2E
DIFF_Q_SCALE = (C_QK_DIM ** -0.5) * LOG2E
DIFF_BLOCKS_PER_STEP = 2
DIFF_KEY_SEG = 1024


def _compiler_params():
    return pltpu.CompilerParams(vmem_limit_bytes=VMEM_LIMIT)


def _resident(shape):
    return pl.BlockSpec(shape, lambda *_: (0,) * len(shape), pipeline_mode=pl.Buffered(1))


def _sigmoid(v):
    return 1.0 / (1.0 + jnp.exp(-v))


def _dot(a, b):
    return jnp.dot(a, b, preferred_element_type=F32)


def _dot_nt(a, b):
    return lax.dot_general(a, b, (((1,), (1,)), ((), ())), preferred_element_type=F32)


def _ada_kernel(c_ref, w_ref, b_ref, o_ref):
    cv = c_ref[...]
    s = cv * _sigmoid(cv)
    o_ref[0] = _dot(s.astype(BF16), w_ref[0].astype(BF16)) + b_ref[0]


def _ada_table(cvec, w_ada, b_ada):
    depth, d, n = w_ada.shape
    return pl.pallas_call(
        _ada_kernel,
        grid=(depth, n // ADA_TN),
        in_specs=[
            pl.BlockSpec((ADA_ROWS, d), lambda l, j: (0, 0)),
            pl.BlockSpec((1, d, ADA_TN), lambda l, j: (l, 0, j)),
            pl.BlockSpec((1, 1, ADA_TN), lambda l, j: (l, 0, j)),
        ],
        out_specs=pl.BlockSpec((1, ADA_ROWS, ADA_TN), lambda l, j: (l, 0, j)),
        out_shape=jax.ShapeDtypeStruct((depth, ADA_ROWS, n), F32),
        compiler_params=_compiler_params(),
        name="ada_table",
    )(cvec, w_ada, b_ada.reshape(depth, 1, n))


def _selection_matrices():
    lane = np.arange(GROUP_W)
    part, h, j = lane // 128, (lane % 128) // 32, lane % 32
    grp, jc = (lane % 128) // 16, lane % 16
    sources = (
        (GROUP_W, h * HEAD_DIM + part * 32 + j),
        (A_KV_HEADS * HEAD_DIM, (h // 2) * HEAD_DIM + part * 32 + j),
        (A_KV_HEADS * HEAD_DIM, ((lane // HEAD_DIM) // 2) * HEAD_DIM + lane % HEAD_DIM),
        (GROUP_W, grp * C_QK_DIM + part * 16 + jc),
    )
    return [np.equal(np.arange(n_src)[:, None], src[None, :]).astype(np.float32) for n_src, src in sources]


def _relayout_kernel(w_ref, s_qa, s_ka, s_va, s_c, o_ref):
    def cols(a, b):
        return w_ref[0, :, a:b].astype(BF16)

    def select(a, b, s_ref):
        return _dot(cols(a, b), s_ref[...]).astype(BF16)

    o_ref[:, OFF_QA:OFF_KA] = select(0, 256, s_qa)
    o_ref[:, OFF_KA:OFF_VA] = select(256, 384, s_ka)
    o_ref[:, OFF_VA:OFF_UB] = select(384, 512, s_va)
    o_ref[:, OFF_UB:OFF_QC] = cols(512, 1024)
    o_ref[:, OFF_QC:OFF_KC] = select(1024, 1280, s_c)
    o_ref[:, OFF_KC:OFF_VC] = select(1280, 1536, s_c)
    o_ref[:, OFF_VC:PROJ_WIDTH] = cols(1536, 2560)


def _relayout_w_in(w_in, layer):
    _, d, n_in = w_in.shape
    rows = 256
    sel = [jnp.asarray(s, BF16) for s in _selection_matrices()]
    return pl.pallas_call(
        _relayout_kernel,
        grid=(d // rows,),
        in_specs=[pl.BlockSpec((1, rows, n_in), lambda i: (layer, i, 0))]
                 + [pl.BlockSpec(s.shape, lambda i: (0, 0)) for s in sel],
        out_specs=pl.BlockSpec((rows, PROJ_WIDTH), lambda i: (i, 0)),
        out_shape=jax.ShapeDtypeStruct((d, PROJ_WIDTH), BF16),
        compiler_params=_compiler_params(),
        name="relayout_w_in",
    )(w_in, *sel)


PROJ_WIDTH = 9 * GROUP_W + 512
OFF_QA, OFF_KA, OFF_VA, OFF_UB, OFF_QC, OFF_KC, OFF_VC, OFF_QD, OFF_KD, OFF_VD = (
    0, 256, 512, 768, 1280, 1536, 1792, 2048, 2304, 2560)


def _inproj_kernel(rope, n_cast, x_ref, g_ref, sh_ref, sc_ref, w_ref, *rest):
    if rope:
        ca_ref, sa_ref, cc_ref, sc2_ref = rest[:4]
        rest = rest[4:]
    cast_in, rest = rest[:n_cast], rest[n_cast:]
    qa_o, ka_o, va_o, ub_o, qc_o, kc_o, vc_o, qd_o, kd_o, vd_o = rest[:10]
    for src_ref, dst_ref in zip(cast_in, rest[10:]):
        dst_ref[...] = src_ref[0].astype(BF16)
    x = x_ref[...]
    r = lax.rsqrt(jnp.mean(x * x, axis=-1, keepdims=True) + EPS)
    h = (x * r) * g_ref[...]
    h = h * (1.0 + sc_ref[0]) + sh_ref[0]
    hb = h.astype(BF16)

    def proj(off, width=GROUP_W):
        return _dot(hb, w_ref[:, off:off + width])

    def store_rot(o_ref, y, c_ref, s_ref, scale, channel_major=False):
        if rope:
            x1, x2 = y[:, :128], y[:, 128:]
            cs, sn = c_ref[...], s_ref[...]
            y1, y2 = x1 * cs - x2 * sn, x1 * sn + x2 * cs
        else:
            y1, y2 = y[:, :128], y[:, 128:]
        if channel_major:
            o_ref[:128, :] = (y1 * scale).T.astype(BF16)
            o_ref[128:, :] = (y2 * scale).T.astype(BF16)
        else:
            o_ref[:, :128] = (y1 * scale).astype(BF16)
            o_ref[:, 128:] = (y2 * scale).astype(BF16)

    ca = sa = cc = sc2 = None
    if rope:
        ca, sa, cc, sc2 = ca_ref, sa_ref, cc_ref, sc2_ref
    store_rot(qa_o, proj(OFF_QA), ca, sa, ATTN_Q_SCALE)
    store_rot(ka_o, proj(OFF_KA), ca, sa, 1.0)
    va_o[...] = proj(OFF_VA).astype(BF16)
    ub_o[...] = _glu(proj(OFF_UB, 2 * GROUP_W))
    store_rot(qc_o, proj(OFF_QC), cc, sc2, DIFF_Q_SCALE, channel_major=True)
    store_rot(kc_o, proj(OFF_KC), cc, sc2, 1.0)
    vc_o[...] = proj(OFF_VC).T.astype(BF16)
    qd_o[...] = (proj(OFF_QD) * ATTN_Q_SCALE).astype(BF16)
    kd_o[...] = proj(OFF_KD).astype(BF16)
    vd_o[...] = proj(OFF_VD).astype(BF16)


def _mod_spec(d, chunk, mod_rows, tm, lag=0):
    first_row, rows_per_mod = mod_rows
    tiles_per_mod = rows_per_mod // tm
    return pl.BlockSpec(
        (1, 1, d), lambda i: ((first_row + jnp.maximum(i - lag, 0) // tiles_per_mod) * 6 + chunk, 0, 0))


def _in_projection(x, g, mods, mod_rows, w, rope_tables, cast_weights=(), layer=0):
    rows, d = x.shape
    tm = PROJ_ROW_TILE
    steps = rows // tm
    rope = rope_tables is not None
    in_specs = [
        pl.BlockSpec((tm, d), lambda i: (i, 0)),
        pl.BlockSpec((1, d), lambda i: (0, 0)),
        _mod_spec(d, 0, mod_rows, tm), _mod_spec(d, 1, mod_rows, tm),
        _resident((d, PROJ_WIDTH)),
    ]
    args = [x, g.reshape(1, d), mods, mods, w]
    if rope:
        tiles_per_seq = rope_tables[0].shape[0] // tm
        tab_spec = pl.BlockSpec((tm, 128), lambda i: (i % tiles_per_seq, 0))
        in_specs += [tab_spec] * 4
        args += list(rope_tables)
    narrow = pl.BlockSpec((tm, GROUP_W), lambda i: (i, 0))
    chan_major = pl.BlockSpec((GROUP_W, tm), lambda i: (0, i))
    out_specs = [narrow, narrow, narrow, narrow, chan_major, narrow, chan_major, narrow, narrow, narrow]
    bf = jax.ShapeDtypeStruct((rows, GROUP_W), BF16)
    bf_t = jax.ShapeDtypeStruct((GROUP_W, rows), BF16)
    out_shape = [bf, bf, bf, jax.ShapeDtypeStruct((rows, GROUP_W), F32), bf_t, bf, bf_t, bf, bf, bf]
    for cw in cast_weights:
        _, r_w, c_w = cw.shape
        blk_rows = next(n for n in range(BF16_SUBLANES, r_w + 1, BF16_SUBLANES)
                        if r_w % n == 0 and steps % (r_w // n) == 0 and r_w // n <= steps)
        steps_per_blk = steps // (r_w // blk_rows)
        in_specs.append(pl.BlockSpec((1, blk_rows, c_w), lambda i, s=steps_per_blk: (layer, i // s, 0)))
        args.append(cw)
        out_specs.append(pl.BlockSpec((blk_rows, c_w), lambda i, s=steps_per_blk: (i // s, 0)))
        out_shape.append(jax.ShapeDtypeStruct((r_w, c_w), BF16))
    return pl.pallas_call(
        functools.partial(_inproj_kernel, rope, len(cast_weights)),
        grid=(steps,),
        in_specs=in_specs,
        out_specs=out_specs,
        out_shape=out_shape,
        compiler_params=_compiler_params(),
        name="in_projection",
    )(*args)


def _lane_iota():
    return lax.broadcasted_iota(jnp.int32, (1, GROUP_W), 1)


def _split_half_head(lane):
    return jnp.right_shift(jnp.bitwise_and(lane, 127), 5)


def _natural_head(lane):
    return jnp.right_shift(lane, 6)


def _stack_heads(q, lane_group, groups):
    zero = jnp.zeros_like(q)
    return jnp.concatenate([jnp.where(lane_group == g, q, zero) for g in groups], axis=0)


def _softmax_pv(scores, values, sink_col=None):
    m = functools.reduce(jnp.maximum, [jnp.max(s, axis=-1, keepdims=True) for s in scores])
    if sink_col is not None:
        m = jnp.maximum(m, sink_col)
    es = [jnp.exp2(s - m) for s in scores]
    l = functools.reduce(jnp.add, [jnp.sum(e, axis=-1, keepdims=True) for e in es])
    if sink_col is not None:
        l = l + jnp.exp2(sink_col - m)
    o = functools.reduce(jnp.add, [_dot(e.astype(BF16), v) for e, v in zip(es, values)])
    return o / l


def _select_heads(o, tq):
    head = _natural_head(_lane_iota())
    out = o[0:tq]
    for h in range(1, N_HEADS):
        out = jnp.where(head == h, o[h * tq:(h + 1) * tq], out)
    return out


def _sink_column(sink_ref, tq):
    row = lax.broadcasted_iota(jnp.int32, (N_HEADS * tq, 1), 0)
    col = jnp.full((N_HEADS * tq, 1), sink_ref[N_HEADS - 1], F32)
    for h in range(N_HEADS - 2, -1, -1):
        col = jnp.where(row < (h + 1) * tq, sink_ref[h], col)
    return col * LOG2E


def _win_attn_kernel(sink_ref, q_ref, kl_ref, vl_ref, kc_ref, vc_ref, o_ref):
    seq = q_ref.shape[1]
    lane_head = _split_half_head(_lane_iota())
    sink_col = _sink_column(sink_ref, BLK)
    shape = (BLK, 3 * BLK)
    rel = lax.broadcasted_iota(jnp.int32, shape, 1) - lax.broadcasted_iota(jnp.int32, shape, 0)
    bands = {}
    for n in range(seq // BLK):
        start = min(max((n - 1) * BLK, 0), seq - 3 * BLK)
        shift = start - n * BLK
        if shift not in bands:
            bands[shift] = jnp.where(jnp.abs(rel + shift) <= A_WINDOW, 0.0, NEG_INF)
        qs = _stack_heads(q_ref[0, n * BLK:(n + 1) * BLK, :], lane_head, range(N_HEADS))
        kl = kl_ref[0, start:start + 3 * BLK, :]
        vl = vl_ref[0, start:start + 3 * BLK, :]
        s_loc = _dot_nt(qs, kl)
        s_ctx = _dot_nt(qs, kc_ref[0])
        s_loc = (s_loc.reshape(N_HEADS, BLK, 3 * BLK) + bands[shift][None]).reshape(N_HEADS * BLK, 3 * BLK)
        o = _softmax_pv([s_loc, s_ctx], [vl, vc_ref[0]], sink_col)
        o_ref[0, n * BLK:(n + 1) * BLK, :] = _select_heads(o, BLK).astype(BF16)


def _window_attention(sink, q, k, v, kc, vc):
    b, seq, w = q.shape
    ctx_len = kc.shape[1]
    whole = pl.BlockSpec((1, seq, w), lambda i: (i, 0, 0))
    ctx_spec = pl.BlockSpec((1, ctx_len, w), lambda i: (i, 0, 0))
    return pl.pallas_call(
        _win_attn_kernel,
        grid=(b,),
        in_specs=[pl.BlockSpec(memory_space=pltpu.SMEM), whole, whole, whole, ctx_spec, ctx_spec],
        out_specs=whole,
        out_shape=jax.ShapeDtypeStruct((b, seq, w), BF16),
        compiler_params=_compiler_params(),
        name="window_attention",
    )(sink, q, k, v, kc, vc)


def _na_bias_table(rpb):
    cq = np.arange(GRID_W)
    cs = np.clip(cq - NA_KW // 2, 0, GRID_W - NA_KW)
    col_valid = (cq[None, :] >= cs[:, None]) & (cq[None, :] < cs[:, None] + NA_KW)
    dc = np.clip(cq[None, :] - cq[:, None], -(NA_KW - 1), NA_KW - 1) + (NA_KW - 1)
    n_dr, n_dc = 2 * NA_KH - 1, 2 * NA_KW - 1
    onehot = (dc.reshape(1, -1) == np.arange(n_dc)[:, None]).astype(np.float32)
    t = jnp.dot(rpb.astype(F32).reshape(N_HEADS * n_dr, n_dc), onehot, precision=lax.Precision.HIGHEST)
    t = jnp.where(col_valid[None, None], t.reshape(N_HEADS, n_dr, GRID_W, GRID_W) * LOG2E, NEG_INF)
    t = jnp.transpose(t, (1, 0, 2, 3)).reshape(n_dr, N_HEADS * GRID_W, GRID_W)
    return jnp.concatenate([t[:-1], t[1:]], axis=-1)


def _na_attn_kernel(q_ref, kl_ref, vl_ref, kc_ref, vc_ref, bias_ref, o_ref):
    rows = q_ref.shape[1] // GRID_W
    lane_head = _natural_head(_lane_iota())
    for r in range(rows):
        first = min(max(r - NA_KH // 2, 0), rows - NA_KH)
        start = first * GRID_W
        off = first - r + NA_KH - 1
        qs = _stack_heads(q_ref[0, r * GRID_W:(r + 1) * GRID_W, :], lane_head, range(N_HEADS))
        kl = kl_ref[0, start:start + NA_KH * GRID_W, :]
        vl = vl_ref[0, start:start + NA_KH * GRID_W, :]
        bias = jnp.concatenate([bias_ref[off + 2 * i] for i in range(NA_KH // 2)], axis=1)
        s_loc = _dot_nt(qs, kl) + bias
        s_ctx = _dot_nt(qs, kc_ref[0])
        o = _softmax_pv([s_loc, s_ctx], [vl, vc_ref[0]])
        o_ref[0, r * GRID_W:(r + 1) * GRID_W, :] = _select_heads(o, GRID_W).astype(BF16)


def _neighbourhood_attention(bias, q, k, v, kc, vc):
    b, seq, w = q.shape
    ctx_len = kc.shape[1]
    whole = pl.BlockSpec((1, seq, w), lambda i: (i, 0, 0))
    ctx_spec = pl.BlockSpec((1, ctx_len, w), lambda i: (i, 0, 0))
    return pl.pallas_call(
        _na_attn_kernel,
        grid=(b,),
        in_specs=[whole, whole, whole, ctx_spec, ctx_spec, _resident(bias.shape)],
        out_specs=whole,
        out_shape=jax.ShapeDtypeStruct((b, seq, w), BF16),
        compiler_params=_compiler_params(),
        name="neighbourhood_attention",
    )(q, k, v, kc, vc, bias)


def _ctx_attn_kernel(split_layout, has_sink, *refs):
    if has_sink:
        sink_ref, q_ref, k_ref, v_ref, o_ref = refs
    else:
        q_ref, k_ref, v_ref, o_ref = refs
    tq = q_ref.shape[1]
    lane = _lane_iota()
    group = _split_half_head(lane) if split_layout else _natural_head(lane)
    qs = _stack_heads(q_ref[0], group, range(N_HEADS))
    s = _dot_nt(qs, k_ref[0])
    sink_col = _sink_column(sink_ref, tq) if has_sink else None
    o = _softmax_pv([s], [v_ref[0]], sink_col)
    o_ref[0] = _select_heads(o, tq).astype(BF16)


def _context_attention(q, k, v, sink=None, split_layout=False):
    b, n, w = q.shape
    spec = pl.BlockSpec((1, n, w), lambda i: (i, 0, 0))
    in_specs = [spec, spec, spec]
    args = [q, k, v]
    if sink is not None:
        in_specs = [pl.BlockSpec(memory_space=pltpu.SMEM)] + in_specs
        args = [sink] + args
    return pl.pallas_call(
        functools.partial(_ctx_attn_kernel, split_layout, sink is not None),
        grid=(b,),
        in_specs=in_specs,
        out_specs=spec,
        out_shape=jax.ShapeDtypeStruct((b, n, w), BF16),
        compiler_params=_compiler_params(),
        name="context_attention",
    )(*args)


def _diff_attn_kernel(has_local, lambda_init, lam_ref, subg_ref, qt_ref, *refs):
    if has_local:
        kl_ref, vtl_ref, kc_ref, vtc_ref, o_ref = refs
    else:
        kc_ref, vtc_ref, o_ref = refs
    tq = min(DIFF_TQ, qt_ref.shape[1])
    n_blocks = qt_ref.shape[1] // tq
    lv = lam_ref[...]
    lam = (jnp.exp(jnp.sum(lv[0:1] * lv[1:2], axis=-1, keepdims=True))
           - jnp.exp(jnp.sum(lv[2:3] * lv[3:4], axis=-1, keepdims=True)) + lambda_init)
    chan = lax.broadcasted_iota(jnp.int32, (GROUP_W, 1), 0)
    group = jnp.right_shift(jnp.bitwise_and(chan, 127), 4)
    zero = jnp.zeros((GROUP_W, tq), BF16)
    segs = [(kc_ref, vtc_ref, 0, kc_ref.shape[1])]
    if has_local:
        n_loc = kl_ref.shape[1]
        segs = [(kl_ref, vtl_ref, k0, DIFF_KEY_SEG) for k0 in range(0, n_loc, DIFF_KEY_SEG)] + segs
    sum_rows = 16

    def head_scores(unit):
        blk, h = unit
        qt = qt_ref[:, blk * tq:(blk + 1) * tq]
        qs = jnp.concatenate([jnp.where(group == 2 * h, qt, zero),
                              jnp.where(group == 2 * h + 1, qt, zero)], axis=1)
        return [_dot(k_ref[0, k0:k0 + nk, :], qs) for k_ref, _, k0, nk in segs]

    def column_max(s, slab=64):
        part = jnp.max(s.reshape(s.shape[0] // slab, slab, s.shape[1]), axis=0)
        return jnp.max(part, axis=0, keepdims=True)

    ys = []
    units = [(blk, h) for blk in range(n_blocks) for h in range(N_HEADS)]
    nxt = head_scores(units[0])
    for ui, (blk, h) in enumerate(units):
        scores = nxt
        if ui + 1 < len(units):
            nxt = head_scores(units[ui + 1])
        m = functools.reduce(jnp.maximum, [column_max(s) for s in scores])
        pv = None
        for s, (_, vt_ref, k0, nk) in zip(scores, segs):
            e = jnp.exp2(s - m).astype(BF16)
            vt = vt_ref[h * HEAD_DIM:(h + 1) * HEAD_DIM, k0:k0 + nk]
            lhs = jnp.concatenate([vt, jnp.ones((sum_rows, nk), BF16)], axis=0)
            p = _dot(lhs, e)
            pv = p if pv is None else pv + p
        inv = 1.0 / pv[HEAD_DIM:HEAD_DIM + 1]
        pv = pv[:HEAD_DIM]
        o = pv[:, :tq] * inv[:, :tq] - pv[:, tq:] * (lam * inv[:, tq:])
        ms = jnp.mean(o * o, axis=0, keepdims=True)
        ys.append(o * lax.rsqrt(ms + EPS))
        if h == N_HEADS - 1:
            out = jnp.concatenate(ys, axis=0).T
            o_ref[0, blk * tq:(blk + 1) * tq, :] = (out * subg_ref[...] * (1.0 - lambda_init)).astype(BF16)
            ys = []


def _diff_attention(lam_vecs, subg, lambda_init, b, qt, kc, vtc, k=None, vt=None):
    w = qt.shape[0]
    n = qt.shape[1] // b
    ctx_len = kc.shape[1]
    has_local = k is not None
    tq = min(DIFF_TQ * DIFF_BLOCKS_PER_STEP, n)
    steps = n // tq
    q_spec = pl.BlockSpec((w, tq), lambda i, j: (0, i * steps + j))
    in_specs = [pl.BlockSpec(lam_vecs.shape, lambda i, j: (0, 0)),
                pl.BlockSpec((1, w), lambda i, j: (0, 0)), q_spec]
    args = [lam_vecs, subg, qt]
    if has_local:
        in_specs += [pl.BlockSpec((1, n, w), lambda i, j: (i, 0, 0)),
                     pl.BlockSpec((w, n), lambda i, j: (0, i))]
        args += [k, vt]
    in_specs += [pl.BlockSpec((1, ctx_len, w), lambda i, j: (i, 0, 0)),
                 pl.BlockSpec((w, ctx_len), lambda i, j: (0, i))]
    args += [kc, vtc]
    return pl.pallas_call(
        functools.partial(_diff_attn_kernel, has_local, lambda_init),
        grid=(b, n // tq),
        in_specs=in_specs,
        out_specs=pl.BlockSpec((1, tq, w), lambda i, j: (i, j, 0)),
        out_shape=jax.ShapeDtypeStruct((b, n, w), BF16),
        compiler_params=_compiler_params(),
        name="diff_attention",
    )(*args)


CONV_WIN = CONV_TC + 2 * CONV_HALO


def _glu(u):
    return u[:, :GROUP_W] * _sigmoid(u[:, GROUP_W:])


def _conv_chunk(win, w_ref, b_ref, g_ref, beta_ref):
    halo, tc = CONV_HALO, CONV_TC
    acc = jnp.zeros((tc, win.shape[1]), F32)
    for sub in range(8):
        shifted = win if sub == 0 else pltpu.roll(win, CONV_WIN - sub, axis=0)
        for blk8 in range(CONV_WIN // 8):
            tap = 8 * blk8 + sub - (halo - CONV_K // 2)
            if 0 <= tap < CONV_K:
                acc = acc + shifted[8 * blk8:8 * blk8 + tc] * w_ref[tap:tap + 1, :]
    hcv = acc + b_ref[...]
    mu = jnp.mean(hcv, axis=-1, keepdims=True)
    cen = hcv - mu
    var = jnp.mean(cen * cen, axis=-1, keepdims=True)
    y = cen * lax.rsqrt(var + EPS) * g_ref[...] + beta_ref[...]
    return y * _sigmoid(y)


def _conv_kernel(seq, hg_ref, w_ref, b_ref, g_ref, beta_ref, o_ref, pad_ref):
    halo, tc, ch = CONV_HALO, CONV_TC, GROUP_W
    zeros = jnp.zeros((halo, ch), F32)
    pad_ref[0:halo, :] = zeros
    pad_ref[halo:halo + seq, :] = hg_ref[0]
    pad_ref[halo + seq:2 * halo + seq, :] = zeros

    def chunk(i, carry):
        c0 = pl.multiple_of(i * tc, tc)
        win = pad_ref[pl.ds(c0, CONV_WIN), :]
        o_ref[0, pl.ds(c0, tc), :] = _conv_chunk(win, w_ref, b_ref, g_ref, beta_ref).astype(BF16)
        return carry

    lax.fori_loop(0, seq // tc, chunk, 0, unroll=4)


def _conformer_conv(hg, w, bias, ln_g, ln_b):
    b, seq, ch = hg.shape
    vec = pl.BlockSpec((1, ch), lambda i: (0, 0))
    return pl.pallas_call(
        functools.partial(_conv_kernel, seq),
        grid=(b,),
        in_specs=[pl.BlockSpec((1, seq, ch), lambda i: (i, 0, 0)),
                  pl.BlockSpec((CONV_K, ch), lambda i: (0, 0)), vec, vec, vec],
        out_specs=pl.BlockSpec((1, seq, ch), lambda i: (i, 0, 0)),
        out_shape=jax.ShapeDtypeStruct((b, seq, ch), BF16),
        scratch_shapes=[pltpu.VMEM((seq + 2 * CONV_HALO, ch), F32)],
        compiler_params=_compiler_params(),
        name="conformer_conv",
    )(hg, w, bias.reshape(1, ch), ln_g.reshape(1, ch), ln_b.reshape(1, ch))


def _fused_conv_pad(tiles_per_seq, u_ref, up_ref, un_ref, pad_ref):
    halo, tm = CONV_HALO, u_ref.shape[0]
    n_tiles = pl.num_programs(0) - 1
    pos = lax.rem(jnp.minimum(pl.program_id(0), n_tiles - 1), tiles_per_seq)
    pad_ref[0:halo, :] = jnp.where(pos > 0, up_ref[...], 0.0)
    pad_ref[halo:halo + tm, :] = u_ref[...]
    pad_ref[halo + tm:2 * halo + tm, :] = jnp.where(pos < tiles_per_seq - 1, un_ref[...], 0.0)


def _ffn_kernel(final, conv_tiles_per_seq, x_ref, ya_ref, *rest):
    if conv_tiles_per_seq is None:
        yb_ref, rest = rest[0], rest[1:]
    else:
        conv_refs, rest = rest[:7], rest[7:]
    (yc_ref, yd_ref, g1_ref, sh_ref, sc_ref, g2_ref, ng_ref, wo_ref, wg_ref, wu_ref, wd_ref), rest = rest[:11], rest[11:]
    if final:
        fg_ref, rest = rest[0], rest[1:]
    o_ref, ycat_ref, act_ref = rest[:3]
    if conv_tiles_per_seq is not None:
        yb_ref, pad_ref = rest[3:]

        @pl.when(pl.program_id(0) == 0)
        def _():
            yb_ref[...] = jnp.zeros(yb_ref.shape, BF16)

    for j, y_ref in enumerate((ya_ref, yb_ref, yc_ref, yd_ref)):
        ycat_ref[:, j * GROUP_W:(j + 1) * GROUP_W] = y_ref[...]
    conv_chunks = []
    if conv_tiles_per_seq is not None:
        u_ref, up_ref, un_ref, cw_ref, cb_ref, cg_ref, cbeta_ref = conv_refs
        _fused_conv_pad(conv_tiles_per_seq, u_ref, up_ref, un_ref, pad_ref)
        conv_chunks = list(range(0, u_ref.shape[0], CONV_TC))
    n_conv_chunks = len(conv_chunks)
    x = x_ref[...] + g1_ref[0] * _dot(ycat_ref[...], wo_ref[...])
    r = lax.rsqrt(jnp.mean(x * x, axis=-1, keepdims=True) + EPS)
    h = (x * r) * ng_ref[...]
    hb = (h * (1.0 + sc_ref[0]) + sh_ref[0]).astype(BF16)
    for c in range(0, FFN_HIDDEN, FFN_CHUNK):
        gate = _dot(hb, wg_ref[:, c:c + FFN_CHUNK])
        up = _dot(hb, wu_ref[:, c:c + FFN_CHUNK])
        act_ref[:, c:c + FFN_CHUNK] = (gate * _sigmoid(gate) * up).astype(BF16)
        n_ffn_chunks = FFN_HIDDEN // FFN_CHUNK
        due = -(-(c // FFN_CHUNK + 1) * n_conv_chunks // n_ffn_chunks)
        while n_conv_chunks - len(conv_chunks) < due:
            c0 = conv_chunks.pop(0)
            win = pad_ref[c0:c0 + CONV_WIN, :]
            yconv = _conv_chunk(win, cw_ref, cb_ref, cg_ref, cbeta_ref)
            yb_ref[c0:c0 + CONV_TC, :] = yconv.astype(BF16)
            anchor = (slice(0, BF16_SUBLANES), slice(c, c + 128))
            act_ref[anchor] = (act_ref[anchor].astype(F32) + 0.0 * yconv[:BF16_SUBLANES, :128]).astype(BF16)
    assert not conv_chunks
    x = x + g2_ref[0] * _dot(act_ref[...], wd_ref[...])
    if final:
        r = lax.rsqrt(jnp.mean(x * x, axis=-1, keepdims=True) + EPS)
        x = (x * r) * fg_ref[...]
    o_ref[...] = x


def _out_projection_ffn(x, ys, mods, mod_rows, norm_g, wo, wg, wu, wd, final_g=None, conv=None):
    rows, d = x.shape
    tm = ROW_TILE
    n_tiles = rows // tm
    final = final_g is not None
    lag = 0 if conv is None else 1

    def tile(i):
        return jnp.maximum(i - lag, 0)

    row_spec = pl.BlockSpec((tm, d), lambda i: (tile(i), 0))
    y_spec = pl.BlockSpec((tm, GROUP_W), lambda i: (tile(i), 0))
    vec_spec = pl.BlockSpec((1, d), lambda i: (0, 0))
    scratch = [pltpu.VMEM((tm, d), BF16), pltpu.VMEM((tm, FFN_HIDDEN), BF16)]
    if conv is None:
        yb_specs, yb_args, tiles_per_seq = [y_spec], [ys[1]], None
    else:
        u, cw, cb, cg, cbeta, seq = conv
        ch = cw.shape[1]
        halo_blocks = tm // CONV_HALO

        def conv_tile(i):
            return jnp.minimum(i, n_tiles - 1)

        cvec = pl.BlockSpec((1, ch), lambda i: (0, 0))
        yb_specs = [
            pl.BlockSpec((tm, ch), lambda i: (conv_tile(i), 0)),
            pl.BlockSpec((CONV_HALO, ch), lambda i: (jnp.maximum(conv_tile(i) * halo_blocks - 1, 0), 0)),
            pl.BlockSpec((CONV_HALO, ch),
                         lambda i: (jnp.minimum((conv_tile(i) + 1) * halo_blocks, rows // CONV_HALO - 1), 0)),
            pl.BlockSpec(cw.shape, lambda i: (0, 0)), cvec, cvec, cvec]
        yb_args = [u, u, u, cw, cb.reshape(1, ch), cg.reshape(1, ch), cbeta.reshape(1, ch)]
        tiles_per_seq = seq // tm
        scratch += [pltpu.VMEM((tm, ch), BF16), pltpu.VMEM((tm + 2 * CONV_HALO, ch), F32)]
    in_specs = ([row_spec, y_spec] + yb_specs + [y_spec, y_spec]
                + [_mod_spec(d, chunk, mod_rows, tm, lag) for chunk in (2, 3, 4, 5)]
                + [vec_spec, _resident(wo.shape), _resident(wg.shape), _resident(wu.shape), _resident(wd.shape)])
    args = [x, ys[0], *yb_args, ys[2], ys[3], mods, mods, mods, mods, norm_g.reshape(1, d), wo, wg, wu, wd]
    if final:
        in_specs.append(vec_spec)
        args.append(final_g.reshape(1, d))
    return pl.pallas_call(
        functools.partial(_ffn_kernel, final, tiles_per_seq),
        grid=(n_tiles + lag,),
        in_specs=in_specs,
        out_specs=row_spec,
        out_shape=jax.ShapeDtypeStruct((rows, d), F32),
        scratch_shapes=scratch,
        compiler_params=_compiler_params(),
        name="out_projection_ffn",
    )(*args)


def _rope_tables(n_tok, dim):
    t = np.arange(n_tok)
    row = (t // GRID_W).astype(np.float32)
    col = (t % GRID_W).astype(np.float32)
    nf = dim // 4
    inv = (np.float32(ROPE_BASE) ** (-np.arange(nf, dtype=np.float32) / np.float32(nf))).astype(np.float32)
    ang = np.concatenate([row[:, None] * inv, col[:, None] * inv], axis=-1).astype(np.float64)
    reps = 128 // (dim // 2)
    return (jnp.asarray(np.tile(np.cos(ang), (1, reps)), F32), jnp.asarray(np.tile(np.sin(ang), (1, reps)), F32))


def kernel(x, c, ctx, c_ctx, norm1_g, norm2_g, w_ada, b_ada, w_in, w_out, attn_sink, conv_w, conv_b,
           conv_ln_g, conv_ln_b, diff_lq1, diff_lk1, diff_lq2, diff_lk2, diff_subln_g, na_rpb,
           w_gate, w_up, w_down, final_g):
    batch, seq, d = x.shape
    ctx_len = ctx.shape[1]
    depth = w_ada.shape[0]

    cvec = jnp.zeros((ADA_ROWS, d), F32).at[:batch].set(c).at[batch].set(c_ctx)
    mods_all = _ada_table(cvec, w_ada, b_ada).reshape(depth, ADA_ROWS * 6, 1, d)
    rope = _rope_tables(seq, HEAD_DIM) + _rope_tables(seq, C_QK_DIM)
    lat_rows = (0, seq)
    ctx_rows = (batch, batch * ctx_len)

    xl = x.reshape(batch * seq, d)
    xc = ctx.reshape(batch * ctx_len, d)
    for l in range(depth):
        ctx_needed = l < depth - 1
        mods = mods_all[l]
        lambda_init = 0.8 - 0.6 * math.exp(-0.3 * l)
        w_in_l = _relayout_w_in(w_in, l)
        lam_vecs = jnp.stack([diff_lq1[l], diff_lk1[l], diff_lq2[l], diff_lk2[l]]).astype(F32)
        subg = jnp.tile(diff_subln_g[l], N_HEADS).reshape(1, GROUP_W)
        cw = conv_w[l].reshape(CONV_K, GROUP_W)

        def per_batch(tensors, n_tok):
            return [t if i in (4, 6) else t.reshape(batch, n_tok, -1) for i, t in enumerate(tensors)]

        lat_out = _in_projection(xl, norm1_g[l], mods, lat_rows, w_in_l, rope,
                                 cast_weights=(w_out, w_gate, w_up, w_down), layer=l)
        qa, ka, va, ub, qct, kc, vct, qd, kd, vd = per_batch(lat_out[:10], seq)
        wo, wg, wu, wd = lat_out[10:]
        qa_c, ka_c, va_c, ub_c, qct_c, kc_c, vct_c, qd_c, kd_c, vd_c = per_batch(
            _in_projection(xc, norm1_g[l], mods, ctx_rows, w_in_l, None), ctx_len)

        y_a = _window_attention(attn_sink[l], qa, ka, va, ka_c, va_c)
        y_c = _diff_attention(lam_vecs, subg, lambda_init, batch, qct, kc_c, vct_c, kc, vct)
        y_d = _neighbourhood_attention(_na_bias_table(na_rpb[l]), qd, kd, vd, kd_c, vd_c)
        ys = [None if t is None else t.reshape(batch * seq, GROUP_W) for t in (y_a, None, y_c, y_d)]
        conv = (ub.reshape(batch * seq, GROUP_W), cw, conv_b[l], conv_ln_g[l], conv_ln_b[l], seq)
        xl = _out_projection_ffn(xl, ys, mods, lat_rows, norm2_g[l], wo, wg, wu, wd,
                                 final_g=None if ctx_needed else final_g, conv=conv)
        if ctx_needed:
            yc_a = _context_attention(qa_c, ka_c, va_c, sink=attn_sink[l], split_layout=True)
            yc_b = _conformer_conv(ub_c, cw, conv_b[l], conv_ln_g[l], conv_ln_b[l])
            yc_c = _diff_attention(lam_vecs, subg, lambda_init, batch, qct_c, kc_c, vct_c)
            yc_d = _context_attention(qd_c, kd_c, vd_c)
            ycs = [t.reshape(batch * ctx_len, GROUP_W) for t in (yc_a, yc_b, yc_c, yc_d)]
            xc = _out_projection_ffn(xc, ycs, mods, ctx_rows, norm2_g[l], wo, wg, wu, wd)
    return xl.reshape(batch, seq, d)
```
